```python
import math
import jax, jax.numpy as jnp
from jax import lax
import numpy as np


D_MODEL = 1024
BATCH = 8
SEQ = 4096
DEPTH = 1
DEC_BATCH = 4
DEC_SEQ = 4096
PAST_LEN = 128

D_HYENA = D_MODEL // 2
HYENA_ORDER = 2
SHORT_CONV = 3
FILTER_EMB = 33
FILTER_ORDER = 64
N_FILTER_INNER = 2
N_DIRECTIONS = 2
N_FILT_CH = N_DIRECTIONS * HYENA_ORDER * D_HYENA
FAST_DECAY_PCT = 0.3
SLOW_DECAY_PCT = 1.5
DECAY_TARGET = 1e-2
FILTER_OUT_SCALE = 0.05
N_Q_HEADS = 8
N_KV_HEADS = 2
HEAD_DIM = 64
D_ATTN = N_Q_HEADS * HEAD_DIM
D_KV = N_KV_HEADS * HEAD_DIM
Q_PER_KV = N_Q_HEADS // N_KV_HEADS
WINDOW = 128
BLOCK = 128
ROPE_THETA = 500000.0
ROT_DIM = HEAD_DIM // 4
D_IN = (HYENA_ORDER + 1) * D_HYENA + D_ATTN + 2 * D_KV + 2 * D_MODEL
D_FF = (8 * D_MODEL + 3 * 256 - 1) // (3 * 256) * 256
EPS = 1e-6
NEG_INF = -1e30

kernel_name = 'hybrid_hyena_swa_gated_encoder'


def rmsnorm(x, g):
    xf = x.astype(jnp.float32)
    y = xf * lax.rsqrt(jnp.mean(xf * xf, axis=-1, keepdims=True) + EPS)
    return (y * g.astype(jnp.float32)).astype(x.dtype)


def hyena_filters(L, w0, b0, w_inner, b_inner, w_out, freq):
    f32 = jnp.float32
    bands = (FILTER_EMB - 1) // 2
    t = jnp.linspace(0.0, 1.0, L, dtype=f32)[:, None]
    w = 2.0 * math.pi * jnp.arange(L, dtype=f32)[:, None] / L
    f = jnp.linspace(1e-4, bands - 1, bands, dtype=f32)[None, :]
    z = jnp.concatenate([t, jnp.cos(f * w), jnp.sin(f * w)], axis=-1)
    fr = freq.astype(f32)
    h = jnp.sin(fr * (z @ w0.astype(f32) + b0.astype(f32)))
    for i in range(N_FILTER_INNER):
        h = jnp.sin(fr * (h @ w_inner[i].astype(f32) + b_inner[i].astype(f32)))
    h = h @ w_out.astype(f32)
    min_decay = math.log(DECAY_TARGET) / SLOW_DECAY_PCT
    max_decay = math.log(DECAY_TARGET) / FAST_DECAY_PCT
    deltas = jnp.tile(jnp.linspace(min_decay, max_decay, D_HYENA, dtype=f32), N_DIRECTIONS * HYENA_ORDER)
    h = h * jnp.exp(-t * jnp.abs(deltas))
    return h.reshape(L, N_DIRECTIONS, HYENA_ORDER, D_HYENA)


def bidir_fftconv(u, h_fwd, h_bwd):
    L = u.shape[1]
    k = jnp.concatenate([h_fwd, jnp.zeros_like(h_fwd[:1]), h_bwd[:0:-1]], axis=0)
    k_f = jnp.fft.rfft(k, n=2 * L, axis=0)
    u_f = jnp.fft.rfft(u, n=2 * L, axis=1)
    return jnp.fft.irfft(u_f * k_f[None], n=2 * L, axis=1)[:, :L]


def hyena_mixer(u, short_w, short_b, filt, hyena_bias):
    L = u.shape[1]
    pad = SHORT_CONV // 2
    up = jnp.pad(u, ((0, 0), (pad, pad), (0, 0)))
    u = sum(up[:, j:j + L] * short_w[j] for j in range(SHORT_CONV)) + short_b
    v, x1, x2 = jnp.split(u, HYENA_ORDER + 1, axis=-1)
    z = v.astype(jnp.float32)
    for o, gate in enumerate((x1, x2)):
        z = gate.astype(jnp.float32) * (bidir_fftconv(z, filt[:, 0, o], filt[:, 1, o])
                                        + hyena_bias[o].astype(jnp.float32) * z)
    return z.astype(u.dtype)


def partial_rope(x, pos):
    half = ROT_DIM // 2
    inv = 1.0 / (ROPE_THETA ** (jnp.arange(0, ROT_DIM, 2, dtype=jnp.float32) / ROT_DIM))
    ang = pos[:, None] * inv[None, :]
    cos = jnp.cos(ang)[None, :, None, :]
    sin = jnp.sin(ang)[None, :, None, :]
    xr = x[..., :ROT_DIM].astype(jnp.float32)
    a, b = xr[..., :half], xr[..., half:]
    rot = jnp.concatenate([a * cos - b * sin, b * cos + a * sin], axis=-1)
    return jnp.concatenate([rot.astype(x.dtype), x[..., ROT_DIM:]], axis=-1)


def window_attention(q, k, v, sink):
    B, L = q.shape[0], q.shape[1]
    nb = L // BLOCK
    qb = q.reshape(B, nb, BLOCK, N_KV_HEADS, Q_PER_KV, HEAD_DIM)

    def band(t):
        tp = jnp.pad(t, ((0, 0), (BLOCK, BLOCK), (0, 0), (0, 0))).reshape(B, nb + 2, BLOCK, N_KV_HEADS, HEAD_DIM)
        return jnp.concatenate([tp[:, 0:nb], tp[:, 1:nb + 1], tp[:, 2:nb + 2]], axis=2)

    kb, vb = band(k), band(v)
    s = jnp.einsum('bnqhgd,bnkhd->bnhgqk', qb, kb).astype(jnp.float32) * (HEAD_DIM ** -0.5)
    qpos = jnp.arange(nb)[:, None] * BLOCK + jnp.arange(BLOCK)[None, :]
    kpos = jnp.arange(nb)[:, None] * BLOCK - BLOCK + jnp.arange(3 * BLOCK)[None, :]
    valid = ((jnp.abs(qpos[:, :, None] - kpos[:, None, :]) <= WINDOW)
             & (kpos[:, None, :] >= 0) & (kpos[:, None, :] < L))
    s = jnp.where(valid[None, :, None, None], s, NEG_INF)
    sk = sink.astype(jnp.float32).reshape(1, 1, N_KV_HEADS, Q_PER_KV, 1, 1)
    m = jnp.maximum(jnp.max(s, axis=-1, keepdims=True), sk)
    p = jnp.exp(s - m)
    p = p / (jnp.sum(p, axis=-1, keepdims=True) + jnp.exp(sk - m))
    o = jnp.einsum('bnhgqk,bnkhd->bnqhgd', p.astype(v.dtype), vb)
    return o.reshape(B, L, D_ATTN)


def encoder_layer(x, norm1_g, w_in, short_w, short_b, filt_w0, filt_b0, filt_w_inner, filt_b_inner,
                  filt_w_out, filt_freq, hyena_bias, sink_logit, w_up_hyena, w_up_attn, w_out,
                  norm2_g, w_ff_gate, w_ff_up, w_ff_down):
    B, L, _ = x.shape
    h = rmsnorm(x, norm1_g)
    proj = h @ w_in
    splits = np.cumsum([(HYENA_ORDER + 1) * D_HYENA, D_ATTN, D_KV, D_KV, D_MODEL]).tolist()
    u_h, q, k, v, g_h, g_a = jnp.split(proj, splits, axis=-1)
    filt = hyena_filters(L, filt_w0, filt_b0, filt_w_inner, filt_b_inner, filt_w_out, filt_freq)
    y_h = hyena_mixer(u_h, short_w, short_b, filt, hyena_bias)
    pos = jnp.arange(L, dtype=jnp.float32)
    q = partial_rope(q.reshape(B, L, N_Q_HEADS, HEAD_DIM), pos)
    k = partial_rope(k.reshape(B, L, N_KV_HEADS, HEAD_DIM), pos)
    y_a = window_attention(q, k, v.reshape(B, L, N_KV_HEADS, HEAD_DIM), sink_logit)
    merged = jax.nn.sigmoid(g_h) * (y_h @ w_up_hyena) + jax.nn.sigmoid(g_a) * (y_a @ w_up_attn)
    x = x + merged @ w_out
    h = rmsnorm(x, norm2_g)
    x = x + (jax.nn.silu(h @ w_ff_gate) * (h @ w_ff_up)) @ w_ff_down
    return x


def encoder_trunk(x, norm1_g, w_in, short_w, short_b, filt_w0, filt_b0, filt_w_inner, filt_b_inner,
                  filt_w_out, filt_freq, hyena_bias, sink_logit, w_up_hyena, w_up_attn, w_out,
                  norm2_g, w_ff_gate, w_ff_up, w_ff_down, final_g):
    for l in range(DEPTH):
        x = encoder_layer(x, norm1_g[l], w_in[l], short_w[l], short_b[l], filt_w0[l], filt_b0[l],
                          filt_w_inner[l], filt_b_inner[l], filt_w_out[l], filt_freq[l], hyena_bias[l],
                          sink_logit[l], w_up_hyena[l], w_up_attn[l], w_out[l], norm2_g[l],
                          w_ff_gate[l], w_ff_up[l], w_ff_down[l])
    return rmsnorm(x, final_g)


def setup_inputs(seed: int = 0) -> dict:
    key = jax.random.key(seed)
    ks = jax.random.split(key, 24)
    f32 = jnp.float32

    def nrm(k, shape, scale):
        return jax.random.normal(k, shape, f32) * scale

    D3 = (HYENA_ORDER + 1) * D_HYENA
    return {
        'x_prompt': nrm(ks[0], (BATCH, SEQ, D_MODEL), 1.0),
        'x_sample': nrm(ks[1], (DEC_BATCH, DEC_SEQ, D_MODEL), 1.0),
        'norm1_g': 1.0 + nrm(ks[2], (DEPTH, D_MODEL), 0.01),
        'w_in': nrm(ks[3], (DEPTH, D_MODEL, D_IN), D_MODEL ** -0.5),
        'short_w': nrm(ks[4], (DEPTH, SHORT_CONV, D3), SHORT_CONV ** -0.5),
        'short_b': nrm(ks[5], (DEPTH, D3), 0.01),
        'filt_w0': nrm(ks[6], (DEPTH, FILTER_EMB, FILTER_ORDER), FILTER_EMB ** -0.5),
        'filt_b0': nrm(ks[7], (DEPTH, FILTER_ORDER), 0.1),
        'filt_w_inner': nrm(ks[8], (DEPTH, N_FILTER_INNER, FILTER_ORDER, FILTER_ORDER), FILTER_ORDER ** -0.5),
        'filt_b_inner': nrm(ks[9], (DEPTH, N_FILTER_INNER, FILTER_ORDER), 0.1),
        'filt_w_out': nrm(ks[10], (DEPTH, FILTER_ORDER, N_FILT_CH), FILTER_OUT_SCALE * FILTER_ORDER ** -0.5),
        'filt_freq': 1.0 + nrm(ks[11], (DEPTH, FILTER_ORDER), 0.01),
        'hyena_bias': nrm(ks[12], (DEPTH, HYENA_ORDER, D_HYENA), 1.0),
        'sink_logit': nrm(ks[13], (DEPTH, N_Q_HEADS), 0.5),
        'w_up_hyena': nrm(ks[14], (DEPTH, D_HYENA, D_MODEL), D_HYENA ** -0.5),
        'w_up_attn': nrm(ks[15], (DEPTH, D_ATTN, D_MODEL), D_ATTN ** -0.5),
        'w_out': nrm(ks[16], (DEPTH, D_MODEL, D_MODEL), D_MODEL ** -0.5),
        'norm2_g': 1.0 + nrm(ks[17], (DEPTH, D_MODEL), 0.01),
        'w_ff_gate': nrm(ks[18], (DEPTH, D_MODEL, D_FF), D_MODEL ** -0.5),
        'w_ff_up': nrm(ks[19], (DEPTH, D_MODEL, D_FF), D_MODEL ** -0.5),
        'w_ff_down': nrm(ks[20], (DEPTH, D_FF, D_MODEL), D_FF ** -0.5),
        'final_g': 1.0 + nrm(ks[21], (D_MODEL,), 0.01),
    }


def reference(x_prompt, x_sample, norm1_g, w_in, short_w, short_b, filt_w0, filt_b0, filt_w_inner,
              filt_b_inner, filt_w_out, filt_freq, hyena_bias, sink_logit, w_up_hyena, w_up_attn, w_out,
              norm2_g, w_ff_gate, w_ff_up, w_ff_down, final_g):
    weights = (norm1_g, w_in, short_w, short_b, filt_w0, filt_b0, filt_w_inner, filt_b_inner,
               filt_w_out, filt_freq, hyena_bias, sink_logit, w_up_hyena, w_up_attn, w_out,
               norm2_g, w_ff_gate, w_ff_up, w_ff_down, final_g)
    y_prompt = encoder_trunk(x_prompt, *weights)
    y_sample = encoder_trunk(x_sample, *weights)
    return (y_prompt, y_sample)
```

```python
import functools
import math

import jax
import jax.numpy as jnp
from jax import lax
from jax.experimental import pallas as pl
from jax.experimental.pallas import tpu as pltpu

F32 = jnp.float32
BF16 = jnp.bfloat16

D_MODEL = 1024
SEQ = 4096
D_HYENA = 512
HYENA_ORDER = 2
SHORT_CONV = 3
FILTER_EMB = 33
FILTER_ORDER = 64
N_FILT_CH = 2 * HYENA_ORDER * D_HYENA
FAST_DECAY_PCT = 0.3
SLOW_DECAY_PCT = 1.5
DECAY_TARGET = 1e-2
N_Q_HEADS = 8
N_KV_HEADS = 2
HEAD_DIM = 64
D_ATTN = N_Q_HEADS * HEAD_DIM
D_KV = N_KV_HEADS * HEAD_DIM
WINDOW = 128
ROPE_THETA = 500000.0
ROT_DIM = HEAD_DIM // 4
D_UH = (HYENA_ORDER + 1) * D_HYENA
D_IN = D_UH + D_ATTN + 2 * D_KV + 2 * D_MODEL
D_FF = 2816
EPS = 1e-6
NEG_INF = -1e30

LANES = 128
N_FFT = 2 * SEQ
N1 = 128
N2 = N_FFT // N1
H1 = N1 // 2
Z_STRIDE = N2 + 8
A_STRIDE = N1 + 8
CB = LANES
N_CB = D_HYENA // CB
HALO = 16
VMEM_LIMIT = 56 * 1024 * 1024


def _cparams(sem):
    return pltpu.CompilerParams(dimension_semantics=sem, vmem_limit_bytes=VMEM_LIMIT)


def _const_spec(shape):
    nd = len(shape)
    return pl.BlockSpec(shape, lambda *_: (0,) * nd, pipeline_mode=pl.Buffered(1))


def _dft_tables():
    two_pi = 2.0 * math.pi
    n1 = jnp.arange(H1, dtype=jnp.int32)
    n2 = jnp.arange(N2, dtype=jnp.int32)
    k1 = jnp.arange(N1, dtype=jnp.int32)
    m = ((N2 * n1[None, None, :] + n2[:, None, None]) * k1[None, :, None]) % N_FFT
    ang = m.astype(F32) * (two_pi / N_FFT)
    er, ei = jnp.cos(ang), -jnp.sin(ang)
    f1 = jnp.concatenate([jnp.concatenate([er, -ei], axis=2),
                          jnp.concatenate([ei, er], axis=2)], axis=1)
    ert, eit = jnp.swapaxes(er, 1, 2), jnp.swapaxes(ei, 1, 2)
    i2 = jnp.concatenate([jnp.concatenate([ert, eit], axis=2),
                          jnp.concatenate([-eit, ert], axis=2)], axis=1)
    mg = (n2[:, None] * n2[None, :]) % N2
    angg = mg.astype(F32) * (two_pi / N2)
    gr, gi = jnp.cos(angg), -jnp.sin(angg)
    f2 = jnp.concatenate([jnp.concatenate([gr, -gi], axis=1),
                          jnp.concatenate([gi, gr], axis=1)], axis=0)
    i1 = jnp.concatenate([jnp.concatenate([gr, gi], axis=1),
                          jnp.concatenate([-gi, gr], axis=1)], axis=0)
    return f1.astype(BF16), f2.astype(BF16), i1.astype(BF16), i2.astype(BF16)


def _rope_tables():
    half = ROT_DIM // 2
    pos = jnp.arange(SEQ, dtype=F32)
    inv = 1.0 / (ROPE_THETA ** (jnp.arange(0, ROT_DIM, 2, dtype=F32) / ROT_DIM))
    ang = pos[:, None] * inv[None, :]
    cos, sin = jnp.cos(ang), jnp.sin(ang)
    ones = jnp.ones((SEQ, HEAD_DIM - ROT_DIM), F32)
    zeros = jnp.zeros((SEQ, HEAD_DIM - ROT_DIM), F32)
    zh = jnp.zeros((SEQ, half), F32)
    c = jnp.concatenate([cos, cos, ones], axis=1)
    s_up = jnp.concatenate([-sin, zh, zeros], axis=1)
    s_dn = jnp.concatenate([zh, sin, zeros], axis=1)
    rep = LANES // HEAD_DIM
    return jnp.tile(c, (1, rep)), jnp.tile(s_up, (1, rep)), jnp.tile(s_dn, (1, rep))


def _filter_embedding():
    bands = (FILTER_EMB - 1) // 2
    t = jnp.linspace(0.0, 1.0, SEQ, dtype=F32)[:, None]
    w = 2.0 * math.pi * jnp.arange(SEQ, dtype=F32)[:, None] / SEQ
    f = jnp.linspace(1e-4, bands - 1, bands, dtype=F32)[None, :]
    z = jnp.concatenate([t, jnp.cos(f * w), jnp.sin(f * w)], axis=-1)
    z = jnp.pad(z, ((0, 0), (0, LANES - FILTER_EMB)))
    min_decay = math.log(DECAY_TARGET) / SLOW_DECAY_PCT
    max_decay = math.log(DECAY_TARGET) / FAST_DECAY_PCT
    deltas = jnp.tile(jnp.linspace(min_decay, max_decay, D_HYENA, dtype=F32), 2 * HYENA_ORDER)
    return z, jnp.abs(deltas)[None, :]


PROJ_TM = 512


def _proj_kernel(x_ref, g_ref, w_ref, c_ref, su_ref, sd_ref, uh_ref, q_ref, k_ref, v_ref, gate_ref):
    xf = x_ref[0]
    hn = xf * lax.rsqrt(jnp.mean(xf * xf, axis=-1, keepdims=True) + EPS) * g_ref[...]
    hn = hn.astype(BF16)

    def proj(c0, width):
        return jnp.dot(hn, w_ref[:, c0:c0 + width], preferred_element_type=F32)

    for j in range(D_UH // LANES):
        uh_ref[0, j] = proj(j * LANES, LANES).astype(BF16)

    def rope(xc):
        return (xc * c_ref[...] + pltpu.roll(xc, LANES - ROT_DIM // 2, axis=1) * su_ref[...]
                + pltpu.roll(xc, ROT_DIM // 2, axis=1) * sd_ref[...])

    for j in range(D_ATTN // LANES):
        qc = proj(D_UH + j * LANES, LANES)
        q_ref[0, :, j * LANES:(j + 1) * LANES] = (rope(qc) * (HEAD_DIM ** -0.5)).astype(BF16)
    k_ref[0] = rope(proj(D_UH + D_ATTN, D_KV)).astype(BF16)
    v_ref[0] = proj(D_UH + D_ATTN + D_KV, D_KV).astype(BF16)
    g0 = D_UH + D_ATTN + 2 * D_KV
    for j in range(2 * D_MODEL // 512):
        gate_ref[0, :, j * 512:(j + 1) * 512] = jax.nn.sigmoid(proj(g0 + j * 512, 512)).astype(BF16)


def _proj_call(x, g1, w_in, rope_c, rope_su, rope_sd):
    b = x.shape[0]
    nt = SEQ // PROJ_TM
    tile = lambda w: pl.BlockSpec((1, PROJ_TM, w), lambda bi, i: (bi, i, 0))
    rope_spec = pl.BlockSpec((PROJ_TM, LANES), lambda bi, i: (i, 0))
    return pl.pallas_call(
        _proj_kernel,
        grid=(b, nt),
        in_specs=[tile(D_MODEL), _const_spec((1, D_MODEL)), _const_spec((D_MODEL, D_IN)),
                  rope_spec, rope_spec, rope_spec],
        out_specs=[pl.BlockSpec((1, D_UH // LANES, PROJ_TM, LANES), lambda bi, i: (bi, 0, i, 0)),
                   tile(D_ATTN), tile(D_KV), tile(D_KV), tile(2 * D_MODEL)],
        out_shape=[jax.ShapeDtypeStruct((b, D_UH // LANES, SEQ, LANES), BF16),
                   jax.ShapeDtypeStruct((b, SEQ, D_ATTN), BF16),
                   jax.ShapeDtypeStruct((b, SEQ, D_KV), BF16),
                   jax.ShapeDtypeStruct((b, SEQ, D_KV), BF16),
                   jax.ShapeDtypeStruct((b, SEQ, 2 * D_MODEL), BF16)],
        compiler_params=_cparams(("parallel", "parallel")),
        name="proj",
    )(x, g1, w_in, rope_c, rope_su, rope_sd)


FILT_TL = 512


def _filter_kernel(z_ref, w0_ref, b0_ref, wi_ref, bi_ref, wo_ref, fr_ref, dl_ref, h_ref):
    hi = lax.Precision.HIGHEST
    z = z_ref[...]
    fr = fr_ref[...]
    h = jnp.sin(fr * (jnp.dot(z, w0_ref[...], precision=hi, preferred_element_type=F32) + b0_ref[...]))
    for i in range(wi_ref.shape[0]):
        h = jnp.sin(fr * (jnp.dot(h, wi_ref[i], precision=hi, preferred_element_type=F32) + bi_ref[i]))
    out = jnp.dot(h, wo_ref[...], precision=hi, preferred_element_type=F32)
    t = z[:, 0:1]
    h_ref[...] = out * jnp.exp(-t * dl_ref[...])


def _filter_call(z, w0, b0, wi, bi, wo, fr, deltas):
    n_inner = wi.shape[0]
    return pl.pallas_call(
        _filter_kernel,
        grid=(SEQ // FILT_TL,),
        in_specs=[pl.BlockSpec((FILT_TL, LANES), lambda i: (i, 0)),
                  _const_spec((LANES, FILTER_ORDER)), _const_spec((1, FILTER_ORDER)),
                  _const_spec((n_inner, FILTER_ORDER, FILTER_ORDER)), _const_spec((n_inner, 1, FILTER_ORDER)),
                  _const_spec((FILTER_ORDER, N_FILT_CH)), _const_spec((1, FILTER_ORDER)),
                  _const_spec((1, N_FILT_CH))],
        out_specs=pl.BlockSpec((FILT_TL, N_FILT_CH), lambda i: (i, 0)),
        out_shape=jax.ShapeDtypeStruct((SEQ, N_FILT_CH), F32),
        compiler_params=_cparams(("parallel",)),
        name="filt",
    )(z, w0, b0, wi, bi, wo, fr, deltas)


def _stage_f1(z_ref, ab_ref, f1_ref, real_only):
    def body(n2, carry):
        zr = z_ref[0, pl.ds(n2, H1, stride=Z_STRIDE), :]
        if real_only:
            res = jnp.dot(f1_ref[n2][:, :H1], zr.astype(BF16), preferred_element_type=F32)
        else:
            zi = z_ref[1, pl.ds(n2, H1, stride=Z_STRIDE), :]
            st = jnp.concatenate([zr, zi], axis=0).astype(BF16)
            res = jnp.dot(f1_ref[n2], st, preferred_element_type=F32)
        base = pl.multiple_of(n2 * A_STRIDE, 8)
        ab_ref[0, pl.ds(base, N1), :] = res[:N1]
        ab_ref[1, pl.ds(base, N1), :] = res[N1:]
        return carry
    lax.fori_loop(0, N2, body, 0, unroll=2)


def _load_spectrum_block(ab_ref, f2_ref, k1):
    ar = ab_ref[0, pl.ds(k1, N2, stride=A_STRIDE), :]
    ai = ab_ref[1, pl.ds(k1, N2, stride=A_STRIDE), :]
    st = jnp.concatenate([ar, ai], axis=0).astype(BF16)
    x = jnp.dot(f2_ref[...], st, preferred_element_type=F32)
    return x[:N2], x[N2:]


def _kf_kernel(hf_ref, hb_ref, f1_ref, f2_ref, kr_ref, ki_ref, z_ref, ab_ref):
    inv_n = 1.0 / N_FFT

    def load_time(src_ref, drop_first):
        def body(j, carry):
            rows = pl.multiple_of(j * N2, N2)
            v = src_ref[pl.ds(rows, N2), :]
            if drop_first:
                ridx = lax.broadcasted_iota(jnp.int32, (N2, CB), 0) + rows
                v = jnp.where(ridx == 0, 0.0, v)
            z_ref[0, pl.ds(pl.multiple_of(j * Z_STRIDE, 8), N2), :] = v
            return carry
        lax.fori_loop(0, H1, body, 0)

    def spectrum(accumulate):
        def body(k1, carry):
            xr, xi = _load_spectrum_block(ab_ref, f2_ref, k1)
            rows = pl.ds(pl.multiple_of(k1 * N2, N2), N2)
            if accumulate:
                kr_ref[rows, :] = kr_ref[rows, :] + xr * inv_n
                ki_ref[rows, :] = ki_ref[rows, :] - xi * inv_n
            else:
                kr_ref[rows, :] = xr * inv_n
                ki_ref[rows, :] = xi * inv_n
            return carry
        lax.fori_loop(0, N1, body, 0, unroll=2)

    load_time(hf_ref, False)
    _stage_f1(z_ref, ab_ref, f1_ref, True)
    spectrum(False)
    load_time(hb_ref, True)
    _stage_f1(z_ref, ab_ref, f1_ref, True)
    spectrum(True)


def _kf_call(h, f1, f2):
    blocks_per_order = N_CB
    fwd = pl.BlockSpec((SEQ, CB), lambda o, c: (0, o * blocks_per_order + c))
    bwd = pl.BlockSpec((SEQ, CB), lambda o, c: (0, (HYENA_ORDER + o) * blocks_per_order + c))
    out = pl.BlockSpec((None, N_FFT, CB), lambda o, c: (o, 0, c))
    return pl.pallas_call(
        _kf_kernel,
        grid=(HYENA_ORDER, N_CB),
        in_specs=[fwd, bwd, _const_spec((N2, 2 * N1, 2 * H1)), _const_spec((2 * N2, 2 * N2))],
        out_specs=[out, out],
        out_shape=[jax.ShapeDtypeStruct((HYENA_ORDER, N_FFT, D_HYENA), F32)] * 2,
        scratch_shapes=[pltpu.VMEM((2, H1 * Z_STRIDE, CB), F32), pltpu.VMEM((2, N2 * A_STRIDE, CB), F32)],
        compiler_params=_cparams(("parallel", "parallel")),
        name="kf",
    )(h, h, f1, f2)


def _short_conv_chunk(src_ref, r, j, w_ref, row0, apply_conv):
    start = pl.multiple_of(j * N2, N2)
    cur = src_ref[r, pl.ds(start, N2), :]
    if not apply_conv:
        return cur.astype(F32)
    lo = pl.multiple_of(jnp.maximum(start - HALO, 0), HALO)
    hi = pl.multiple_of(jnp.minimum(start + N2, SEQ - HALO), HALO)
    ext = jnp.concatenate([src_ref[r, pl.ds(lo, HALO), :], cur, src_ref[r, pl.ds(hi, HALO), :]], axis=0)
    ext = ext.astype(F32)
    rows = N2 + 2 * HALO
    prev = pltpu.roll(ext, 1, axis=0)[HALO:HALO + N2]
    nxt = pltpu.roll(ext, rows - 1, axis=0)[HALO:HALO + N2]
    ridx = lax.broadcasted_iota(jnp.int32, (N2, CB), 0) + start
    prev = jnp.where(ridx == 0, 0.0, prev)
    nxt = jnp.where(ridx == SEQ - 1, 0.0, nxt)
    w = w_ref[...]
    return (prev * w[row0:row0 + 1] + cur.astype(F32) * w[row0 + 1:row0 + 2]
            + nxt * w[row0 + 2:row0 + 3] + w[row0 + 3:row0 + 4])


def _hyena_kernel(a_ref, g_ref, kr_ref, ki_ref, f1_ref, f2_ref, i1_ref, i2_ref, w_ref, o_ref, z_ref, ab_ref,
                  *, conv_a):
    def fill(j, carry):
        dst = pl.ds(pl.multiple_of(j * Z_STRIDE, 8), N2)
        for r in range(2):
            z_ref[r, dst, :] = _short_conv_chunk(a_ref, r, j, w_ref, 0, conv_a)
        return carry
    lax.fori_loop(0, H1, fill, 0)

    _stage_f1(z_ref, ab_ref, f1_ref, False)

    def mid(k1, carry):
        xr, xi = _load_spectrum_block(ab_ref, f2_ref, k1)
        rows = pl.ds(pl.multiple_of(k1 * N2, N2), N2)
        kr, ki = kr_ref[rows, :], ki_ref[rows, :]
        yr = xr * kr - xi * ki
        yi = xr * ki + xi * kr
        st = jnp.concatenate([yr, yi], axis=0).astype(BF16)
        bv = jnp.dot(i1_ref[...], st, preferred_element_type=F32)
        ab_ref[0, pl.ds(k1, N2, stride=A_STRIDE), :] = bv[:N2]
        ab_ref[1, pl.ds(k1, N2, stride=A_STRIDE), :] = bv[N2:]
        return carry
    lax.fori_loop(0, N1, mid, 0, unroll=2)

    def last(n2, carry):
        base = pl.multiple_of(n2 * A_STRIDE, 8)
        st = jnp.concatenate([ab_ref[0, pl.ds(base, N1), :], ab_ref[1, pl.ds(base, N1), :]], axis=0).astype(BF16)
        y = jnp.dot(i2_ref[n2], st, preferred_element_type=F32)
        z_ref[0, pl.ds(n2, H1, stride=Z_STRIDE), :] = y[:H1]
        z_ref[1, pl.ds(n2, H1, stride=Z_STRIDE), :] = y[H1:]
        return carry
    lax.fori_loop(0, N2, last, 0, unroll=2)

    def finish(j, carry):
        src = pl.ds(pl.multiple_of(j * Z_STRIDE, 8), N2)
        dst = pl.ds(pl.multiple_of(j * N2, N2), N2)
        bias = w_ref[8:9]
        for r in range(2):
            zc = _short_conv_chunk(a_ref, r, j, w_ref, 0, conv_a)
            gc = _short_conv_chunk(g_ref, r, j, w_ref, 4, True)
            o_ref[r, dst, :] = (gc * (z_ref[r, src, :] + bias * zc)).astype(BF16)
        return carry
    lax.fori_loop(0, H1, finish, 0)


def _hyena_call(a, a_slot, g, g_slot, kr, ki, order, tables, wpack, conv_a):
    b = a.shape[0]
    f1, f2, i1, i2 = tables
    io = lambda slot: pl.BlockSpec((2, None, SEQ, CB), lambda c, p: (p, slot + c, 0, 0))
    kspec = pl.BlockSpec((None, N_FFT, CB), lambda c, p: (order, 0, c), pipeline_mode=pl.Buffered(1))
    return pl.pallas_call(
        functools.partial(_hyena_kernel, conv_a=conv_a),
        grid=(N_CB, b // 2),
        in_specs=[io(a_slot), io(g_slot), kspec, kspec,
                  _const_spec((N2, 2 * N1, 2 * H1)), _const_spec((2 * N2, 2 * N2)),
                  _const_spec((2 * N2, 2 * N2)), _const_spec((N2, 2 * H1, 2 * N1)),
                  pl.BlockSpec((None, 16, CB), lambda c, p: (c, 0, 0))],
        out_specs=io(0),
        out_shape=jax.ShapeDtypeStruct((b, N_CB, SEQ, CB), BF16),
        scratch_shapes=[pltpu.VMEM((2, H1 * Z_STRIDE, CB), F32), pltpu.VMEM((2, N2 * A_STRIDE, CB), F32)],
        compiler_params=_cparams(("parallel", "parallel")),
        name=f"hyena{order}",
    )(a, g, kr, ki, f1, f2, i1, i2, wpack)


ATT_TQ = 128
ATT_KW = ATT_TQ + 2 * WINDOW


def _attn_kernel(sink_ref, q_ref, k_ref, v_ref, o_ref):
    i = pl.program_id(1)
    start = pl.multiple_of(jnp.clip(i * ATT_TQ - WINDOW, 0, SEQ - ATT_KW), LANES)
    kw = k_ref[0, pl.ds(start, ATT_KW), :].astype(F32)
    vw = v_ref[0, pl.ds(start, ATT_KW), :].astype(F32)
    kroll = pltpu.roll(kw, HEAD_DIM, axis=1)
    vroll = pltpu.roll(vw, HEAD_DIM, axis=1)
    lo = lax.broadcasted_iota(jnp.int32, (ATT_KW, LANES), 1) < HEAD_DIM
    qpos = i * ATT_TQ + lax.broadcasted_iota(jnp.int32, (ATT_TQ, ATT_KW), 0)
    kpos = start + lax.broadcasted_iota(jnp.int32, (ATT_TQ, ATT_KW), 1)
    valid = jnp.abs(qpos - kpos) <= WINDOW
    lo_q = lax.broadcasted_iota(jnp.int32, (ATT_TQ, LANES), 1) < HEAD_DIM

    def block_diag(x, xroll, h):
        own, other = (x, xroll) if h == 0 else (xroll, x)
        top = jnp.where(lo, own, 0.0)
        bot = jnp.where(lo, 0.0, other)
        return jnp.concatenate([top, bot], axis=0).astype(BF16)

    for h in range(N_KV_HEADS):
        kbd = block_diag(kw, kroll, h)
        vbd = block_diag(vw, vroll, h)
        for j in range(2):
            c0 = LANES * (2 * h + j)
            q2 = q_ref[0, :, c0:c0 + LANES]
            s = lax.dot_general(q2, kbd, (((1,), (1,)), ((), ())), preferred_element_type=F32)
            ps, inv = [], []
            for e in range(2):
                sk = sink_ref[4 * h + 2 * j + e]
                se = jnp.where(valid, s[:, e * ATT_KW:(e + 1) * ATT_KW], NEG_INF)
                m = jnp.maximum(jnp.max(se, axis=-1, keepdims=True), sk)
                p = jnp.exp(se - m)
                inv.append(1.0 / (jnp.sum(p, axis=-1, keepdims=True) + jnp.exp(sk - m)))
                ps.append(p.astype(BF16))
            o2 = jnp.dot(jnp.concatenate(ps, axis=1), vbd, preferred_element_type=F32)
            o_ref[0, :, c0:c0 + LANES] = (o2 * jnp.where(lo_q, inv[0], inv[1])).astype(BF16)


def _attn_call(sink, q, k, v):
    b = q.shape[0]
    full = pl.BlockSpec((1, SEQ, D_KV), lambda bi, i: (bi, 0, 0))
    tile = pl.BlockSpec((1, ATT_TQ, D_ATTN), lambda bi, i: (bi, i, 0))
    return pl.pallas_call(
        _attn_kernel,
        grid=(b, SEQ // ATT_TQ),
        in_specs=[pl.BlockSpec(memory_space=pltpu.SMEM), tile, full, full],
        out_specs=tile,
        out_shape=jax.ShapeDtypeStruct((b, SEQ, D_ATTN), BF16),
        compiler_params=_cparams(("parallel", "parallel")),
        name="attn",
    )(sink, q, k, v)


MIX_TM = 512
FF_CHUNK = D_FF // 2


def _rms(x, g):
    return x * lax.rsqrt(jnp.mean(x * x, axis=-1, keepdims=True) + EPS) * g


def _mix_kernel(x_ref, yh_ref, ya_ref, gate_ref, wuh_ref, wua_ref, wo_ref, g2_ref, wg_ref, wu_ref, wd_ref,
                gf_ref, o_ref):
    yh = jnp.concatenate([yh_ref[0, c] for c in range(N_CB)], axis=1)
    up_h = jnp.dot(yh, wuh_ref[...], preferred_element_type=F32)
    up_a = jnp.dot(ya_ref[0], wua_ref[...], preferred_element_type=F32)
    merged = (gate_ref[0, :, :D_MODEL].astype(F32) * up_h + gate_ref[0, :, D_MODEL:].astype(F32) * up_a)
    x1 = x_ref[0] + jnp.dot(merged.astype(BF16), wo_ref[...], preferred_element_type=F32)
    hn = _rms(x1, g2_ref[...]).astype(BF16)
    acc = x1
    for c in range(D_FF // FF_CHUNK):
        cols = slice(c * FF_CHUNK, (c + 1) * FF_CHUNK)
        gate = jnp.dot(hn, wg_ref[:, cols], preferred_element_type=F32)
        up = jnp.dot(hn, wu_ref[:, cols], preferred_element_type=F32)
        act = (jax.nn.silu(gate) * up).astype(BF16)
        acc = acc + jnp.dot(act, wd_ref[cols, :], preferred_element_type=F32)
    o_ref[0] = _rms(acc, gf_ref[...])


def _mix_call(x, yh, ya, gates, wuh, wua, wo, g2, wg, wu, wd, gf):
    b = x.shape[0]
    tile = lambda w: pl.BlockSpec((1, MIX_TM, w), lambda bi, i: (bi, i, 0))
    return pl.pallas_call(
        _mix_kernel,
        grid=(b, SEQ // MIX_TM),
        in_specs=[tile(D_MODEL), pl.BlockSpec((1, N_CB, MIX_TM, CB), lambda bi, i: (bi, 0, i, 0)),
                  tile(D_ATTN), tile(2 * D_MODEL),
                  _const_spec((D_HYENA, D_MODEL)), _const_spec((D_ATTN, D_MODEL)),
                  _const_spec((D_MODEL, D_MODEL)), _const_spec((1, D_MODEL)),
                  _const_spec((D_MODEL, D_FF)), _const_spec((D_MODEL, D_FF)), _const_spec((D_FF, D_MODEL)),
                  _const_spec((1, D_MODEL))],
        out_specs=tile(D_MODEL),
        out_shape=jax.ShapeDtypeStruct((b, SEQ, D_MODEL), F32),
        compiler_params=_cparams(("parallel", "parallel")),
        name="mix",
    )(x, yh, ya, gates, wuh, wua, wo, g2, wg, wu, wd, gf)


def _pack_hyena_weights(short_w, short_b, hyena_bias, a_idx, g_idx, order):
    def seg(idx):
        cols = slice(idx * D_HYENA, (idx + 1) * D_HYENA)
        return jnp.concatenate([short_w[:, cols], short_b[None, cols]], axis=0)
    rows = jnp.concatenate([seg(a_idx), seg(g_idx), hyena_bias[order][None, :],
                            jnp.zeros((7, D_HYENA), F32)], axis=0)
    return rows.reshape(16, N_CB, CB).transpose(1, 0, 2)


def _layer(x, tables, rope, kf, norm1_g, w_in, short_w, short_b, hyena_bias, sink_logit, w_up_hyena,
           w_up_attn, w_out, norm2_g, w_ff_gate, w_ff_up, w_ff_down, final_g):
    kr, ki = kf
    uh, q, k, v, gates = _proj_call(x, norm1_g[None, :], w_in.astype(BF16), *rope)
    blocks = D_HYENA // LANES
    w0 = _pack_hyena_weights(short_w, short_b, hyena_bias, 0, 1, 0)
    w1 = _pack_hyena_weights(short_w, short_b, hyena_bias, 0, 2, 1)
    z1 = _hyena_call(uh, 0, uh, blocks, kr, ki, 0, tables, w0, True)
    yh = _hyena_call(z1, 0, uh, 2 * blocks, kr, ki, 1, tables, w1, False)
    ya = _attn_call(sink_logit, q, k, v)
    return _mix_call(x, yh, ya, gates, w_up_hyena.astype(BF16), w_up_attn.astype(BF16), w_out.astype(BF16),
                     norm2_g[None, :], w_ff_gate.astype(BF16), w_ff_up.astype(BF16), w_ff_down.astype(BF16),
                     final_g[None, :])


def kernel(x_prompt, x_sample, norm1_g, w_in, short_w, short_b, filt_w0, filt_b0, filt_w_inner, filt_b_inner,
           filt_w_out, filt_freq, hyena_bias, sink_logit, w_up_hyena, w_up_attn, w_out, norm2_g, w_ff_gate,
           w_ff_up, w_ff_down, final_g):
    tables = _dft_tables()
    rope = _rope_tables()
    z_emb, deltas = _filter_embedding()
    w0 = jnp.pad(filt_w0[0], ((0, LANES - FILTER_EMB), (0, 0)))
    h = _filter_call(z_emb, w0, filt_b0[0][None, :], filt_w_inner[0], filt_b_inner[0][:, None, :],
                     filt_w_out[0], filt_freq[0][None, :], deltas)
    kf = _kf_call(h, tables[0], tables[1])
    args = (norm1_g[0], w_in[0], short_w[0], short_b[0], hyena_bias[0], sink_logit[0], w_up_hyena[0],
            w_up_attn[0], w_out[0], norm2_g[0], w_ff_gate[0], w_ff_up[0], w_ff_down[0], final_g)
    y_prompt = _layer(x_prompt, tables, rope, kf, *args)
    y_sample = _layer(x_sample, tables, rope, kf, *args)
    return (y_prompt, y_sample)
```

```python
import functools
import math

import jax
import jax.numpy as jnp
from jax import lax
from jax.experimental import pallas as pl
from jax.experimental.pallas import tpu as pltpu

F32 = jnp.float32
BF16 = jnp.bfloat16

D_MODEL = 1024
SEQ = 4096
D_HYENA = 512
HYENA_ORDER = 2
SHORT_CONV = 3
FILTER_EMB = 33
FILTER_ORDER = 64
N_FILT_CH = 2 * HYENA_ORDER * D_HYENA
FAST_DECAY_PCT = 0.3
SLOW_DECAY_PCT = 1.5
DECAY_TARGET = 1e-2
N_Q_HEADS = 8
N_KV_HEADS = 2
HEAD_DIM = 64
D_ATTN = N_Q_HEADS * HEAD_DIM
D_KV = N_KV_HEADS * HEAD_DIM
WINDOW = 128
ROPE_THETA = 500000.0
ROT_DIM = HEAD_DIM // 4
D_UH = (HYENA_ORDER + 1) * D_HYENA
D_IN = D_UH + D_ATTN + 2 * D_KV + 2 * D_MODEL
D_FF = 2816
EPS = 1e-6
NEG_INF = -1e30

LANES = 128
N_FFT = 2 * SEQ
N1 = 128
N2 = N_FFT // N1
H1 = N1 // 2
Z_STRIDE = N2 + 8
A_STRIDE = N1 + 8
CB = LANES
N_CB = D_HYENA // CB
HALO = 16
VMEM_LIMIT = 56 * 1024 * 1024


def _cparams(sem):
    return pltpu.CompilerParams(dimension_semantics=sem, vmem_limit_bytes=VMEM_LIMIT)


def _const_spec(shape):
    nd = len(shape)
    return pl.BlockSpec(shape, lambda *_: (0,) * nd, pipeline_mode=pl.Buffered(1))


def _dft_tables():
    two_pi = 2.0 * math.pi
    n1 = jnp.arange(H1, dtype=jnp.int32)
    n2 = jnp.arange(N2, dtype=jnp.int32)
    k1 = jnp.arange(N1, dtype=jnp.int32)
    m = ((N2 * n1[None, None, :] + n2[:, None, None]) * k1[None, :, None]) % N_FFT
    ang = m.astype(F32) * (two_pi / N_FFT)
    er, ei = jnp.cos(ang), -jnp.sin(ang)
    f1 = jnp.concatenate([jnp.concatenate([er, -ei], axis=2),
                          jnp.concatenate([ei, er], axis=2)], axis=1)
    ert, eit = jnp.swapaxes(er, 1, 2), jnp.swapaxes(ei, 1, 2)
    i2 = jnp.concatenate([jnp.concatenate([ert, eit], axis=2),
                          jnp.concatenate([-eit, ert], axis=2)], axis=1)
    mg = (n2[:, None] * n2[None, :]) % N2
    angg = mg.astype(F32) * (two_pi / N2)
    gr, gi = jnp.cos(angg), -jnp.sin(angg)
    f2 = jnp.concatenate([jnp.concatenate([gr, -gi], axis=1),
                          jnp.concatenate([gi, gr], axis=1)], axis=0)
    i1 = jnp.concatenate([jnp.concatenate([gr, gi], axis=1),
                          jnp.concatenate([-gi, gr], axis=1)], axis=0)
    return f1.astype(BF16), f2.astype(BF16), i1.astype(BF16), i2.astype(BF16)


def _rope_tables():
    half = ROT_DIM // 2
    pos = jnp.arange(SEQ, dtype=F32)
    inv = 1.0 / (ROPE_THETA ** (jnp.arange(0, ROT_DIM, 2, dtype=F32) / ROT_DIM))
    ang = pos[:, None] * inv[None, :]
    cos, sin = jnp.cos(ang), jnp.sin(ang)
    ones = jnp.ones((SEQ, HEAD_DIM - ROT_DIM), F32)
    zeros = jnp.zeros((SEQ, HEAD_DIM - ROT_DIM), F32)
    zh = jnp.zeros((SEQ, half), F32)
    c = jnp.concatenate([cos, cos, ones], axis=1)
    s_up = jnp.concatenate([-sin, zh, zeros], axis=1)
    s_dn = jnp.concatenate([zh, sin, zeros], axis=1)
    rep = LANES // HEAD_DIM
    return jnp.tile(c, (1, rep)), jnp.tile(s_up, (1, rep)), jnp.tile(s_dn, (1, rep))


def _filter_embedding():
    bands = (FILTER_EMB - 1) // 2
    t = jnp.linspace(0.0, 1.0, SEQ, dtype=F32)[:, None]
    w = 2.0 * math.pi * jnp.arange(SEQ, dtype=F32)[:, None] / SEQ
    f = jnp.linspace(1e-4, bands - 1, bands, dtype=F32)[None, :]
    z = jnp.concatenate([t, jnp.cos(f * w), jnp.sin(f * w)], axis=-1)
    z = jnp.pad(z, ((0, 0), (0, LANES - FILTER_EMB)))
    min_decay = math.log(DECAY_TARGET) / SLOW_DECAY_PCT
    max_decay = math.log(DECAY_TARGET) / FAST_DECAY_PCT
    deltas = jnp.tile(jnp.linspace(min_decay, max_decay, D_HYENA, dtype=F32), 2 * HYENA_ORDER)
    return z, jnp.abs(deltas)[None, :]


PROJ_TM = 512


def _proj_kernel(x_ref, g_ref, w_ref, c_ref, su_ref, sd_ref, uh_ref, q_ref, k_ref, v_ref, gate_ref):
    xf = x_ref[0]
    hn = xf * lax.rsqrt(jnp.mean(xf * xf, axis=-1, keepdims=True) + EPS) * g_ref[...]
    hn = hn.astype(BF16)

    def proj(c0, width):
        return jnp.dot(hn, w_ref[:, c0:c0 + width], preferred_element_type=F32)

    for j in range(D_UH // 512):
        u = proj(j * 512, 512)
        for i in range(512 // LANES):
            uh_ref[0, j * (512 // LANES) + i] = u[:, i * LANES:(i + 1) * LANES].astype(BF16)

    def rope(xc):
        return (xc * c_ref[...] + pltpu.roll(xc, LANES - ROT_DIM // 2, axis=1) * su_ref[...]
                + pltpu.roll(xc, ROT_DIM // 2, axis=1) * sd_ref[...])

    qkv = proj(D_UH, D_ATTN + 2 * D_KV)
    for j in range(D_ATTN // LANES):
        qc = qkv[:, j * LANES:(j + 1) * LANES]
        q_ref[0, :, j * LANES:(j + 1) * LANES] = (rope(qc) * (HEAD_DIM ** -0.5)).astype(BF16)
    k_ref[0] = rope(qkv[:, D_ATTN:D_ATTN + D_KV]).astype(BF16)
    v_ref[0] = qkv[:, D_ATTN + D_KV:].astype(BF16)
    g0 = D_UH + D_ATTN + 2 * D_KV
    for j in range(2 * D_MODEL // 512):
        gate_ref[0, :, j * 512:(j + 1) * 512] = jax.nn.sigmoid(proj(g0 + j * 512, 512)).astype(BF16)


def _proj_call(x, g1, w_in, rope_c, rope_su, rope_sd):
    b = x.shape[0]
    nt = SEQ // PROJ_TM
    tile = lambda w: pl.BlockSpec((1, PROJ_TM, w), lambda bi, i: (bi, i, 0))
    rope_spec = pl.BlockSpec((PROJ_TM, LANES), lambda bi, i: (i, 0))
    return pl.pallas_call(
        _proj_kernel,
        grid=(b, nt),
        in_specs=[tile(D_MODEL), _const_spec((1, D_MODEL)), _const_spec((D_MODEL, D_IN)),
                  rope_spec, rope_spec, rope_spec],
        out_specs=[pl.BlockSpec((1, D_UH // LANES, PROJ_TM, LANES), lambda bi, i: (bi, 0, i, 0)),
                   tile(D_ATTN), tile(D_KV), tile(D_KV), tile(2 * D_MODEL)],
        out_shape=[jax.ShapeDtypeStruct((b, D_UH // LANES, SEQ, LANES), BF16),
                   jax.ShapeDtypeStruct((b, SEQ, D_ATTN), BF16),
                   jax.ShapeDtypeStruct((b, SEQ, D_KV), BF16),
                   jax.ShapeDtypeStruct((b, SEQ, D_KV), BF16),
                   jax.ShapeDtypeStruct((b, SEQ, 2 * D_MODEL), BF16)],
        compiler_params=_cparams(("parallel", "parallel")),
        name="proj",
    )(x, g1, w_in, rope_c, rope_su, rope_sd)


FILT_TL = 512


def _filter_kernel(z_ref, w0_ref, b0_ref, wi_ref, bi_ref, wo_ref, fr_ref, dl_ref, h_ref):
    hi = lax.Precision.HIGHEST
    z = z_ref[...]
    fr = fr_ref[...]
    h = jnp.sin(fr * (jnp.dot(z, w0_ref[...], precision=hi, preferred_element_type=F32) + b0_ref[...]))
    for i in range(wi_ref.shape[0]):
        h = jnp.sin(fr * (jnp.dot(h, wi_ref[i], precision=hi, preferred_element_type=F32) + bi_ref[i]))
    out = jnp.dot(h, wo_ref[...], precision=hi, preferred_element_type=F32)
    t = z[:, 0:1]
    h_ref[...] = out * jnp.exp(-t * dl_ref[...])


def _filter_call(z, w0, b0, wi, bi, wo, fr, deltas):
    n_inner = wi.shape[0]
    return pl.pallas_call(
        _filter_kernel,
        grid=(SEQ // FILT_TL,),
        in_specs=[pl.BlockSpec((FILT_TL, LANES), lambda i: (i, 0)),
                  _const_spec((LANES, FILTER_ORDER)), _const_spec((1, FILTER_ORDER)),
                  _const_spec((n_inner, FILTER_ORDER, FILTER_ORDER)), _const_spec((n_inner, 1, FILTER_ORDER)),
                  _const_spec((FILTER_ORDER, N_FILT_CH)), _const_spec((1, FILTER_ORDER)),
                  _const_spec((1, N_FILT_CH))],
        out_specs=pl.BlockSpec((FILT_TL, N_FILT_CH), lambda i: (i, 0)),
        out_shape=jax.ShapeDtypeStruct((SEQ, N_FILT_CH), F32),
        compiler_params=_cparams(("parallel",)),
        name="filt",
    )(z, w0, b0, wi, bi, wo, fr, deltas)


STAGE_UNROLL = 8
KB = 4
MID_UNROLL = 4


def _stage_f1(z_ref, ab_ref, f1_ref, real_only):
    def body(n2, carry):
        zr = z_ref[0, pl.ds(n2, H1, stride=Z_STRIDE), :]
        if real_only:
            res = jnp.dot(f1_ref[n2][:, :H1], zr.astype(BF16), preferred_element_type=F32)
        else:
            zi = z_ref[1, pl.ds(n2, H1, stride=Z_STRIDE), :]
            st = jnp.concatenate([zr, zi], axis=0).astype(BF16)
            res = jnp.dot(f1_ref[n2], st, preferred_element_type=F32)
        base = pl.multiple_of(n2 * A_STRIDE, 8)
        ab_ref[0, pl.ds(base, N1), :] = res[:N1]
        ab_ref[1, pl.ds(base, N1), :] = res[N1:]
        return carry
    lax.fori_loop(0, N2, body, 0, unroll=STAGE_UNROLL)


def _load_spectrum_blocks(ab_ref, f2_ref, k1b):
    ar = jnp.concatenate([ab_ref[0, pl.ds(k1b + i, N2, stride=A_STRIDE), :] for i in range(KB)], axis=1)
    ai = jnp.concatenate([ab_ref[1, pl.ds(k1b + i, N2, stride=A_STRIDE), :] for i in range(KB)], axis=1)
    st = jnp.concatenate([ar, ai], axis=0).astype(BF16)
    x = jnp.dot(f2_ref[...], st, preferred_element_type=F32)
    return x[:N2], x[N2:]


def _spectrum_rows(k1b, i):
    return pl.ds(pl.multiple_of((k1b + i) * N2, N2), N2)


def _kf_kernel(hf_ref, hb_ref, f1_ref, f2_ref, kr_ref, ki_ref, z_ref, ab_ref):
    inv_n = 1.0 / N_FFT

    def load_time(src_ref, drop_first):
        def body(j, carry):
            rows = pl.multiple_of(j * N2, N2)
            v = src_ref[pl.ds(rows, N2), :]
            if drop_first:
                ridx = lax.broadcasted_iota(jnp.int32, (N2, CB), 0) + rows
                v = jnp.where(ridx == 0, 0.0, v)
            z_ref[0, pl.ds(pl.multiple_of(j * Z_STRIDE, 8), N2), :] = v
            return carry
        lax.fori_loop(0, H1, body, 0, unroll=4)

    def spectrum(accumulate):
        def body(it, carry):
            k1b = it * KB
            xr, xi = _load_spectrum_blocks(ab_ref, f2_ref, k1b)
            for i in range(KB):
                rows = _spectrum_rows(k1b, i)
                lanes = slice(i * CB, (i + 1) * CB)
                if accumulate:
                    kr_ref[rows, :] = kr_ref[rows, :] + xr[:, lanes] * inv_n
                    ki_ref[rows, :] = ki_ref[rows, :] - xi[:, lanes] * inv_n
                else:
                    kr_ref[rows, :] = xr[:, lanes] * inv_n
                    ki_ref[rows, :] = xi[:, lanes] * inv_n
            return carry
        lax.fori_loop(0, N1 // KB, body, 0, unroll=MID_UNROLL)

    load_time(hf_ref, False)
    _stage_f1(z_ref, ab_ref, f1_ref, True)
    spectrum(False)
    load_time(hb_ref, True)
    _stage_f1(z_ref, ab_ref, f1_ref, True)
    spectrum(True)


def _kf_call(h, f1, f2):
    blocks_per_order = N_CB
    fwd = pl.BlockSpec((SEQ, CB), lambda o, c: (0, o * blocks_per_order + c))
    bwd = pl.BlockSpec((SEQ, CB), lambda o, c: (0, (HYENA_ORDER + o) * blocks_per_order + c))
    out = pl.BlockSpec((None, N_FFT, CB), lambda o, c: (o, 0, c))
    return pl.pallas_call(
        _kf_kernel,
        grid=(HYENA_ORDER, N_CB),
        in_specs=[fwd, bwd, _const_spec((N2, 2 * N1, 2 * H1)), _const_spec((2 * N2, 2 * N2))],
        out_specs=[out, out],
        out_shape=[jax.ShapeDtypeStruct((HYENA_ORDER, N_FFT, D_HYENA), F32)] * 2,
        scratch_shapes=[pltpu.VMEM((2, H1 * Z_STRIDE, CB), F32), pltpu.VMEM((2, N2 * A_STRIDE, CB), F32)],
        compiler_params=_cparams(("parallel", "parallel")),
        name="kf",
    )(h, h, f1, f2)


CONV_ROWS = 256
GROUPS = CONV_ROWS // N2


def _conv_rows(src_ref, r, j, w_ref, row0, apply_conv):
    start = pl.multiple_of(j * CONV_ROWS, CONV_ROWS)
    cur = src_ref[r, pl.ds(start, CONV_ROWS), :].astype(F32)
    if not apply_conv:
        return cur
    lo = pl.multiple_of(jnp.maximum(start - HALO, 0), HALO)
    hi = pl.multiple_of(jnp.minimum(start + CONV_ROWS, SEQ - HALO), HALO)
    head = src_ref[r, pl.ds(lo, HALO), :].astype(F32) * (start > 0).astype(F32)
    tail = src_ref[r, pl.ds(hi, HALO), :].astype(F32) * (start + CONV_ROWS < SEQ).astype(F32)
    ext = jnp.concatenate([head, cur, tail], axis=0)
    rows = CONV_ROWS + 2 * HALO
    prev = pltpu.roll(ext, 1, axis=0)[HALO:HALO + CONV_ROWS]
    nxt = pltpu.roll(ext, rows - 1, axis=0)[HALO:HALO + CONV_ROWS]
    w = w_ref[...]
    return (prev * w[row0:row0 + 1] + cur * w[row0 + 1:row0 + 2]
            + nxt * w[row0 + 2:row0 + 3] + w[row0 + 3:row0 + 4])


def _group_rows(j, i):
    return pl.ds(pl.multiple_of((j * GROUPS + i) * Z_STRIDE, 8), N2)


def _hyena_kernel(a_ref, g_ref, kr_ref, ki_ref, f1_ref, f2_ref, i1_ref, i2_ref, w_ref, o_ref,
                  z_ref, y_ref, ab_ref, *, conv_a):
    def fill(j, carry):
        for r in range(2):
            val = _conv_rows(a_ref, r, j, w_ref, 0, conv_a)
            for i in range(GROUPS):
                z_ref[r, _group_rows(j, i), :] = val[i * N2:(i + 1) * N2]
        return carry
    lax.fori_loop(0, SEQ // CONV_ROWS, fill, 0)

    _stage_f1(z_ref, ab_ref, f1_ref, False)

    def store_blocks(k1b, vr, vi):
        for i in range(KB):
            lanes = slice(i * CB, (i + 1) * CB)
            ab_ref[0, pl.ds(k1b + i, N2, stride=A_STRIDE), :] = vr[:, lanes]
            ab_ref[1, pl.ds(k1b + i, N2, stride=A_STRIDE), :] = vi[:, lanes]

    def mid_fwd(it, carry):
        k1b = it * KB
        xr, xi = _load_spectrum_blocks(ab_ref, f2_ref, k1b)
        kr = jnp.concatenate([kr_ref[_spectrum_rows(k1b, i), :] for i in range(KB)], axis=1)
        ki = jnp.concatenate([ki_ref[_spectrum_rows(k1b, i), :] for i in range(KB)], axis=1)
        store_blocks(k1b, xr * kr - xi * ki, xr * ki + xi * kr)
        return carry
    lax.fori_loop(0, N1 // KB, mid_fwd, 0, unroll=MID_UNROLL)

    def mid_inv(it, carry):
        k1b = it * KB
        br, bi = _load_spectrum_blocks(ab_ref, i1_ref, k1b)
        store_blocks(k1b, br, bi)
        return carry
    lax.fori_loop(0, N1 // KB, mid_inv, 0, unroll=MID_UNROLL)

    def last(n2, carry):
        base = pl.multiple_of(n2 * A_STRIDE, 8)
        st = jnp.concatenate([ab_ref[0, pl.ds(base, N1), :], ab_ref[1, pl.ds(base, N1), :]], axis=0).astype(BF16)
        y = jnp.dot(i2_ref[n2], st, preferred_element_type=F32)
        y_ref[0, pl.ds(n2, H1, stride=Z_STRIDE), :] = y[:H1]
        y_ref[1, pl.ds(n2, H1, stride=Z_STRIDE), :] = y[H1:]
        return carry
    lax.fori_loop(0, N2, last, 0, unroll=STAGE_UNROLL)

    def finish(j, carry):
        bias = w_ref[8:9]
        for r in range(2):
            gc = _conv_rows(g_ref, r, j, w_ref, 4, True)
            y = jnp.concatenate([y_ref[r, _group_rows(j, i), :] for i in range(GROUPS)], axis=0)
            z = jnp.concatenate([z_ref[r, _group_rows(j, i), :] for i in range(GROUPS)], axis=0)
            o_ref[r, pl.ds(pl.multiple_of(j * CONV_ROWS, CONV_ROWS), CONV_ROWS), :] = (
                gc * (y + bias * z)).astype(BF16)
        return carry
    lax.fori_loop(0, SEQ // CONV_ROWS, finish, 0)


def _hyena_call(a, a_slot, g, g_slot, kr, ki, order, tables, wpack, conv_a):
    b = a.shape[0]
    f1, f2, i1, i2 = tables
    io = lambda slot: pl.BlockSpec((2, None, SEQ, CB), lambda c, p: (p, slot + c, 0, 0))
    kspec = pl.BlockSpec((None, N_FFT, CB), lambda c, p: (order, 0, c), pipeline_mode=pl.Buffered(1))
    time_scratch = pltpu.VMEM((2, H1 * Z_STRIDE, CB), F32)
    return pl.pallas_call(
        functools.partial(_hyena_kernel, conv_a=conv_a),
        grid=(N_CB, b // 2),
        in_specs=[io(a_slot), io(g_slot), kspec, kspec,
                  _const_spec((N2, 2 * N1, 2 * H1)), _const_spec((2 * N2, 2 * N2)),
                  _const_spec((2 * N2, 2 * N2)), _const_spec((N2, 2 * H1, 2 * N1)),
                  pl.BlockSpec((None, 16, CB), lambda c, p: (c, 0, 0))],
        out_specs=io(0),
        out_shape=jax.ShapeDtypeStruct((b, N_CB, SEQ, CB), BF16),
        scratch_shapes=[time_scratch, time_scratch, pltpu.VMEM((2, N2 * A_STRIDE, CB), F32)],
        compiler_params=_cparams(("parallel", "parallel")),
        name=f"hyena{order}",
    )(a, g, kr, ki, f1, f2, i1, i2, wpack)


ATT_TQ = 128
ATT_KW = ATT_TQ + 2 * WINDOW


def _attn_kernel(sink_ref, q_ref, k_ref, v_ref, o_ref):
    i = pl.program_id(1)
    start = pl.multiple_of(jnp.clip(i * ATT_TQ - WINDOW, 0, SEQ - ATT_KW), LANES)
    kw = k_ref[0, pl.ds(start, ATT_KW), :].astype(F32)
    vw = v_ref[0, pl.ds(start, ATT_KW), :].astype(F32)
    kroll = pltpu.roll(kw, HEAD_DIM, axis=1)
    vroll = pltpu.roll(vw, HEAD_DIM, axis=1)
    lo = lax.broadcasted_iota(jnp.int32, (ATT_KW, LANES), 1) < HEAD_DIM
    qpos = i * ATT_TQ + lax.broadcasted_iota(jnp.int32, (ATT_TQ, ATT_KW), 0)
    kpos = start + lax.broadcasted_iota(jnp.int32, (ATT_TQ, ATT_KW), 1)
    valid = jnp.abs(qpos - kpos) <= WINDOW
    lo_q = lax.broadcasted_iota(jnp.int32, (ATT_TQ, LANES), 1) < HEAD_DIM

    def block_diag(x, xroll, h):
        own, other = (x, xroll) if h == 0 else (xroll, x)
        top = jnp.where(lo, own, 0.0)
        bot = jnp.where(lo, 0.0, other)
        return jnp.concatenate([top, bot], axis=0).astype(BF16)

    for h in range(N_KV_HEADS):
        kbd = block_diag(kw, kroll, h)
        vbd = block_diag(vw, vroll, h)
        for j in range(2):
            c0 = LANES * (2 * h + j)
            q2 = q_ref[0, :, c0:c0 + LANES]
            s = lax.dot_general(q2, kbd, (((1,), (1,)), ((), ())), preferred_element_type=F32)
            ps, inv = [], []
            for e in range(2):
                sk = sink_ref[4 * h + 2 * j + e]
                se = jnp.where(valid, s[:, e * ATT_KW:(e + 1) * ATT_KW], NEG_INF)
                m = jnp.maximum(jnp.max(se, axis=-1, keepdims=True), sk)
                p = jnp.exp(se - m)
                inv.append(1.0 / (jnp.sum(p, axis=-1, keepdims=True) + jnp.exp(sk - m)))
                ps.append(p.astype(BF16))
            o2 = jnp.dot(jnp.concatenate(ps, axis=1), vbd, preferred_element_type=F32)
            o_ref[0, :, c0:c0 + LANES] = (o2 * jnp.where(lo_q, inv[0], inv[1])).astype(BF16)


def _attn_call(sink, q, k, v):
    b = q.shape[0]
    full = pl.BlockSpec((1, SEQ, D_KV), lambda bi, i: (bi, 0, 0))
    tile = pl.BlockSpec((1, ATT_TQ, D_ATTN), lambda bi, i: (bi, i, 0))
    return pl.pallas_call(
        _attn_kernel,
        grid=(b, SEQ // ATT_TQ),
        in_specs=[pl.BlockSpec(memory_space=pltpu.SMEM), tile, full, full],
        out_specs=tile,
        out_shape=jax.ShapeDtypeStruct((b, SEQ, D_ATTN), BF16),
        compiler_params=_cparams(("parallel", "parallel")),
        name="attn",
    )(sink, q, k, v)


MIX_TM = 512
FF_CHUNK = D_FF // 2


def _rms(x, g):
    return x * lax.rsqrt(jnp.mean(x * x, axis=-1, keepdims=True) + EPS) * g


def _mix_kernel(x_ref, yh_ref, ya_ref, gate_ref, wuh_ref, wua_ref, wo_ref, g2_ref, wg_ref, wu_ref, wd_ref,
                gf_ref, o_ref):
    yh = jnp.concatenate([yh_ref[0, c] for c in range(N_CB)], axis=1)
    up_h = jnp.dot(yh, wuh_ref[...], preferred_element_type=F32)
    up_a = jnp.dot(ya_ref[0], wua_ref[...], preferred_element_type=F32)
    merged = (gate_ref[0, :, :D_MODEL].astype(F32) * up_h + gate_ref[0, :, D_MODEL:].astype(F32) * up_a)
    x1 = x_ref[0] + jnp.dot(merged.astype(BF16), wo_ref[...], preferred_element_type=F32)
    hn = _rms(x1, g2_ref[...]).astype(BF16)
    acc = x1
    for c in range(D_FF // FF_CHUNK):
        cols = slice(c * FF_CHUNK, (c + 1) * FF_CHUNK)
        gate = jnp.dot(hn, wg_ref[:, cols], preferred_element_type=F32)
        up = jnp.dot(hn, wu_ref[:, cols], preferred_element_type=F32)
        act = (jax.nn.silu(gate) * up).astype(BF16)
        acc = acc + jnp.dot(act, wd_ref[cols, :], preferred_element_type=F32)
    o_ref[0] = _rms(acc, gf_ref[...])


def _mix_call(x, yh, ya, gates, wuh, wua, wo, g2, wg, wu, wd, gf):
    b = x.shape[0]
    tile = lambda w: pl.BlockSpec((1, MIX_TM, w), lambda bi, i: (bi, i, 0))
    return pl.pallas_call(
        _mix_kernel,
        grid=(b, SEQ // MIX_TM),
        in_specs=[tile(D_MODEL), pl.BlockSpec((1, N_CB, MIX_TM, CB), lambda bi, i: (bi, 0, i, 0)),
                  tile(D_ATTN), tile(2 * D_MODEL),
                  _const_spec((D_HYENA, D_MODEL)), _const_spec((D_ATTN, D_MODEL)),
                  _const_spec((D_MODEL, D_MODEL)), _const_spec((1, D_MODEL)),
                  _const_spec((D_MODEL, D_FF)), _const_spec((D_MODEL, D_FF)), _const_spec((D_FF, D_MODEL)),
                  _const_spec((1, D_MODEL))],
        out_specs=tile(D_MODEL),
        out_shape=jax.ShapeDtypeStruct((b, SEQ, D_MODEL), F32),
        compiler_params=_cparams(("parallel", "parallel")),
        name="mix",
    )(x, yh, ya, gates, wuh, wua, wo, g2, wg, wu, wd, gf)


def _pack_hyena_weights(short_w, short_b, hyena_bias, a_idx, g_idx, order):
    def seg(idx):
        cols = slice(idx * D_HYENA, (idx + 1) * D_HYENA)
        return jnp.concatenate([short_w[:, cols], short_b[None, cols]], axis=0)
    rows = jnp.concatenate([seg(a_idx), seg(g_idx), hyena_bias[order][None, :],
                            jnp.zeros((7, D_HYENA), F32)], axis=0)
    return rows.reshape(16, N_CB, CB).transpose(1, 0, 2)


def _layer(x, tables, rope, kf, norm1_g, w_in, short_w, short_b, hyena_bias, sink_logit, w_up_hyena,
           w_up_attn, w_out, norm2_g, w_ff_gate, w_ff_up, w_ff_down, final_g):
    kr, ki = kf
    uh, q, k, v, gates = _proj_call(x, norm1_g[None, :], w_in.astype(BF16), *rope)
    blocks = D_HYENA // LANES
    w0 = _pack_hyena_weights(short_w, short_b, hyena_bias, 0, 1, 0)
    w1 = _pack_hyena_weights(short_w, short_b, hyena_bias, 0, 2, 1)
    z1 = _hyena_call(uh, 0, uh, blocks, kr, ki, 0, tables, w0, True)
    yh = _hyena_call(z1, 0, uh, 2 * blocks, kr, ki, 1, tables, w1, False)
    ya = _attn_call(sink_logit, q, k, v)
    return _mix_call(x, yh, ya, gates, w_up_hyena.astype(BF16), w_up_attn.astype(BF16), w_out.astype(BF16),
                     norm2_g[None, :], w_ff_gate.astype(BF16), w_ff_up.astype(BF16), w_ff_down.astype(BF16),
                     final_g[None, :])


def kernel(x_prompt, x_sample, norm1_g, w_in, short_w, short_b, filt_w0, filt_b0, filt_w_inner, filt_b_inner,
           filt_w_out, filt_freq, hyena_bias, sink_logit, w_up_hyena, w_up_attn, w_out, norm2_g, w_ff_gate,
           w_ff_up, w_ff_down, final_g):
    tables = _dft_tables()
    rope = _rope_tables()
    z_emb, deltas = _filter_embedding()
    w0 = jnp.pad(filt_w0[0], ((0, LANES - FILTER_EMB), (0, 0)))
    h = _filter_call(z_emb, w0, filt_b0[0][None, :], filt_w_inner[0], filt_b_inner[0][:, None, :],
                     filt_w_out[0], filt_freq[0][None, :], deltas)
    kf = _kf_call(h, tables[0], tables[1])
    args = (norm1_g[0], w_in[0], short_w[0], short_b[0], hyena_bias[0], sink_logit[0], w_up_hyena[0],
            w_up_attn[0], w_out[0], norm2_g[0], w_ff_gate[0], w_ff_up[0], w_ff_down[0], final_g)
    y_prompt = _layer(x_prompt, tables, rope, kf, *args)
    y_sample = _layer(x_sample, tables, rope, kf, *args)
    return (y_prompt, y_sample)
```

```python
import functools
import math

import jax
import jax.numpy as jnp
from jax import lax
from jax.experimental import pallas as pl
from jax.experimental.pallas import tpu as pltpu

F32 = jnp.float32
BF16 = jnp.bfloat16

D_MODEL = 1024
SEQ = 4096
D_HYENA = 512
HYENA_ORDER = 2
SHORT_CONV = 3
FILTER_EMB = 33
FILTER_ORDER = 64
N_FILT_CH = 2 * HYENA_ORDER * D_HYENA
FAST_DECAY_PCT = 0.3
SLOW_DECAY_PCT = 1.5
DECAY_TARGET = 1e-2
N_Q_HEADS = 8
N_KV_HEADS = 2
HEAD_DIM = 64
D_ATTN = N_Q_HEADS * HEAD_DIM
D_KV = N_KV_HEADS * HEAD_DIM
WINDOW = 128
ROPE_THETA = 500000.0
ROT_DIM = HEAD_DIM // 4
D_UH = (HYENA_ORDER + 1) * D_HYENA
D_IN = D_UH + D_ATTN + 2 * D_KV + 2 * D_MODEL
D_FF = 2816
EPS = 1e-6
NEG_INF = -1e30

LANES = 128
N_FFT = 2 * SEQ
N1 = 128
N2 = N_FFT // N1
H1 = N1 // 2
Z_STRIDE = N2 + 8
A_STRIDE = N1 + 8
CB = LANES
N_CB = D_HYENA // CB
HALO = 16
VMEM_LIMIT = 56 * 1024 * 1024


def _cparams(sem):
    return pltpu.CompilerParams(dimension_semantics=sem, vmem_limit_bytes=VMEM_LIMIT)


def _const_spec(shape):
    nd = len(shape)
    return pl.BlockSpec(shape, lambda *_: (0,) * nd, pipeline_mode=pl.Buffered(1))


def _dft_tables():
    two_pi = 2.0 * math.pi
    n1 = jnp.arange(H1, dtype=jnp.int32)
    n2 = jnp.arange(N2, dtype=jnp.int32)
    k1 = jnp.arange(N1, dtype=jnp.int32)
    m = ((N2 * n1[None, None, :] + n2[:, None, None]) * k1[None, :, None]) % N_FFT
    ang = m.astype(F32) * (two_pi / N_FFT)
    er, ei = jnp.cos(ang), -jnp.sin(ang)
    f1 = jnp.concatenate([jnp.concatenate([er, -ei], axis=2),
                          jnp.concatenate([ei, er], axis=2)], axis=1)
    ert, eit = jnp.swapaxes(er, 1, 2), jnp.swapaxes(ei, 1, 2)
    i2 = jnp.concatenate([jnp.concatenate([ert, eit], axis=2),
                          jnp.concatenate([-eit, ert], axis=2)], axis=1)
    mg = (n2[:, None] * n2[None, :]) % N2
    angg = mg.astype(F32) * (two_pi / N2)
    gr, gi = jnp.cos(angg), -jnp.sin(angg)
    f2 = jnp.concatenate([jnp.concatenate([gr, -gi], axis=1),
                          jnp.concatenate([gi, gr], axis=1)], axis=0)
    i1 = jnp.concatenate([jnp.concatenate([gr, gi], axis=1),
                          jnp.concatenate([-gi, gr], axis=1)], axis=0)
    return f1.astype(BF16), f2.astype(BF16), i1.astype(BF16), i2.astype(BF16)


def _rope_tables():
    half = ROT_DIM // 2
    pos = jnp.arange(SEQ, dtype=F32)
    inv = 1.0 / (ROPE_THETA ** (jnp.arange(0, ROT_DIM, 2, dtype=F32) / ROT_DIM))
    ang = pos[:, None] * inv[None, :]
    cos, sin = jnp.cos(ang), jnp.sin(ang)
    ones = jnp.ones((SEQ, HEAD_DIM - ROT_DIM), F32)
    zeros = jnp.zeros((SEQ, HEAD_DIM - ROT_DIM), F32)
    zh = jnp.zeros((SEQ, half), F32)
    c = jnp.concatenate([cos, cos, ones], axis=1)
    s_up = jnp.concatenate([-sin, zh, zeros], axis=1)
    s_dn = jnp.concatenate([zh, sin, zeros], axis=1)
    rep = LANES // HEAD_DIM
    return jnp.tile(c, (1, rep)), jnp.tile(s_up, (1, rep)), jnp.tile(s_dn, (1, rep))


def _filter_embedding():
    bands = (FILTER_EMB - 1) // 2
    t = jnp.linspace(0.0, 1.0, SEQ, dtype=F32)[:, None]
    w = 2.0 * math.pi * jnp.arange(SEQ, dtype=F32)[:, None] / SEQ
    f = jnp.linspace(1e-4, bands - 1, bands, dtype=F32)[None, :]
    z = jnp.concatenate([t, jnp.cos(f * w), jnp.sin(f * w)], axis=-1)
    z = jnp.pad(z, ((0, 0), (0, LANES - FILTER_EMB)))
    min_decay = math.log(DECAY_TARGET) / SLOW_DECAY_PCT
    max_decay = math.log(DECAY_TARGET) / FAST_DECAY_PCT
    deltas = jnp.tile(jnp.linspace(min_decay, max_decay, D_HYENA, dtype=F32), 2 * HYENA_ORDER)
    return z, jnp.abs(deltas)[None, :]


PROJ_TM = 512
LOG2E = math.log2(math.e)
Q_SCALE = HEAD_DIM ** -0.5 * LOG2E


def _proj_kernel(x_ref, g_ref, w_ref, c_ref, su_ref, sd_ref, uh_ref, q_ref, k_ref, v_ref, gate_ref):
    xf = x_ref[0]
    hn = xf * lax.rsqrt(jnp.mean(xf * xf, axis=-1, keepdims=True) + EPS) * g_ref[...]
    hn = hn.astype(BF16)

    def proj(c0, width):
        return jnp.dot(hn, w_ref[:, c0:c0 + width], preferred_element_type=F32)

    for j in range(D_UH // 512):
        u = proj(j * 512, 512)
        for i in range(512 // LANES):
            uh_ref[0, j * (512 // LANES) + i] = u[:, i * LANES:(i + 1) * LANES].astype(BF16)

    def rope(xc):
        return (xc * c_ref[...] + pltpu.roll(xc, LANES - ROT_DIM // 2, axis=1) * su_ref[...]
                + pltpu.roll(xc, ROT_DIM // 2, axis=1) * sd_ref[...])

    qkv = proj(D_UH, D_ATTN + 2 * D_KV)
    for j in range(D_ATTN // LANES):
        qc = qkv[:, j * LANES:(j + 1) * LANES]
        q_ref[0, :, j * LANES:(j + 1) * LANES] = (rope(qc) * Q_SCALE).astype(BF16)
    lo = lax.broadcasted_iota(jnp.int32, (PROJ_TM, LANES), 1) < HEAD_DIM
    for val, ref in ((rope(qkv[:, D_ATTN:D_ATTN + D_KV]), k_ref), (qkv[:, D_ATTN + D_KV:], v_ref)):
        rolled = pltpu.roll(val, HEAD_DIM, axis=1)
        ref[0, 0] = jnp.where(lo, val, 0.0).astype(BF16)
        ref[0, 1] = jnp.where(lo, 0.0, rolled).astype(BF16)
        ref[0, 2] = jnp.where(lo, rolled, 0.0).astype(BF16)
        ref[0, 3] = jnp.where(lo, 0.0, val).astype(BF16)
    g0 = D_UH + D_ATTN + 2 * D_KV
    for j in range(2 * D_MODEL // 512):
        gate_ref[0, :, j * 512:(j + 1) * 512] = jax.nn.sigmoid(proj(g0 + j * 512, 512)).astype(BF16)


def _proj_call(x, g1, w_in, rope_c, rope_su, rope_sd):
    b = x.shape[0]
    nt = SEQ // PROJ_TM
    tile = lambda w: pl.BlockSpec((1, PROJ_TM, w), lambda bi, i: (bi, i, 0))
    rope_spec = pl.BlockSpec((PROJ_TM, LANES), lambda bi, i: (i, 0))
    kv_spec = pl.BlockSpec((1, 2 * N_KV_HEADS, PROJ_TM, LANES), lambda bi, i: (bi, 0, i, 0))
    return pl.pallas_call(
        _proj_kernel,
        grid=(b, nt),
        in_specs=[tile(D_MODEL), _const_spec((1, D_MODEL)), _const_spec((D_MODEL, D_IN)),
                  rope_spec, rope_spec, rope_spec],
        out_specs=[pl.BlockSpec((1, D_UH // LANES, PROJ_TM, LANES), lambda bi, i: (bi, 0, i, 0)),
                   tile(D_ATTN), kv_spec, kv_spec, tile(2 * D_MODEL)],
        out_shape=[jax.ShapeDtypeStruct((b, D_UH // LANES, SEQ, LANES), BF16),
                   jax.ShapeDtypeStruct((b, SEQ, D_ATTN), BF16),
                   jax.ShapeDtypeStruct((b, 2 * N_KV_HEADS, SEQ, LANES), BF16),
                   jax.ShapeDtypeStruct((b, 2 * N_KV_HEADS, SEQ, LANES), BF16),
                   jax.ShapeDtypeStruct((b, SEQ, 2 * D_MODEL), BF16)],
        compiler_params=_cparams(("parallel", "parallel")),
        name="proj",
    )(x, g1, w_in, rope_c, rope_su, rope_sd)


FILT_TL = 512


def _filter_kernel(z_ref, w0_ref, b0_ref, wi_ref, bi_ref, wo_ref, fr_ref, dl_ref, h_ref):
    hi = lax.Precision.HIGHEST
    z = z_ref[...]
    fr = fr_ref[...]
    h = jnp.sin(fr * (jnp.dot(z, w0_ref[...], precision=hi, preferred_element_type=F32) + b0_ref[...]))
    for i in range(wi_ref.shape[0]):
        h = jnp.sin(fr * (jnp.dot(h, wi_ref[i], precision=hi, preferred_element_type=F32) + bi_ref[i]))
    out = jnp.dot(h, wo_ref[...], precision=hi, preferred_element_type=F32)
    t = z[:, 0:1]
    h_ref[...] = out * jnp.exp(-t * dl_ref[...])


def _filter_call(z, w0, b0, wi, bi, wo, fr, deltas):
    n_inner = wi.shape[0]
    return pl.pallas_call(
        _filter_kernel,
        grid=(SEQ // FILT_TL,),
        in_specs=[pl.BlockSpec((FILT_TL, LANES), lambda i: (i, 0)),
                  _const_spec((LANES, FILTER_ORDER)), _const_spec((1, FILTER_ORDER)),
                  _const_spec((n_inner, FILTER_ORDER, FILTER_ORDER)), _const_spec((n_inner, 1, FILTER_ORDER)),
                  _const_spec((FILTER_ORDER, N_FILT_CH)), _const_spec((1, FILTER_ORDER)),
                  _const_spec((1, N_FILT_CH))],
        out_specs=pl.BlockSpec((FILT_TL, N_FILT_CH), lambda i: (i, 0)),
        out_shape=jax.ShapeDtypeStruct((SEQ, N_FILT_CH), F32),
        compiler_params=_cparams(("parallel",)),
        name="filt",
    )(z, w0, b0, wi, bi, wo, fr, deltas)


STAGE_UNROLL = 16
KB = 4
MID_UNROLL = 4


def _stage_f1(z_ref, ab_ref, f1_ref, real_only):
    def body(n2, carry):
        zr = z_ref[0, pl.ds(n2, H1, stride=Z_STRIDE), :]
        if real_only:
            res = jnp.dot(f1_ref[n2][:, :H1], zr.astype(BF16), preferred_element_type=F32)
        else:
            zi = z_ref[1, pl.ds(n2, H1, stride=Z_STRIDE), :]
            st = jnp.concatenate([zr, zi], axis=0).astype(BF16)
            res = jnp.dot(f1_ref[n2], st, preferred_element_type=F32)
        base = pl.multiple_of(n2 * A_STRIDE, 8)
        ab_ref[0, pl.ds(base, N1), :] = res[:N1]
        ab_ref[1, pl.ds(base, N1), :] = res[N1:]
        return carry
    lax.fori_loop(0, N2, body, 0, unroll=STAGE_UNROLL)


def _load_spectrum_blocks(ab_ref, f2_ref, k1b):
    ar = jnp.concatenate([ab_ref[0, pl.ds(k1b + i, N2, stride=A_STRIDE), :] for i in range(KB)], axis=1)
    ai = jnp.concatenate([ab_ref[1, pl.ds(k1b + i, N2, stride=A_STRIDE), :] for i in range(KB)], axis=1)
    st = jnp.concatenate([ar, ai], axis=0).astype(BF16)
    x = jnp.dot(f2_ref[...], st, preferred_element_type=F32)
    return x[:N2], x[N2:]


def _spectrum_rows(k1b, i):
    return pl.ds(pl.multiple_of((k1b + i) * N2, N2), N2)


def _kf_kernel(hf_ref, hb_ref, f1_ref, f2_ref, kr_ref, ki_ref, z_ref, ab_ref):
    inv_n = 1.0 / N_FFT

    def load_time(src_ref, drop_first):
        def body(j, carry):
            rows = pl.multiple_of(j * N2, N2)
            v = src_ref[pl.ds(rows, N2), :]
            if drop_first:
                ridx = lax.broadcasted_iota(jnp.int32, (N2, CB), 0) + rows
                v = jnp.where(ridx == 0, 0.0, v)
            z_ref[0, pl.ds(pl.multiple_of(j * Z_STRIDE, 8), N2), :] = v
            return carry
        lax.fori_loop(0, H1, body, 0, unroll=4)

    def spectrum(accumulate):
        def body(it, carry):
            k1b = it * KB
            xr, xi = _load_spectrum_blocks(ab_ref, f2_ref, k1b)
            for i in range(KB):
                rows = _spectrum_rows(k1b, i)
                lanes = slice(i * CB, (i + 1) * CB)
                if accumulate:
                    kr_ref[rows, :] = kr_ref[rows, :] + xr[:, lanes] * inv_n
                    ki_ref[rows, :] = ki_ref[rows, :] - xi[:, lanes] * inv_n
                else:
                    kr_ref[rows, :] = xr[:, lanes] * inv_n
                    ki_ref[rows, :] = xi[:, lanes] * inv_n
            return carry
        lax.fori_loop(0, N1 // KB, body, 0, unroll=MID_UNROLL)

    load_time(hf_ref, False)
    _stage_f1(z_ref, ab_ref, f1_ref, True)
    spectrum(False)
    load_time(hb_ref, True)
    _stage_f1(z_ref, ab_ref, f1_ref, True)
    spectrum(True)


def _kf_call(h, f1, f2):
    blocks_per_order = N_CB
    fwd = pl.BlockSpec((SEQ, CB), lambda o, c: (0, o * blocks_per_order + c))
    bwd = pl.BlockSpec((SEQ, CB), lambda o, c: (0, (HYENA_ORDER + o) * blocks_per_order + c))
    out = pl.BlockSpec((None, N_FFT, CB), lambda o, c: (o, 0, c))
    return pl.pallas_call(
        _kf_kernel,
        grid=(HYENA_ORDER, N_CB),
        in_specs=[fwd, bwd, _const_spec((N2, 2 * N1, 2 * H1)), _const_spec((2 * N2, 2 * N2))],
        out_specs=[out, out],
        out_shape=[jax.ShapeDtypeStruct((HYENA_ORDER, N_FFT, D_HYENA), F32)] * 2,
        scratch_shapes=[pltpu.VMEM((2, H1 * Z_STRIDE, CB), F32), pltpu.VMEM((2, N2 * A_STRIDE, CB), F32)],
        compiler_params=_cparams(("parallel", "parallel")),
        name="kf",
    )(h, h, f1, f2)


CONV_ROWS = 256
GROUPS = CONV_ROWS // N2


def _conv_rows(src_ref, r, j, w_ref, row0, apply_conv):
    start = pl.multiple_of(j * CONV_ROWS, CONV_ROWS)
    cur = src_ref[r, pl.ds(start, CONV_ROWS), :].astype(F32)
    if not apply_conv:
        return cur
    lo = pl.multiple_of(jnp.maximum(start - HALO, 0), HALO)
    hi = pl.multiple_of(jnp.minimum(start + CONV_ROWS, SEQ - HALO), HALO)
    head = src_ref[r, pl.ds(lo, HALO), :].astype(F32) * (start > 0).astype(F32)
    tail = src_ref[r, pl.ds(hi, HALO), :].astype(F32) * (start + CONV_ROWS < SEQ).astype(F32)
    ext = jnp.concatenate([head, cur, tail], axis=0)
    rows = CONV_ROWS + 2 * HALO
    prev = pltpu.roll(ext, 1, axis=0)[HALO:HALO + CONV_ROWS]
    nxt = pltpu.roll(ext, rows - 1, axis=0)[HALO:HALO + CONV_ROWS]
    w = w_ref[...]
    return (prev * w[row0:row0 + 1] + cur * w[row0 + 1:row0 + 2]
            + nxt * w[row0 + 2:row0 + 3] + w[row0 + 3:row0 + 4])


def _group_rows(j, i):
    return pl.ds(pl.multiple_of((j * GROUPS + i) * Z_STRIDE, 8), N2)


def _hyena_kernel(a_ref, g_ref, kr_ref, ki_ref, f1_ref, f2_ref, i1_ref, i2_ref, w_ref, o_ref,
                  z_ref, y_ref, ab_ref, *, conv_a):
    def fill(j, carry):
        for r in range(2):
            val = _conv_rows(a_ref, r, j, w_ref, 0, conv_a)
            for i in range(GROUPS):
                z_ref[r, _group_rows(j, i), :] = val[i * N2:(i + 1) * N2]
        return carry
    lax.fori_loop(0, SEQ // CONV_ROWS, fill, 0)

    _stage_f1(z_ref, ab_ref, f1_ref, False)

    def store_blocks(k1b, vr, vi):
        for i in range(KB):
            lanes = slice(i * CB, (i + 1) * CB)
            ab_ref[0, pl.ds(k1b + i, N2, stride=A_STRIDE), :] = vr[:, lanes]
            ab_ref[1, pl.ds(k1b + i, N2, stride=A_STRIDE), :] = vi[:, lanes]

    def mid_fwd(it, carry):
        k1b = it * KB
        xr, xi = _load_spectrum_blocks(ab_ref, f2_ref, k1b)
        kr = jnp.concatenate([kr_ref[_spectrum_rows(k1b, i), :] for i in range(KB)], axis=1)
        ki = jnp.concatenate([ki_ref[_spectrum_rows(k1b, i), :] for i in range(KB)], axis=1)
        store_blocks(k1b, xr * kr - xi * ki, xr * ki + xi * kr)
        return carry
    lax.fori_loop(0, N1 // KB, mid_fwd, 0, unroll=MID_UNROLL)

    def mid_inv(it, carry):
        k1b = it * KB
        br, bi = _load_spectrum_blocks(ab_ref, i1_ref, k1b)
        store_blocks(k1b, br, bi)
        return carry
    lax.fori_loop(0, N1 // KB, mid_inv, 0, unroll=MID_UNROLL)

    def last(n2, carry):
        base = pl.multiple_of(n2 * A_STRIDE, 8)
        st = jnp.concatenate([ab_ref[0, pl.ds(base, N1), :], ab_ref[1, pl.ds(base, N1), :]], axis=0).astype(BF16)
        y = jnp.dot(i2_ref[n2], st, preferred_element_type=F32)
        y_ref[0, pl.ds(n2, H1, stride=Z_STRIDE), :] = y[:H1]
        y_ref[1, pl.ds(n2, H1, stride=Z_STRIDE), :] = y[H1:]
        return carry
    lax.fori_loop(0, N2, last, 0, unroll=STAGE_UNROLL)

    def finish(j, carry):
        bias = w_ref[8:9]
        rows = pl.ds(pl.multiple_of(j * CONV_ROWS, CONV_ROWS), CONV_ROWS)
        for r in range(2):
            gc = _conv_rows(g_ref, r, j, w_ref, 4, True)
            y = jnp.concatenate([y_ref[r, _group_rows(j, i), :] for i in range(GROUPS)], axis=0)
            z = jnp.concatenate([z_ref[r, _group_rows(j, i), :] for i in range(GROUPS)], axis=0)
            o_ref[r, rows, :] = (gc * (y + bias * z)).astype(BF16)
        return carry
    lax.fori_loop(0, SEQ // CONV_ROWS, finish, 0)


def _hyena_call(a, a_slot, g, g_slot, kr, ki, order, tables, wpack, conv_a):
    b = a.shape[0]
    f1, f2, i1, i2 = tables
    io = lambda slot: pl.BlockSpec((2, None, SEQ, CB), lambda c, p: (p, slot + c, 0, 0))
    kspec = pl.BlockSpec((None, N_FFT, CB), lambda c, p: (order, 0, c), pipeline_mode=pl.Buffered(1))
    time_scratch = pltpu.VMEM((2, H1 * Z_STRIDE, CB), F32)
    return pl.pallas_call(
        functools.partial(_hyena_kernel, conv_a=conv_a),
        grid=(N_CB, b // 2),
        in_specs=[io(a_slot), io(g_slot), kspec, kspec,
                  _const_spec((N2, 2 * N1, 2 * H1)), _const_spec((2 * N2, 2 * N2)),
                  _const_spec((2 * N2, 2 * N2)), _const_spec((N2, 2 * H1, 2 * N1)),
                  pl.BlockSpec((None, 16, CB), lambda c, p: (c, 0, 0))],
        out_specs=io(0),
        out_shape=jax.ShapeDtypeStruct((b, N_CB, SEQ, CB), BF16),
        scratch_shapes=[time_scratch, time_scratch, pltpu.VMEM((2, N2 * A_STRIDE, CB), F32)],
        compiler_params=_cparams(("parallel", "parallel")),
        name=f"hyena{order}",
    )(a, g, kr, ki, f1, f2, i1, i2, wpack)


ATT_TQ = 128
ATT_KW = ATT_TQ + 2 * WINDOW


ATT_QB = 4


def _attn_kernel(sink_ref, q_ref, k_ref, v_ref, o_ref):
    lo_q = lax.broadcasted_iota(jnp.int32, (ATT_TQ, LANES), 1) < HEAD_DIM
    for qb in range(ATT_QB):
        i = pl.program_id(1) * ATT_QB + qb
        start = pl.multiple_of(jnp.clip(i * ATT_TQ - WINDOW, 0, SEQ - ATT_KW), LANES)
        win = pl.ds(start, ATT_KW)
        qrows = slice(qb * ATT_TQ, (qb + 1) * ATT_TQ)
        qpos = i * ATT_TQ + lax.broadcasted_iota(jnp.int32, (ATT_TQ, ATT_KW), 0)
        kpos = start + lax.broadcasted_iota(jnp.int32, (ATT_TQ, ATT_KW), 1)
        bias = jnp.where(jnp.abs(qpos - kpos) <= WINDOW, 0.0, NEG_INF)
        pairs = [(h, j) for h in range(N_KV_HEADS) for j in range(2)]
        scores = []
        for h, j in pairs:
            kbd = jnp.concatenate([k_ref[0, 2 * h, win, :], k_ref[0, 2 * h + 1, win, :]], axis=0)
            q2 = q_ref[0, qrows, LANES * (2 * h + j):LANES * (2 * h + j + 1)]
            scores.append(lax.dot_general(q2, kbd, (((1,), (1,)), ((), ())), preferred_element_type=F32))
        probs, scales = [], []
        for (h, j), s in zip(pairs, scores):
            ps, inv = [], []
            for e in range(2):
                sk = sink_ref[4 * h + 2 * j + e] * LOG2E
                se = s[:, e * ATT_KW:(e + 1) * ATT_KW] + bias
                m = jnp.maximum(jnp.max(se, axis=-1, keepdims=True), sk)
                p = jnp.exp2(se - m)
                inv.append(1.0 / (jnp.sum(p, axis=-1, keepdims=True) + jnp.exp2(sk - m)))
                ps.append(p.astype(BF16))
            probs.append(jnp.concatenate(ps, axis=1))
            scales.append(jnp.where(lo_q, inv[0], inv[1]))
        for (h, j), p2, sc in zip(pairs, probs, scales):
            vbd = jnp.concatenate([v_ref[0, 2 * h, win, :], v_ref[0, 2 * h + 1, win, :]], axis=0)
            o2 = jnp.dot(p2, vbd, preferred_element_type=F32)
            c0 = LANES * (2 * h + j)
            o_ref[0, qrows, c0:c0 + LANES] = (o2 * sc).astype(BF16)


def _attn_call(sink, q, k, v):
    b = q.shape[0]
    full = pl.BlockSpec((1, 2 * N_KV_HEADS, SEQ, LANES), lambda bi, i: (bi, 0, 0, 0))
    tile = pl.BlockSpec((1, ATT_TQ * ATT_QB, D_ATTN), lambda bi, i: (bi, i, 0))
    return pl.pallas_call(
        _attn_kernel,
        grid=(b, SEQ // (ATT_TQ * ATT_QB)),
        in_specs=[pl.BlockSpec(memory_space=pltpu.SMEM), tile, full, full],
        out_specs=tile,
        out_shape=jax.ShapeDtypeStruct((b, SEQ, D_ATTN), BF16),
        compiler_params=_cparams(("parallel", "parallel")),
        name="attn",
    )(sink, q, k, v)


MIX_TM = 512
MXU_COLS = 256
FF_SPLITS = (0, 6 * MXU_COLS, D_FF)


def _rms(x, g):
    return x * lax.rsqrt(jnp.mean(x * x, axis=-1, keepdims=True) + EPS) * g


def _mix_kernel(x_ref, yh_ref, ya_ref, gate_ref, wuh_ref, wua_ref, wo_ref, g2_ref, wg_ref, wu_ref, wd_ref,
                gf_ref, o_ref):
    yh = jnp.concatenate([yh_ref[0, c] for c in range(N_CB)], axis=1)
    up_h = jnp.dot(yh, wuh_ref[...], preferred_element_type=F32)
    up_a = jnp.dot(ya_ref[0], wua_ref[...], preferred_element_type=F32)
    merged = (gate_ref[0, :, :D_MODEL].astype(F32) * up_h + gate_ref[0, :, D_MODEL:].astype(F32) * up_a)
    x1 = x_ref[0] + jnp.dot(merged.astype(BF16), wo_ref[...], preferred_element_type=F32)
    hn = _rms(x1, g2_ref[...]).astype(BF16)
    acc = x1
    for c in range(len(FF_SPLITS) - 1):
        cols = slice(FF_SPLITS[c], FF_SPLITS[c + 1])
        gate = jnp.dot(hn, wg_ref[:, cols], preferred_element_type=F32)
        up = jnp.dot(hn, wu_ref[:, cols], preferred_element_type=F32)
        act = (jax.nn.silu(gate) * up).astype(BF16)
        acc = acc + jnp.dot(act, wd_ref[cols, :], preferred_element_type=F32)
    o_ref[0] = _rms(acc, gf_ref[...])


def _mix_call(x, yh, ya, gates, wuh, wua, wo, g2, wg, wu, wd, gf):
    b = x.shape[0]
    tile = lambda w: pl.BlockSpec((1, MIX_TM, w), lambda bi, i: (bi, i, 0))
    return pl.pallas_call(
        _mix_kernel,
        grid=(b, SEQ // MIX_TM),
        in_specs=[tile(D_MODEL), pl.BlockSpec((1, N_CB, MIX_TM, CB), lambda bi, i: (bi, 0, i, 0)),
                  tile(D_ATTN), tile(2 * D_MODEL),
                  _const_spec((D_HYENA, D_MODEL)), _const_spec((D_ATTN, D_MODEL)),
                  _const_spec((D_MODEL, D_MODEL)), _const_spec((1, D_MODEL)),
                  _const_spec((D_MODEL, D_FF)), _const_spec((D_MODEL, D_FF)), _const_spec((D_FF, D_MODEL)),
                  _const_spec((1, D_MODEL))],
        out_specs=tile(D_MODEL),
        out_shape=jax.ShapeDtypeStruct((b, SEQ, D_MODEL), F32),
        compiler_params=_cparams(("parallel", "parallel")),
        name="mix",
    )(x, yh, ya, gates, wuh, wua, wo, g2, wg, wu, wd, gf)


def _pack_hyena_weights(short_w, short_b, hyena_bias, a_idx, g_idx, order):
    def seg(idx):
        cols = slice(idx * D_HYENA, (idx + 1) * D_HYENA)
        return jnp.concatenate([short_w[:, cols], short_b[None, cols]], axis=0)
    rows = jnp.concatenate([seg(a_idx), seg(g_idx), hyena_bias[order][None, :],
                            jnp.zeros((7, D_HYENA), F32)], axis=0)
    return rows.reshape(16, N_CB, CB).transpose(1, 0, 2)


def _layer(x, tables, rope, kf, norm1_g, w_in, short_w, short_b, hyena_bias, sink_logit, w_up_hyena,
           w_up_attn, w_out, norm2_g, w_ff_gate, w_ff_up, w_ff_down, final_g):
    kr, ki = kf
    uh, q, k, v, gates = _proj_call(x, norm1_g[None, :], w_in.astype(BF16), *rope)
    blocks = D_HYENA // LANES
    w0 = _pack_hyena_weights(short_w, short_b, hyena_bias, 0, 1, 0)
    w1 = _pack_hyena_weights(short_w, short_b, hyena_bias, 0, 2, 1)
    z1 = _hyena_call(uh, 0, uh, blocks, kr, ki, 0, tables, w0, True)
    yh = _hyena_call(z1, 0, uh, 2 * blocks, kr, ki, 1, tables, w1, False)
    ya = _attn_call(sink_logit, q, k, v)
    return _mix_call(x, yh, ya, gates, w_up_hyena.astype(BF16), w_up_attn.astype(BF16), w_out.astype(BF16),
                     norm2_g[None, :], w_ff_gate.astype(BF16), w_ff_up.astype(BF16), w_ff_down.astype(BF16),
                     final_g[None, :])


def kernel(x_prompt, x_sample, norm1_g, w_in, short_w, short_b, filt_w0, filt_b0, filt_w_inner, filt_b_inner,
           filt_w_out, filt_freq, hyena_bias, sink_logit, w_up_hyena, w_up_attn, w_out, norm2_g, w_ff_gate,
           w_ff_up, w_ff_down, final_g):
    tables = _dft_tables()
    rope = _rope_tables()
    z_emb, deltas = _filter_embedding()
    w0 = jnp.pad(filt_w0[0], ((0, LANES - FILTER_EMB), (0, 0)))
    h = _filter_call(z_emb, w0, filt_b0[0][None, :], filt_w_inner[0], filt_b_inner[0][:, None, :],
                     filt_w_out[0], filt_freq[0][None, :], deltas)
    kf = _kf_call(h, tables[0], tables[1])
    args = (norm1_g[0], w_in[0], short_w[0], short_b[0], hyena_bias[0], sink_logit[0], w_up_hyena[0],
            w_up_attn[0], w_out[0], norm2_g[0], w_ff_gate[0], w_ff_up[0], w_ff_down[0], final_g)
    y_prompt = _layer(x_prompt, tables, rope, kf, *args)
    y_sample = _layer(x_sample, tables, rope, kf, *args)
    return (y_prompt, y_sample)
```

```python
import functools
import math

import jax
import jax.numpy as jnp
from jax import lax
from jax.experimental import pallas as pl
from jax.experimental.pallas import tpu as pltpu

F32 = jnp.float32
BF16 = jnp.bfloat16

D_MODEL = 1024
SEQ = 4096
D_HYENA = 512
HYENA_ORDER = 2
SHORT_CONV = 3
FILTER_EMB = 33
FILTER_ORDER = 64
N_FILT_CH = 2 * HYENA_ORDER * D_HYENA
FAST_DECAY_PCT = 0.3
SLOW_DECAY_PCT = 1.5
DECAY_TARGET = 1e-2
N_Q_HEADS = 8
N_KV_HEADS = 2
HEAD_DIM = 64
D_ATTN = N_Q_HEADS * HEAD_DIM
D_KV = N_KV_HEADS * HEAD_DIM
WINDOW = 128
ROPE_THETA = 500000.0
ROT_DIM = HEAD_DIM // 4
D_UH = (HYENA_ORDER + 1) * D_HYENA
D_IN = D_UH + D_ATTN + 2 * D_KV + 2 * D_MODEL
D_FF = 2816
EPS = 1e-6
NEG_INF = -1e30

LANES = 128
N_FFT = 2 * SEQ
N1 = 128
N2 = N_FFT // N1
H1 = N1 // 2
Z_STRIDE = N2 + 8
A_STRIDE = N1 + 8
K_STRIDE = N2 + 8
CB = LANES
N_CB = D_HYENA // CB
HALO = 16
VMEM_LIMIT = 56 * 1024 * 1024


def _cparams(sem):
    return pltpu.CompilerParams(dimension_semantics=sem, vmem_limit_bytes=VMEM_LIMIT)


def _const_spec(shape):
    nd = len(shape)
    return pl.BlockSpec(shape, lambda *_: (0,) * nd, pipeline_mode=pl.Buffered(1))


def _dft_tables():
    two_pi = 2.0 * math.pi
    n1 = jnp.arange(H1, dtype=jnp.int32)
    n2 = jnp.arange(N2, dtype=jnp.int32)
    k1 = jnp.arange(N1, dtype=jnp.int32)
    m = ((N2 * n1[None, None, :] + n2[:, None, None]) * k1[None, :, None]) % N_FFT
    ang = m.astype(F32) * (two_pi / N_FFT)
    er, ei = jnp.cos(ang), -jnp.sin(ang)
    f1 = jnp.concatenate([jnp.concatenate([er, -ei], axis=2),
                          jnp.concatenate([ei, er], axis=2)], axis=1)
    ert, eit = jnp.swapaxes(er, 1, 2), jnp.swapaxes(ei, 1, 2)
    i2 = jnp.concatenate([jnp.concatenate([ert, eit], axis=2),
                          jnp.concatenate([-eit, ert], axis=2)], axis=1)
    mg = (n2[:, None] * n2[None, :]) % N2
    angg = mg.astype(F32) * (two_pi / N2)
    gr, gi = jnp.cos(angg), -jnp.sin(angg)
    f2 = jnp.concatenate([jnp.concatenate([gr, -gi], axis=1),
                          jnp.concatenate([gi, gr], axis=1)], axis=0)
    i1 = jnp.concatenate([jnp.concatenate([gr, gi], axis=1),
                          jnp.concatenate([-gi, gr], axis=1)], axis=0)
    return f1.astype(BF16), f2.astype(BF16), i1.astype(BF16), i2.astype(BF16)


def _rope_tables():
    half = ROT_DIM // 2
    pos = jnp.arange(SEQ, dtype=F32)
    inv = 1.0 / (ROPE_THETA ** (jnp.arange(0, ROT_DIM, 2, dtype=F32) / ROT_DIM))
    ang = pos[:, None] * inv[None, :]
    cos, sin = jnp.cos(ang), jnp.sin(ang)
    ones = jnp.ones((SEQ, HEAD_DIM - ROT_DIM), F32)
    zeros = jnp.zeros((SEQ, HEAD_DIM - ROT_DIM), F32)
    zh = jnp.zeros((SEQ, half), F32)
    c = jnp.concatenate([cos, cos, ones], axis=1)
    s_up = jnp.concatenate([-sin, zh, zeros], axis=1)
    s_dn = jnp.concatenate([zh, sin, zeros], axis=1)
    rep = LANES // HEAD_DIM
    return jnp.tile(c, (1, rep)), jnp.tile(s_up, (1, rep)), jnp.tile(s_dn, (1, rep))


def _filter_embedding():
    bands = (FILTER_EMB - 1) // 2
    t = jnp.linspace(0.0, 1.0, SEQ, dtype=F32)[:, None]
    w = 2.0 * math.pi * jnp.arange(SEQ, dtype=F32)[:, None] / SEQ
    f = jnp.linspace(1e-4, bands - 1, bands, dtype=F32)[None, :]
    z = jnp.concatenate([t, jnp.cos(f * w), jnp.sin(f * w)], axis=-1)
    z = jnp.pad(z, ((0, 0), (0, LANES - FILTER_EMB)))
    min_decay = math.log(DECAY_TARGET) / SLOW_DECAY_PCT
    max_decay = math.log(DECAY_TARGET) / FAST_DECAY_PCT
    deltas = jnp.tile(jnp.linspace(min_decay, max_decay, D_HYENA, dtype=F32), 2 * HYENA_ORDER)
    return z, jnp.abs(deltas)[None, :]


PROJ_TM = 512
PROJ_HALO = 16
LOG2E = math.log2(math.e)
Q_SCALE = HEAD_DIM ** -0.5 * LOG2E


def _proj_kernel(x_ref, xp_ref, xn_ref, g_ref, w_ref, cw_ref, c_ref, su_ref, sd_ref,
                 uh_ref, q_ref, k_ref, v_ref, gate_ref):
    def norm(xf):
        return xf * lax.rsqrt(jnp.mean(xf * xf, axis=-1, keepdims=True) + EPS) * g_ref[...]

    i = pl.program_id(1)
    hn = norm(x_ref[0]).astype(BF16)
    h_prev = (norm(xp_ref[0]) * (i > 0).astype(F32)).astype(BF16)
    h_next = (norm(xn_ref[0]) * (i < pl.num_programs(1) - 1).astype(F32)).astype(BF16)
    hn_ext = jnp.concatenate([h_prev, hn, h_next], axis=0)

    def proj(c0, width):
        return jnp.dot(hn, w_ref[:, c0:c0 + width], preferred_element_type=F32)

    rows = PROJ_TM + 2 * PROJ_HALO
    for j in range(D_UH // 512):
        cols = slice(j * 512, (j + 1) * 512)
        u = jnp.dot(hn_ext, w_ref[:, cols], preferred_element_type=F32)
        prev = pltpu.roll(u, 1, axis=0)[PROJ_HALO:PROJ_HALO + PROJ_TM]
        nxt = pltpu.roll(u, rows - 1, axis=0)[PROJ_HALO:PROJ_HALO + PROJ_TM]
        cur = u[PROJ_HALO:PROJ_HALO + PROJ_TM]
        uc = (prev * cw_ref[0:1, cols] + cur * cw_ref[1:2, cols] + nxt * cw_ref[2:3, cols] + cw_ref[3:4, cols])
        for c in range(512 // LANES):
            uh_ref[0, j * (512 // LANES) + c] = uc[:, c * LANES:(c + 1) * LANES].astype(BF16)

    def rope(xc):
        return (xc * c_ref[...] + pltpu.roll(xc, LANES - ROT_DIM // 2, axis=1) * su_ref[...]
                + pltpu.roll(xc, ROT_DIM // 2, axis=1) * sd_ref[...])

    qkv = proj(D_UH, D_ATTN + 2 * D_KV)
    for j in range(D_ATTN // LANES):
        qc = qkv[:, j * LANES:(j + 1) * LANES]
        q_ref[0, :, j * LANES:(j + 1) * LANES] = (rope(qc) * Q_SCALE).astype(BF16)
    lo = lax.broadcasted_iota(jnp.int32, (PROJ_TM, LANES), 1) < HEAD_DIM
    for val, ref in ((rope(qkv[:, D_ATTN:D_ATTN + D_KV]), k_ref), (qkv[:, D_ATTN + D_KV:], v_ref)):
        rolled = pltpu.roll(val, HEAD_DIM, axis=1)
        ref[0, 0] = jnp.where(lo, val, 0.0).astype(BF16)
        ref[0, 1] = jnp.where(lo, 0.0, rolled).astype(BF16)
        ref[0, 2] = jnp.where(lo, rolled, 0.0).astype(BF16)
        ref[0, 3] = jnp.where(lo, 0.0, val).astype(BF16)
    g0 = D_UH + D_ATTN + 2 * D_KV
    for j in range(2 * D_MODEL // 512):
        gate_ref[0, :, j * 512:(j + 1) * 512] = jax.nn.sigmoid(proj(g0 + j * 512, 512)).astype(BF16)


def _proj_call(x, g1, w_in, conv_w, rope_c, rope_su, rope_sd):
    b = x.shape[0]
    nt = SEQ // PROJ_TM
    per_tile = PROJ_TM // PROJ_HALO
    tile = lambda w: pl.BlockSpec((1, PROJ_TM, w), lambda bi, i: (bi, i, 0))
    halo_prev = pl.BlockSpec((1, PROJ_HALO, D_MODEL), lambda bi, i: (bi, jnp.maximum(i * per_tile - 1, 0), 0))
    halo_next = pl.BlockSpec((1, PROJ_HALO, D_MODEL),
                             lambda bi, i: (bi, jnp.minimum((i + 1) * per_tile, SEQ // PROJ_HALO - 1), 0))
    rope_spec = pl.BlockSpec((PROJ_TM, LANES), lambda bi, i: (i, 0))
    kv_spec = pl.BlockSpec((1, 2 * N_KV_HEADS, PROJ_TM, LANES), lambda bi, i: (bi, 0, i, 0))
    return pl.pallas_call(
        _proj_kernel,
        grid=(b, nt),
        in_specs=[tile(D_MODEL), halo_prev, halo_next, _const_spec((1, D_MODEL)), _const_spec((D_MODEL, D_IN)),
                  _const_spec((8, D_UH)), rope_spec, rope_spec, rope_spec],
        out_specs=[pl.BlockSpec((1, D_UH // LANES, PROJ_TM, LANES), lambda bi, i: (bi, 0, i, 0)),
                   tile(D_ATTN), kv_spec, kv_spec, tile(2 * D_MODEL)],
        out_shape=[jax.ShapeDtypeStruct((b, D_UH // LANES, SEQ, LANES), BF16),
                   jax.ShapeDtypeStruct((b, SEQ, D_ATTN), BF16),
                   jax.ShapeDtypeStruct((b, 2 * N_KV_HEADS, SEQ, LANES), BF16),
                   jax.ShapeDtypeStruct((b, 2 * N_KV_HEADS, SEQ, LANES), BF16),
                   jax.ShapeDtypeStruct((b, SEQ, 2 * D_MODEL), BF16)],
        compiler_params=_cparams(("parallel", "parallel")),
        name="proj",
    )(x, x, x, g1, w_in, conv_w, rope_c, rope_su, rope_sd)


FILT_TL = 512


def _filter_kernel(z_ref, w0_ref, b0_ref, wi_ref, bi_ref, wo_ref, fr_ref, dl_ref, h_ref):
    hi = lax.Precision.HIGHEST
    z = z_ref[...]
    fr = fr_ref[...]
    h = jnp.sin(fr * (jnp.dot(z, w0_ref[...], precision=hi, preferred_element_type=F32) + b0_ref[...]))
    for i in range(wi_ref.shape[0]):
        h = jnp.sin(fr * (jnp.dot(h, wi_ref[i], precision=hi, preferred_element_type=F32) + bi_ref[i]))
    out = jnp.dot(h, wo_ref[...], precision=hi, preferred_element_type=F32)
    t = z[:, 0:1]
    h_ref[...] = out * jnp.exp(-t * dl_ref[...])


def _filter_call(z, w0, b0, wi, bi, wo, fr, deltas):
    n_inner = wi.shape[0]
    return pl.pallas_call(
        _filter_kernel,
        grid=(SEQ // FILT_TL,),
        in_specs=[pl.BlockSpec((FILT_TL, LANES), lambda i: (i, 0)),
                  _const_spec((LANES, FILTER_ORDER)), _const_spec((1, FILTER_ORDER)),
                  _const_spec((n_inner, FILTER_ORDER, FILTER_ORDER)), _const_spec((n_inner, 1, FILTER_ORDER)),
                  _const_spec((FILTER_ORDER, N_FILT_CH)), _const_spec((1, FILTER_ORDER)),
                  _const_spec((1, N_FILT_CH))],
        out_specs=pl.BlockSpec((FILT_TL, N_FILT_CH), lambda i: (i, 0)),
        out_shape=jax.ShapeDtypeStruct((SEQ, N_FILT_CH), F32),
        compiler_params=_cparams(("parallel",)),
        name="filt",
    )(z, w0, b0, wi, bi, wo, fr, deltas)


STAGE_UNROLL = 32
KB = 2
MID_UNROLL = 32


def _stage_f1(z_ref, a_ref, f1_ref, real_only):
    def body(n2, carry):
        zr = z_ref[0, pl.ds(n2, H1, stride=Z_STRIDE), :]
        if real_only:
            res = jnp.dot(f1_ref[n2][:, :H1], zr.astype(BF16), preferred_element_type=F32)
        else:
            zi = z_ref[1, pl.ds(n2, H1, stride=Z_STRIDE), :]
            st = jnp.concatenate([zr, zi], axis=0).astype(BF16)
            res = jnp.dot(f1_ref[n2], st, preferred_element_type=F32)
        base = pl.multiple_of(n2 * A_STRIDE, 8)
        a_ref[0, pl.ds(base, N1), :] = res[:N1]
        a_ref[1, pl.ds(base, N1), :] = res[N1:]
        return carry
    lax.fori_loop(0, N2, body, 0, unroll=STAGE_UNROLL)


def _k1_rows(k1):
    return pl.ds(pl.multiple_of(k1 * K_STRIDE, 8), N2)


def _dft_blocks(mat_ref, vr, vi):
    st = jnp.concatenate([vr, vi], axis=0).astype(BF16)
    x = jnp.dot(mat_ref[...], st, preferred_element_type=F32)
    return x[:N2], x[N2:]


def _load_spectrum_blocks(a_ref, f2_ref, k1b):
    ar = jnp.concatenate([a_ref[0, pl.ds(k1b + i, N2, stride=A_STRIDE), :] for i in range(KB)], axis=1)
    ai = jnp.concatenate([a_ref[1, pl.ds(k1b + i, N2, stride=A_STRIDE), :] for i in range(KB)], axis=1)
    return _dft_blocks(f2_ref, ar, ai)


def _spectrum_rows(k1b, i):
    return pl.ds(pl.multiple_of((k1b + i) * N2, N2), N2)


def _kf_kernel(hf_ref, hb_ref, f1_ref, f2_ref, kr_ref, ki_ref, z_ref, ab_ref):
    inv_n = 1.0 / N_FFT

    def load_time(src_ref, drop_first):
        def body(j, carry):
            rows = pl.multiple_of(j * N2, N2)
            v = src_ref[pl.ds(rows, N2), :]
            if drop_first:
                ridx = lax.broadcasted_iota(jnp.int32, (N2, CB), 0) + rows
                v = jnp.where(ridx == 0, 0.0, v)
            z_ref[0, pl.ds(pl.multiple_of(j * Z_STRIDE, 8), N2), :] = v
            return carry
        lax.fori_loop(0, H1, body, 0, unroll=4)

    def spectrum(accumulate):
        def body(it, carry):
            k1b = it * KB
            xr, xi = _load_spectrum_blocks(ab_ref, f2_ref, k1b)
            for i in range(KB):
                rows = _spectrum_rows(k1b, i)
                lanes = slice(i * CB, (i + 1) * CB)
                if accumulate:
                    kr_ref[rows, :] = kr_ref[rows, :] + xr[:, lanes] * inv_n
                    ki_ref[rows, :] = ki_ref[rows, :] - xi[:, lanes] * inv_n
                else:
                    kr_ref[rows, :] = xr[:, lanes] * inv_n
                    ki_ref[rows, :] = xi[:, lanes] * inv_n
            return carry
        lax.fori_loop(0, N1 // KB, body, 0, unroll=MID_UNROLL)

    load_time(hf_ref, False)
    _stage_f1(z_ref, ab_ref, f1_ref, True)
    spectrum(False)
    load_time(hb_ref, True)
    _stage_f1(z_ref, ab_ref, f1_ref, True)
    spectrum(True)


def _kf_call(h, f1, f2):
    blocks_per_order = N_CB
    fwd = pl.BlockSpec((SEQ, CB), lambda o, c: (0, o * blocks_per_order + c))
    bwd = pl.BlockSpec((SEQ, CB), lambda o, c: (0, (HYENA_ORDER + o) * blocks_per_order + c))
    out = pl.BlockSpec((None, N_FFT, CB), lambda o, c: (o, 0, c))
    return pl.pallas_call(
        _kf_kernel,
        grid=(HYENA_ORDER, N_CB),
        in_specs=[fwd, bwd, _const_spec((N2, 2 * N1, 2 * H1)), _const_spec((2 * N2, 2 * N2))],
        out_specs=[out, out],
        out_shape=[jax.ShapeDtypeStruct((HYENA_ORDER, N_FFT, D_HYENA), F32)] * 2,
        scratch_shapes=[pltpu.VMEM((2, H1 * Z_STRIDE, CB), F32), pltpu.VMEM((2, N2 * A_STRIDE, CB), F32)],
        compiler_params=_cparams(("parallel", "parallel")),
        name="kf",
    )(h, h, f1, f2)


TIME_ROWS = 256
GROUPS = TIME_ROWS // N2


def _group_rows(j, i):
    return pl.ds(pl.multiple_of((j * GROUPS + i) * Z_STRIDE, 8), N2)


def _hyena_kernel(a_ref, g_ref, kr_ref, ki_ref, f1_ref, f2_ref, i1_ref, i2_ref, w_ref, o_ref,
                  z_ref, s1_ref, s2_ref):
    def fill(j, carry):
        rows = pl.ds(pl.multiple_of(j * TIME_ROWS, TIME_ROWS), TIME_ROWS)
        for r in range(2):
            val = a_ref[r, rows, :].astype(F32)
            for i in range(GROUPS):
                z_ref[r, _group_rows(j, i), :] = val[i * N2:(i + 1) * N2]
        return carry
    lax.fori_loop(0, SEQ // TIME_ROWS, fill, 0, unroll=2)

    _stage_f1(z_ref, s1_ref, f1_ref, False)

    def store_blocks(k1b, vr, vi):
        for i in range(KB):
            lanes = slice(i * CB, (i + 1) * CB)
            s2_ref[0, _k1_rows(k1b + i), :] = vr[:, lanes]
            s2_ref[1, _k1_rows(k1b + i), :] = vi[:, lanes]

    def mid_fwd(it, carry):
        k1b = it * KB
        xr, xi = _load_spectrum_blocks(s1_ref, f2_ref, k1b)
        kr = jnp.concatenate([kr_ref[_spectrum_rows(k1b, i), :] for i in range(KB)], axis=1)
        ki = jnp.concatenate([ki_ref[_spectrum_rows(k1b, i), :] for i in range(KB)], axis=1)
        store_blocks(k1b, xr * kr - xi * ki, xr * ki + xi * kr)
        return carry
    lax.fori_loop(0, N1 // KB, mid_fwd, 0, unroll=MID_UNROLL)

    def mid_inv(it, carry):
        k1b = it * KB
        yr = jnp.concatenate([s2_ref[0, _k1_rows(k1b + i), :] for i in range(KB)], axis=1)
        yi = jnp.concatenate([s2_ref[1, _k1_rows(k1b + i), :] for i in range(KB)], axis=1)
        br, bi = _dft_blocks(i1_ref, yr, yi)
        store_blocks(k1b, br, bi)
        return carry
    lax.fori_loop(0, N1 // KB, mid_inv, 0, unroll=MID_UNROLL)

    y_ref = s1_ref

    def last(n2, carry):
        st = jnp.concatenate([s2_ref[0, pl.ds(n2, N1, stride=K_STRIDE), :],
                              s2_ref[1, pl.ds(n2, N1, stride=K_STRIDE), :]], axis=0).astype(BF16)
        y = jnp.dot(i2_ref[n2], st, preferred_element_type=F32)
        y_ref[0, pl.ds(n2, H1, stride=Z_STRIDE), :] = y[:H1]
        y_ref[1, pl.ds(n2, H1, stride=Z_STRIDE), :] = y[H1:]
        return carry
    lax.fori_loop(0, N2, last, 0, unroll=STAGE_UNROLL)

    def finish(j, carry):
        bias = w_ref[0:1]
        rows = pl.ds(pl.multiple_of(j * TIME_ROWS, TIME_ROWS), TIME_ROWS)
        for r in range(2):
            y = jnp.concatenate([y_ref[r, _group_rows(j, i), :] for i in range(GROUPS)], axis=0)
            z = jnp.concatenate([z_ref[r, _group_rows(j, i), :] for i in range(GROUPS)], axis=0)
            o_ref[r, rows, :] = (g_ref[r, rows, :].astype(F32) * (y + bias * z)).astype(BF16)
        return carry
    lax.fori_loop(0, SEQ // TIME_ROWS, finish, 0, unroll=2)


def _hyena_call(a, a_slot, g, g_slot, kr, ki, order, tables, bias):
    b = a.shape[0]
    f1, f2, i1, i2 = tables
    io = lambda slot: pl.BlockSpec((2, None, SEQ, CB), lambda c, p: (p, slot + c, 0, 0))
    kspec = pl.BlockSpec((None, N_FFT, CB), lambda c, p: (order, 0, c), pipeline_mode=pl.Buffered(1))
    assert N2 * A_STRIDE >= H1 * Z_STRIDE
    return pl.pallas_call(
        _hyena_kernel,
        grid=(N_CB, b // 2),
        in_specs=[io(a_slot), io(g_slot), kspec, kspec,
                  _const_spec((N2, 2 * N1, 2 * H1)), _const_spec((2 * N2, 2 * N2)),
                  _const_spec((2 * N2, 2 * N2)), _const_spec((N2, 2 * H1, 2 * N1)),
                  pl.BlockSpec((None, 8, CB), lambda c, p: (c, 0, 0))],
        out_specs=io(0),
        out_shape=jax.ShapeDtypeStruct((b, N_CB, SEQ, CB), BF16),
        scratch_shapes=[pltpu.VMEM((2, H1 * Z_STRIDE, CB), F32), pltpu.VMEM((2, N2 * A_STRIDE, CB), F32),
                        pltpu.VMEM((2, N1 * K_STRIDE, CB), F32)],
        compiler_params=_cparams(("parallel", "parallel")),
        name=f"hyena{order}",
    )(a, g, kr, ki, f1, f2, i1, i2, bias)


ATT_TQ = 128
ATT_KW = ATT_TQ + 2 * WINDOW


ATT_QB = 4


def _attn_kernel(sink_ref, q_ref, k_ref, v_ref, o_ref):
    lo_q = lax.broadcasted_iota(jnp.int32, (ATT_TQ, LANES), 1) < HEAD_DIM
    for qb in range(ATT_QB):
        i = pl.program_id(1) * ATT_QB + qb
        start = pl.multiple_of(jnp.clip(i * ATT_TQ - WINDOW, 0, SEQ - ATT_KW), LANES)
        win = pl.ds(start, ATT_KW)
        qrows = slice(qb * ATT_TQ, (qb + 1) * ATT_TQ)
        qpos = i * ATT_TQ + lax.broadcasted_iota(jnp.int32, (ATT_TQ, ATT_KW), 0)
        kpos = start + lax.broadcasted_iota(jnp.int32, (ATT_TQ, ATT_KW), 1)
        bias = jnp.where(jnp.abs(qpos - kpos) <= WINDOW, 0.0, NEG_INF)
        pairs = [(h, j) for h in range(N_KV_HEADS) for j in range(2)]
        scores = []
        for h, j in pairs:
            kbd = jnp.concatenate([k_ref[0, 2 * h, win, :], k_ref[0, 2 * h + 1, win, :]], axis=0)
            q2 = q_ref[0, qrows, LANES * (2 * h + j):LANES * (2 * h + j + 1)]
            scores.append(lax.dot_general(q2, kbd, (((1,), (1,)), ((), ())), preferred_element_type=F32))
        probs, scales = [], []
        for (h, j), s in zip(pairs, scores):
            ps, inv = [], []
            for e in range(2):
                sk = sink_ref[4 * h + 2 * j + e] * LOG2E
                se = s[:, e * ATT_KW:(e + 1) * ATT_KW] + bias
                m = jnp.maximum(jnp.max(se, axis=-1, keepdims=True), sk)
                p = jnp.exp2(se - m)
                inv.append(1.0 / (jnp.sum(p, axis=-1, keepdims=True) + jnp.exp2(sk - m)))
                ps.append(p.astype(BF16))
            probs.append(jnp.concatenate(ps, axis=1))
            scales.append(jnp.where(lo_q, inv[0], inv[1]))
        for (h, j), p2, sc in zip(pairs, probs, scales):
            vbd = jnp.concatenate([v_ref[0, 2 * h, win, :], v_ref[0, 2 * h + 1, win, :]], axis=0)
            o2 = jnp.dot(p2, vbd, preferred_element_type=F32)
            c0 = LANES * (2 * h + j)
            o_ref[0, qrows, c0:c0 + LANES] = (o2 * sc).astype(BF16)


def _attn_call(sink, q, k, v):
    b = q.shape[0]
    full = pl.BlockSpec((1, 2 * N_KV_HEADS, SEQ, LANES), lambda bi, i: (bi, 0, 0, 0))
    tile = pl.BlockSpec((1, ATT_TQ * ATT_QB, D_ATTN), lambda bi, i: (bi, i, 0))
    return pl.pallas_call(
        _attn_kernel,
        grid=(b, SEQ // (ATT_TQ * ATT_QB)),
        in_specs=[pl.BlockSpec(memory_space=pltpu.SMEM), tile, full, full],
        out_specs=tile,
        out_shape=jax.ShapeDtypeStruct((b, SEQ, D_ATTN), BF16),
        compiler_params=_cparams(("parallel", "parallel")),
        name="attn",
    )(sink, q, k, v)


MIX_TM = 512
MXU_COLS = 256
FF_SPLITS = (0, 6 * MXU_COLS, D_FF)


def _rms(x, g):
    return x * lax.rsqrt(jnp.mean(x * x, axis=-1, keepdims=True) + EPS) * g


def _mix_kernel(x_ref, yh_ref, ya_ref, gate_ref, wuh_ref, wua_ref, wo_ref, g2_ref, wg_ref, wu_ref, wd_ref,
                gf_ref, o_ref):
    yh = jnp.concatenate([yh_ref[0, c] for c in range(N_CB)], axis=1)
    up_h = jnp.dot(yh, wuh_ref[...], preferred_element_type=F32)
    up_a = jnp.dot(ya_ref[0], wua_ref[...], preferred_element_type=F32)
    merged = (gate_ref[0, :, :D_MODEL].astype(F32) * up_h + gate_ref[0, :, D_MODEL:].astype(F32) * up_a)
    x1 = x_ref[0] + jnp.dot(merged.astype(BF16), wo_ref[...], preferred_element_type=F32)
    hn = _rms(x1, g2_ref[...]).astype(BF16)
    acc = x1
    for c in range(len(FF_SPLITS) - 1):
        cols = slice(FF_SPLITS[c], FF_SPLITS[c + 1])
        gate = jnp.dot(hn, wg_ref[:, cols], preferred_element_type=F32)
        up = jnp.dot(hn, wu_ref[:, cols], preferred_element_type=F32)
        act = (jax.nn.silu(gate) * up).astype(BF16)
        acc = acc + jnp.dot(act, wd_ref[cols, :], preferred_element_type=F32)
    o_ref[0] = _rms(acc, gf_ref[...])


def _mix_call(x, yh, ya, gates, wuh, wua, wo, g2, wg, wu, wd, gf):
    b = x.shape[0]
    tile = lambda w: pl.BlockSpec((1, MIX_TM, w), lambda bi, i: (bi, i, 0))
    return pl.pallas_call(
        _mix_kernel,
        grid=(b, SEQ // MIX_TM),
        in_specs=[tile(D_MODEL), pl.BlockSpec((1, N_CB, MIX_TM, CB), lambda bi, i: (bi, 0, i, 0)),
                  tile(D_ATTN), tile(2 * D_MODEL),
                  _const_spec((D_HYENA, D_MODEL)), _const_spec((D_ATTN, D_MODEL)),
                  _const_spec((D_MODEL, D_MODEL)), _const_spec((1, D_MODEL)),
                  _const_spec((D_MODEL, D_FF)), _const_spec((D_MODEL, D_FF)), _const_spec((D_FF, D_MODEL)),
                  _const_spec((1, D_MODEL))],
        out_specs=tile(D_MODEL),
        out_shape=jax.ShapeDtypeStruct((b, SEQ, D_MODEL), F32),
        compiler_params=_cparams(("parallel", "parallel")),
        name="mix",
    )(x, yh, ya, gates, wuh, wua, wo, g2, wg, wu, wd, gf)


def _hyena_bias_blocks(bias):
    rows = jnp.concatenate([bias[None, :], jnp.zeros((7, D_HYENA), F32)], axis=0)
    return rows.reshape(8, N_CB, CB).transpose(1, 0, 2)


def _layer(x, tables, rope, kf, norm1_g, w_in, short_w, short_b, hyena_bias, sink_logit, w_up_hyena,
           w_up_attn, w_out, norm2_g, w_ff_gate, w_ff_up, w_ff_down, final_g):
    kr, ki = kf
    conv_w = jnp.concatenate([short_w, short_b[None, :], jnp.zeros((4, D_UH), F32)], axis=0)
    uh, q, k, v, gates = _proj_call(x, norm1_g[None, :], w_in.astype(BF16), conv_w, *rope)
    blocks = D_HYENA // LANES
    z1 = _hyena_call(uh, 0, uh, blocks, kr, ki, 0, tables, _hyena_bias_blocks(hyena_bias[0]))
    yh = _hyena_call(z1, 0, uh, 2 * blocks, kr, ki, 1, tables, _hyena_bias_blocks(hyena_bias[1]))
    ya = _attn_call(sink_logit, q, k, v)
    return _mix_call(x, yh, ya, gates, w_up_hyena.astype(BF16), w_up_attn.astype(BF16), w_out.astype(BF16),
                     norm2_g[None, :], w_ff_gate.astype(BF16), w_ff_up.astype(BF16), w_ff_down.astype(BF16),
                     final_g[None, :])


def kernel(x_prompt, x_sample, norm1_g, w_in, short_w, short_b, filt_w0, filt_b0, filt_w_inner, filt_b_inner,
           filt_w_out, filt_freq, hyena_bias, sink_logit, w_up_hyena, w_up_attn, w_out, norm2_g, w_ff_gate,
           w_ff_up, w_ff_down, final_g):
    tables = _dft_tables()
    rope = _rope_tables()
    z_emb, deltas = _filter_embedding()
    w0 = jnp.pad(filt_w0[0], ((0, LANES - FILTER_EMB), (0, 0)))
    h = _filter_call(z_emb, w0, filt_b0[0][None, :], filt_w_inner[0], filt_b_inner[0][:, None, :],
                     filt_w_out[0], filt_freq[0][None, :], deltas)
    kf = _kf_call(h, tables[0], tables[1])
    args = (norm1_g[0], w_in[0], short_w[0], short_b[0], hyena_bias[0], sink_logit[0], w_up_hyena[0],
            w_up_attn[0], w_out[0], norm2_g[0], w_ff_gate[0], w_ff_up[0], w_ff_down[0], final_g)
    y_prompt = _layer(x_prompt, tables, rope, kf, *args)
    y_sample = _layer(x_sample, tables, rope, kf, *args)
    return (y_prompt, y_sample)
```

```python
import functools
import math

import jax
import jax.numpy as jnp
from jax import lax
from jax.experimental import pallas as pl
from jax.experimental.pallas import tpu as pltpu

F32 = jnp.float32
BF16 = jnp.bfloat16

D_MODEL = 1024
SEQ = 4096
D_HYENA = 512
HYENA_ORDER = 2
SHORT_CONV = 3
FILTER_EMB = 33
FILTER_ORDER = 64
N_FILT_CH = 2 * HYENA_ORDER * D_HYENA
FAST_DECAY_PCT = 0.3
SLOW_DECAY_PCT = 1.5
DECAY_TARGET = 1e-2
N_Q_HEADS = 8
N_KV_HEADS = 2
HEAD_DIM = 64
D_ATTN = N_Q_HEADS * HEAD_DIM
D_KV = N_KV_HEADS * HEAD_DIM
WINDOW = 128
ROPE_THETA = 500000.0
ROT_DIM = HEAD_DIM // 4
D_UH = (HYENA_ORDER + 1) * D_HYENA
D_IN = D_UH + D_ATTN + 2 * D_KV + 2 * D_MODEL
D_FF = 2816
EPS = 1e-6
NEG_INF = -1e30

LANES = 128
SUBLANES = 8
N_FFT = 2 * SEQ
N1 = 128
N2 = N_FFT // N1
H1 = N1 // 2
Z_STRIDE = N2 + 1
A_STRIDE = N1 + 1
K_STRIDE = N2 + 1
CB = LANES
N_CB = D_HYENA // CB
VMEM_LIMIT = 56 * 1024 * 1024


def _cparams(sem):
    return pltpu.CompilerParams(dimension_semantics=sem, vmem_limit_bytes=VMEM_LIMIT)


def _const_spec(shape):
    nd = len(shape)
    return pl.BlockSpec(shape, lambda *_: (0,) * nd, pipeline_mode=pl.Buffered(1))


def _dft_tables():
    two_pi = 2.0 * math.pi
    n1 = jnp.arange(H1, dtype=jnp.int32)
    n2 = jnp.arange(N2, dtype=jnp.int32)
    k1 = jnp.arange(N1, dtype=jnp.int32)
    m = ((N2 * n1[None, None, :] + n2[:, None, None]) * k1[None, :, None]) % N_FFT
    ang = m.astype(F32) * (two_pi / N_FFT)
    er, ei = jnp.cos(ang), -jnp.sin(ang)
    f1 = jnp.concatenate([jnp.concatenate([er, -ei], axis=2),
                          jnp.concatenate([ei, er], axis=2)], axis=1)
    ert, eit = jnp.swapaxes(er, 1, 2), jnp.swapaxes(ei, 1, 2)
    i2 = jnp.concatenate([ert, eit], axis=1)
    mg = (n2[:, None] * n2[None, :]) % N2
    angg = mg.astype(F32) * (two_pi / N2)
    gr, gi = jnp.cos(angg), -jnp.sin(angg)
    f2 = jnp.concatenate([jnp.concatenate([gr, -gi], axis=1),
                          jnp.concatenate([gi, gr], axis=1)], axis=0)
    f2 = f2.reshape(2, N2 // SUBLANES, SUBLANES, 2 * N2).transpose(1, 0, 2, 3).reshape(2 * N2, 2 * N2)
    i1 = jnp.concatenate([jnp.concatenate([gr, gi], axis=1),
                          jnp.concatenate([-gi, gr], axis=1)], axis=0)
    return f1.astype(BF16), f2.astype(BF16), i1.astype(BF16), i2.astype(BF16)


def _rope_tables():
    half = ROT_DIM // 2
    pos = jnp.arange(SEQ, dtype=F32)
    inv = 1.0 / (ROPE_THETA ** (jnp.arange(0, ROT_DIM, 2, dtype=F32) / ROT_DIM))
    ang = pos[:, None] * inv[None, :]
    cos, sin = jnp.cos(ang), jnp.sin(ang)
    ones = jnp.ones((SEQ, HEAD_DIM - ROT_DIM), F32)
    zeros = jnp.zeros((SEQ, HEAD_DIM - ROT_DIM), F32)
    zh = jnp.zeros((SEQ, half), F32)
    c = jnp.concatenate([cos, cos, ones], axis=1)
    s_up = jnp.concatenate([-sin, zh, zeros], axis=1)
    s_dn = jnp.concatenate([zh, sin, zeros], axis=1)
    rep = LANES // HEAD_DIM
    return jnp.tile(c, (1, rep)), jnp.tile(s_up, (1, rep)), jnp.tile(s_dn, (1, rep))


def _filter_embedding():
    bands = (FILTER_EMB - 1) // 2
    t = jnp.linspace(0.0, 1.0, SEQ, dtype=F32)[:, None]
    w = 2.0 * math.pi * jnp.arange(SEQ, dtype=F32)[:, None] / SEQ
    f = jnp.linspace(1e-4, bands - 1, bands, dtype=F32)[None, :]
    z = jnp.concatenate([t, jnp.cos(f * w), jnp.sin(f * w)], axis=-1)
    z = jnp.pad(z, ((0, 0), (0, LANES - FILTER_EMB)))
    min_decay = math.log(DECAY_TARGET) / SLOW_DECAY_PCT
    max_decay = math.log(DECAY_TARGET) / FAST_DECAY_PCT
    deltas = jnp.tile(jnp.linspace(min_decay, max_decay, D_HYENA, dtype=F32), 2 * HYENA_ORDER)
    return z, jnp.abs(deltas)[None, :]


PROJ_TM = 512
PROJ_HALO = 16
LOG2E = math.log2(math.e)
Q_SCALE = HEAD_DIM ** -0.5 * LOG2E


def _proj_kernel(x_ref, xp_ref, xn_ref, g_ref, w_ref, cw_ref, c_ref, su_ref, sd_ref,
                 uh_ref, q_ref, k_ref, v_ref, gate_ref):
    def norm(xf):
        return xf * lax.rsqrt(jnp.mean(xf * xf, axis=-1, keepdims=True) + EPS) * g_ref[...]

    i = pl.program_id(1)
    hn = norm(x_ref[0]).astype(BF16)
    h_prev = (norm(xp_ref[0]) * (i > 0).astype(F32)).astype(BF16)
    h_next = (norm(xn_ref[0]) * (i < pl.num_programs(1) - 1).astype(F32)).astype(BF16)
    hn_ext = jnp.concatenate([h_prev, hn, h_next], axis=0)

    def proj(c0, width):
        return jnp.dot(hn, w_ref[:, c0:c0 + width], preferred_element_type=F32)

    rows = PROJ_TM + 2 * PROJ_HALO
    for j in range(D_UH // 512):
        cols = slice(j * 512, (j + 1) * 512)
        u = jnp.dot(hn_ext, w_ref[:, cols], preferred_element_type=F32)
        prev = pltpu.roll(u, 1, axis=0)[PROJ_HALO:PROJ_HALO + PROJ_TM]
        nxt = pltpu.roll(u, rows - 1, axis=0)[PROJ_HALO:PROJ_HALO + PROJ_TM]
        cur = u[PROJ_HALO:PROJ_HALO + PROJ_TM]
        uc = (prev * cw_ref[0:1, cols] + cur * cw_ref[1:2, cols] + nxt * cw_ref[2:3, cols] + cw_ref[3:4, cols])
        for c in range(512 // LANES):
            uh_ref[0, j * (512 // LANES) + c] = uc[:, c * LANES:(c + 1) * LANES].astype(BF16)

    def rope(xc):
        return (xc * c_ref[...] + pltpu.roll(xc, LANES - ROT_DIM // 2, axis=1) * su_ref[...]
                + pltpu.roll(xc, ROT_DIM // 2, axis=1) * sd_ref[...])

    qkv = proj(D_UH, D_ATTN + 2 * D_KV)
    for j in range(D_ATTN // LANES):
        qc = qkv[:, j * LANES:(j + 1) * LANES]
        q_ref[0, :, j * LANES:(j + 1) * LANES] = (rope(qc) * Q_SCALE).astype(BF16)
    lo = lax.broadcasted_iota(jnp.int32, (PROJ_TM, LANES), 1) < HEAD_DIM
    for val, ref in ((rope(qkv[:, D_ATTN:D_ATTN + D_KV]), k_ref), (qkv[:, D_ATTN + D_KV:], v_ref)):
        rolled = pltpu.roll(val, HEAD_DIM, axis=1)
        ref[0, 0] = jnp.where(lo, val, 0.0).astype(BF16)
        ref[0, 1] = jnp.where(lo, 0.0, rolled).astype(BF16)
        ref[0, 2] = jnp.where(lo, rolled, 0.0).astype(BF16)
        ref[0, 3] = jnp.where(lo, 0.0, val).astype(BF16)
    g0 = D_UH + D_ATTN + 2 * D_KV
    for j in range(2 * D_MODEL // 512):
        gate_ref[0, :, j * 512:(j + 1) * 512] = jax.nn.sigmoid(proj(g0 + j * 512, 512)).astype(BF16)


def _proj_call(x, g1, w_in, conv_w, rope_c, rope_su, rope_sd):
    b = x.shape[0]
    nt = SEQ // PROJ_TM
    per_tile = PROJ_TM // PROJ_HALO
    tile = lambda w: pl.BlockSpec((1, PROJ_TM, w), lambda bi, i: (bi, i, 0))
    halo_prev = pl.BlockSpec((1, PROJ_HALO, D_MODEL), lambda bi, i: (bi, jnp.maximum(i * per_tile - 1, 0), 0))
    halo_next = pl.BlockSpec((1, PROJ_HALO, D_MODEL),
                             lambda bi, i: (bi, jnp.minimum((i + 1) * per_tile, SEQ // PROJ_HALO - 1), 0))
    rope_spec = pl.BlockSpec((PROJ_TM, LANES), lambda bi, i: (i, 0))
    kv_spec = pl.BlockSpec((1, 2 * N_KV_HEADS, PROJ_TM, LANES), lambda bi, i: (bi, 0, i, 0))
    return pl.pallas_call(
        _proj_kernel,
        grid=(b, nt),
        in_specs=[tile(D_MODEL), halo_prev, halo_next, _const_spec((1, D_MODEL)), _const_spec((D_MODEL, D_IN)),
                  _const_spec((8, D_UH)), rope_spec, rope_spec, rope_spec],
        out_specs=[pl.BlockSpec((1, D_UH // LANES, PROJ_TM, LANES), lambda bi, i: (bi, 0, i, 0)),
                   tile(D_ATTN), kv_spec, kv_spec, tile(2 * D_MODEL)],
        out_shape=[jax.ShapeDtypeStruct((b, D_UH // LANES, SEQ, LANES), BF16),
                   jax.ShapeDtypeStruct((b, SEQ, D_ATTN), BF16),
                   jax.ShapeDtypeStruct((b, 2 * N_KV_HEADS, SEQ, LANES), BF16),
                   jax.ShapeDtypeStruct((b, 2 * N_KV_HEADS, SEQ, LANES), BF16),
                   jax.ShapeDtypeStruct((b, SEQ, 2 * D_MODEL), BF16)],
        compiler_params=_cparams(("parallel", "parallel")),
        name="proj",
    )(x, x, x, g1, w_in, conv_w, rope_c, rope_su, rope_sd)


FILT_TL = 512


def _filter_kernel(z_ref, w0_ref, b0_ref, wi_ref, bi_ref, wo_ref, fr_ref, dl_ref, h_ref):
    hi = lax.Precision.HIGHEST
    z = z_ref[...]
    fr = fr_ref[...]
    h = jnp.sin(fr * (jnp.dot(z, w0_ref[...], precision=hi, preferred_element_type=F32) + b0_ref[...]))
    for i in range(wi_ref.shape[0]):
        h = jnp.sin(fr * (jnp.dot(h, wi_ref[i], precision=hi, preferred_element_type=F32) + bi_ref[i]))
    def split(v):
        top = v.astype(BF16)
        return top, (v - top.astype(F32)).astype(BF16)
    h_hi, h_lo = split(h)
    w_hi, w_lo = split(wo_ref[...])
    out = (jnp.dot(h_hi, w_hi, preferred_element_type=F32) + jnp.dot(h_hi, w_lo, preferred_element_type=F32)
           + jnp.dot(h_lo, w_hi, preferred_element_type=F32))
    t = z[:, 0:1]
    h_ref[...] = out * jnp.exp(-t * dl_ref[...])


def _filter_call(z, w0, b0, wi, bi, wo, fr, deltas):
    n_inner = wi.shape[0]
    return pl.pallas_call(
        _filter_kernel,
        grid=(SEQ // FILT_TL,),
        in_specs=[pl.BlockSpec((FILT_TL, LANES), lambda i: (i, 0)),
                  _const_spec((LANES, FILTER_ORDER)), _const_spec((1, FILTER_ORDER)),
                  _const_spec((n_inner, FILTER_ORDER, FILTER_ORDER)), _const_spec((n_inner, 1, FILTER_ORDER)),
                  _const_spec((FILTER_ORDER, N_FILT_CH)), _const_spec((1, FILTER_ORDER)),
                  _const_spec((1, N_FILT_CH))],
        out_specs=pl.BlockSpec((FILT_TL, N_FILT_CH), lambda i: (i, 0)),
        out_shape=jax.ShapeDtypeStruct((SEQ, N_FILT_CH), F32),
        compiler_params=_cparams(("parallel",)),
        name="filt",
    )(z, w0, b0, wi, bi, wo, fr, deltas)


STAGE_UNROLL = 64
KB = 2
MID_UNROLL = 64


def _stage_f1(z_ref, a_ref, f1_ref, real_only):
    def body(n2, carry):
        zr = z_ref[0, pl.ds(n2, H1, stride=Z_STRIDE), :]
        if real_only:
            res = jnp.dot(f1_ref[n2][:, :H1], zr.astype(BF16), preferred_element_type=F32)
        else:
            zi = z_ref[1, pl.ds(n2, H1, stride=Z_STRIDE), :]
            st = jnp.concatenate([zr, zi], axis=0).astype(BF16)
            res = jnp.dot(f1_ref[n2], st, preferred_element_type=F32)
        base = n2 * A_STRIDE
        a_ref[0, pl.ds(base, N1), :] = res[:N1]
        a_ref[1, pl.ds(base, N1), :] = res[N1:]
        return carry
    lax.fori_loop(0, N2, body, 0, unroll=STAGE_UNROLL)


def _k1_rows(k1):
    return pl.ds(k1 * K_STRIDE, N2)


def _dft_blocks(mat_ref, vr, vi, interleaved):
    st = jnp.concatenate([vr, vi], axis=0).astype(BF16)
    x = jnp.dot(mat_ref[...], st, preferred_element_type=F32)
    if not interleaved:
        return x[:N2], x[N2:]
    x4 = x.reshape(N2 // SUBLANES, 2, SUBLANES, x.shape[-1])
    return x4[:, 0].reshape(N2, x.shape[-1]), x4[:, 1].reshape(N2, x.shape[-1])


def _load_spectrum_blocks(a_ref, f2_ref, k1b):
    ar = jnp.concatenate([a_ref[0, pl.ds(k1b + i, N2, stride=A_STRIDE), :] for i in range(KB)], axis=1)
    ai = jnp.concatenate([a_ref[1, pl.ds(k1b + i, N2, stride=A_STRIDE), :] for i in range(KB)], axis=1)
    return _dft_blocks(f2_ref, ar, ai, True)


def _spectrum_rows(k1b, i):
    return pl.ds(pl.multiple_of((k1b + i) * N2, N2), N2)


def _kf_kernel(hf_ref, hb_ref, f1_ref, f2_ref, kr_ref, ki_ref, z_ref, ab_ref):
    inv_n = 1.0 / N_FFT

    def load_time(src_ref, drop_first):
        def body(j, carry):
            rows = pl.multiple_of(j * N2, N2)
            v = src_ref[pl.ds(rows, N2), :]
            if drop_first:
                ridx = lax.broadcasted_iota(jnp.int32, (N2, CB), 0) + rows
                v = jnp.where(ridx == 0, 0.0, v)
            z_ref[0, pl.ds(j * Z_STRIDE, N2), :] = v
            return carry
        lax.fori_loop(0, H1, body, 0, unroll=4)

    def spectrum(accumulate):
        def body(it, carry):
            k1b = it * KB
            xr, xi = _load_spectrum_blocks(ab_ref, f2_ref, k1b)
            for i in range(KB):
                rows = _spectrum_rows(k1b, i)
                lanes = slice(i * CB, (i + 1) * CB)
                if accumulate:
                    kr_ref[rows, :] = kr_ref[rows, :] + xr[:, lanes] * inv_n
                    ki_ref[rows, :] = ki_ref[rows, :] - xi[:, lanes] * inv_n
                else:
                    kr_ref[rows, :] = xr[:, lanes] * inv_n
                    ki_ref[rows, :] = xi[:, lanes] * inv_n
            return carry
        lax.fori_loop(0, N1 // KB, body, 0, unroll=MID_UNROLL)

    load_time(hf_ref, False)
    _stage_f1(z_ref, ab_ref, f1_ref, True)
    spectrum(False)
    load_time(hb_ref, True)
    _stage_f1(z_ref, ab_ref, f1_ref, True)
    spectrum(True)


def _kf_call(h, f1, f2):
    blocks_per_order = N_CB
    fwd = pl.BlockSpec((SEQ, CB), lambda o, c: (0, o * blocks_per_order + c))
    bwd = pl.BlockSpec((SEQ, CB), lambda o, c: (0, (HYENA_ORDER + o) * blocks_per_order + c))
    out = pl.BlockSpec((None, N_FFT, CB), lambda o, c: (o, 0, c))
    return pl.pallas_call(
        _kf_kernel,
        grid=(HYENA_ORDER, N_CB),
        in_specs=[fwd, bwd, _const_spec((N2, 2 * N1, 2 * H1)), _const_spec((2 * N2, 2 * N2))],
        out_specs=[out, out],
        out_shape=[jax.ShapeDtypeStruct((HYENA_ORDER, N_FFT, D_HYENA), F32)] * 2,
        scratch_shapes=[pltpu.VMEM((2, H1 * Z_STRIDE, CB), F32), pltpu.VMEM((2, N2 * A_STRIDE, CB), F32)],
        compiler_params=_cparams(("parallel", "parallel")),
        name="kf",
    )(h, h, f1, f2)


TIME_ROWS = 256
GROUPS = TIME_ROWS // N2


def _group_rows(j, i):
    return pl.ds((j * GROUPS + i) * Z_STRIDE, N2)


def _hyena_kernel(a_ref, g_ref, kr_ref, ki_ref, f1_ref, f2_ref, i1_ref, i2_ref, w_ref, o_ref,
                  z_ref, s1_ref, s2_ref):
    def fill(j, carry):
        rows = pl.ds(pl.multiple_of(j * TIME_ROWS, TIME_ROWS), TIME_ROWS)
        for r in range(2):
            val = a_ref[r, rows, :].astype(F32)
            for i in range(GROUPS):
                z_ref[r, _group_rows(j, i), :] = val[i * N2:(i + 1) * N2]
        return carry
    lax.fori_loop(0, SEQ // TIME_ROWS, fill, 0, unroll=2)

    _stage_f1(z_ref, s1_ref, f1_ref, False)

    def store_blocks(k1b, vr, vi):
        for i in range(KB):
            lanes = slice(i * CB, (i + 1) * CB)
            s2_ref[0, _k1_rows(k1b + i), :] = vr[:, lanes]
            s2_ref[1, _k1_rows(k1b + i), :] = vi[:, lanes]

    def mid_fwd(it, carry):
        k1b = it * KB
        xr, xi = _load_spectrum_blocks(s1_ref, f2_ref, k1b)
        kr = jnp.concatenate([kr_ref[_spectrum_rows(k1b, i), :] for i in range(KB)], axis=1)
        ki = jnp.concatenate([ki_ref[_spectrum_rows(k1b, i), :] for i in range(KB)], axis=1)
        store_blocks(k1b, xr * kr - xi * ki, xr * ki + xi * kr)
        return carry
    lax.fori_loop(0, N1 // KB, mid_fwd, 0, unroll=MID_UNROLL)

    def mid_inv(it, carry):
        k1b = it * KB
        yr = jnp.concatenate([s2_ref[0, _k1_rows(k1b + i), :] for i in range(KB)], axis=1)
        yi = jnp.concatenate([s2_ref[1, _k1_rows(k1b + i), :] for i in range(KB)], axis=1)
        br, bi = _dft_blocks(i1_ref, yr, yi, False)
        store_blocks(k1b, br, bi)
        return carry
    lax.fori_loop(0, N1 // KB, mid_inv, 0, unroll=MID_UNROLL)

    y_ref = s1_ref

    def last(n2, carry):
        st = jnp.concatenate([s2_ref[0, pl.ds(n2, N1, stride=K_STRIDE), :],
                              s2_ref[1, pl.ds(n2, N1, stride=K_STRIDE), :]], axis=1).astype(BF16)
        p = jnp.dot(i2_ref[n2], st, preferred_element_type=F32)
        y_ref[0, pl.ds(n2, H1, stride=Z_STRIDE), :] = p[:H1, :CB] + p[H1:, CB:]
        y_ref[1, pl.ds(n2, H1, stride=Z_STRIDE), :] = p[:H1, CB:] - p[H1:, :CB]
        return carry
    lax.fori_loop(0, N2, last, 0, unroll=STAGE_UNROLL)

    def finish(j, carry):
        bias = w_ref[0:1]
        rows = pl.ds(pl.multiple_of(j * TIME_ROWS, TIME_ROWS), TIME_ROWS)
        for r in range(2):
            y = jnp.concatenate([y_ref[r, _group_rows(j, i), :] for i in range(GROUPS)], axis=0)
            z = jnp.concatenate([z_ref[r, _group_rows(j, i), :] for i in range(GROUPS)], axis=0)
            o_ref[r, rows, :] = (g_ref[r, rows, :].astype(F32) * (y + bias * z)).astype(BF16)
        return carry
    lax.fori_loop(0, SEQ // TIME_ROWS, finish, 0, unroll=2)


def _hyena_call(a, a_slot, g, g_slot, kr, ki, order, tables, bias):
    b = a.shape[0]
    f1, f2, i1, i2 = tables
    io = lambda slot: pl.BlockSpec((2, None, SEQ, CB), lambda c, p: (p, slot + c, 0, 0))
    kspec = pl.BlockSpec((None, N_FFT, CB), lambda c, p: (order, 0, c), pipeline_mode=pl.Buffered(1))
    assert N2 * A_STRIDE >= H1 * Z_STRIDE
    return pl.pallas_call(
        _hyena_kernel,
        grid=(N_CB, b // 2),
        in_specs=[io(a_slot), io(g_slot), kspec, kspec,
                  _const_spec((N2, 2 * N1, 2 * H1)), _const_spec((2 * N2, 2 * N2)),
                  _const_spec((2 * N2, 2 * N2)), _const_spec((N2, 2 * H1, N1)),
                  pl.BlockSpec((None, 8, CB), lambda c, p: (c, 0, 0))],
        out_specs=io(0),
        out_shape=jax.ShapeDtypeStruct((b, N_CB, SEQ, CB), BF16),
        scratch_shapes=[pltpu.VMEM((2, H1 * Z_STRIDE, CB), F32), pltpu.VMEM((2, N2 * A_STRIDE, CB), F32),
                        pltpu.VMEM((2, N1 * K_STRIDE, CB), F32)],
        compiler_params=_cparams(("parallel", "parallel")),
        name=f"hyena{order}",
    )(a, g, kr, ki, f1, f2, i1, i2, bias)


ATT_TQ = 128
ATT_KW = ATT_TQ + 2 * WINDOW


ATT_QB = 4


def _attn_kernel(sink_ref, q_ref, k_ref, v_ref, o_ref):
    lo_q = lax.broadcasted_iota(jnp.int32, (ATT_TQ, LANES), 1) < HEAD_DIM
    for qb in range(ATT_QB):
        i = pl.program_id(1) * ATT_QB + qb
        start = pl.multiple_of(jnp.clip(i * ATT_TQ - WINDOW, 0, SEQ - ATT_KW), LANES)
        win = pl.ds(start, ATT_KW)
        qrows = slice(qb * ATT_TQ, (qb + 1) * ATT_TQ)
        qpos = i * ATT_TQ + lax.broadcasted_iota(jnp.int32, (ATT_TQ, ATT_KW), 0)
        kpos = start + lax.broadcasted_iota(jnp.int32, (ATT_TQ, ATT_KW), 1)
        bias = jnp.where(jnp.abs(qpos - kpos) <= WINDOW, 0.0, NEG_INF)
        pairs = [(h, j) for h in range(N_KV_HEADS) for j in range(2)]
        scores = []
        for h, j in pairs:
            kbd = jnp.concatenate([k_ref[0, 2 * h, win, :], k_ref[0, 2 * h + 1, win, :]], axis=0)
            q2 = q_ref[0, qrows, LANES * (2 * h + j):LANES * (2 * h + j + 1)]
            scores.append(lax.dot_general(q2, kbd, (((1,), (1,)), ((), ())), preferred_element_type=F32))
        probs, scales = [], []
        for (h, j), s in zip(pairs, scores):
            ps, inv = [], []
            for e in range(2):
                sk = sink_ref[4 * h + 2 * j + e] * LOG2E
                se = s[:, e * ATT_KW:(e + 1) * ATT_KW] + bias
                m = jnp.maximum(jnp.max(se, axis=-1, keepdims=True), sk)
                p = jnp.exp2(se - m)
                inv.append(1.0 / (jnp.sum(p, axis=-1, keepdims=True) + jnp.exp2(sk - m)))
                ps.append(p.astype(BF16))
            probs.append(jnp.concatenate(ps, axis=1))
            scales.append(jnp.where(lo_q, inv[0], inv[1]))
        for (h, j), p2, sc in zip(pairs, probs, scales):
            vbd = jnp.concatenate([v_ref[0, 2 * h, win, :], v_ref[0, 2 * h + 1, win, :]], axis=0)
            o2 = jnp.dot(p2, vbd, preferred_element_type=F32)
            c0 = LANES * (2 * h + j)
            o_ref[0, qrows, c0:c0 + LANES] = (o2 * sc).astype(BF16)


def _attn_call(sink, q, k, v):
    b = q.shape[0]
    full = pl.BlockSpec((1, 2 * N_KV_HEADS, SEQ, LANES), lambda bi, i: (bi, 0, 0, 0))
    tile = pl.BlockSpec((1, ATT_TQ * ATT_QB, D_ATTN), lambda bi, i: (bi, i, 0))
    return pl.pallas_call(
        _attn_kernel,
        grid=(b, SEQ // (ATT_TQ * ATT_QB)),
        in_specs=[pl.BlockSpec(memory_space=pltpu.SMEM), tile, full, full],
        out_specs=tile,
        out_shape=jax.ShapeDtypeStruct((b, SEQ, D_ATTN), BF16),
        compiler_params=_cparams(("parallel", "parallel")),
        name="attn",
    )(sink, q, k, v)


MIX_TM = 512
MXU_COLS = 256
FF_SPLITS = (0, 6 * MXU_COLS, D_FF)


def _rms(x, g):
    return x * lax.rsqrt(jnp.mean(x * x, axis=-1, keepdims=True) + EPS) * g


def _mix_kernel(x_ref, yh_ref, ya_ref, gate_ref, wuh_ref, wua_ref, wo_ref, g2_ref, wg_ref, wu_ref, wd_ref,
                gf_ref, o_ref):
    yh = jnp.concatenate([yh_ref[0, c] for c in range(N_CB)], axis=1)
    up_h = jnp.dot(yh, wuh_ref[...], preferred_element_type=F32)
    up_a = jnp.dot(ya_ref[0], wua_ref[...], preferred_element_type=F32)
    merged = (gate_ref[0, :, :D_MODEL].astype(F32) * up_h + gate_ref[0, :, D_MODEL:].astype(F32) * up_a)
    x1 = x_ref[0] + jnp.dot(merged.astype(BF16), wo_ref[...], preferred_element_type=F32)
    hn = _rms(x1, g2_ref[...]).astype(BF16)
    acc = x1
    for c in range(len(FF_SPLITS) - 1):
        cols = slice(FF_SPLITS[c], FF_SPLITS[c + 1])
        gate = jnp.dot(hn, wg_ref[:, cols], preferred_element_type=F32)
        up = jnp.dot(hn, wu_ref[:, cols], preferred_element_type=F32)
        act = (jax.nn.silu(gate) * up).astype(BF16)
        acc = acc + jnp.dot(act, wd_ref[cols, :], preferred_element_type=F32)
    o_ref[0] = _rms(acc, gf_ref[...])


def _mix_call(x, yh, ya, gates, wuh, wua, wo, g2, wg, wu, wd, gf):
    b = x.shape[0]
    tile = lambda w: pl.BlockSpec((1, MIX_TM, w), lambda bi, i: (bi, i, 0))
    return pl.pallas_call(
        _mix_kernel,
        grid=(b, SEQ // MIX_TM),
        in_specs=[tile(D_MODEL), pl.BlockSpec((1, N_CB, MIX_TM, CB), lambda bi, i: (bi, 0, i, 0)),
                  tile(D_ATTN), tile(2 * D_MODEL),
                  _const_spec((D_HYENA, D_MODEL)), _const_spec((D_ATTN, D_MODEL)),
                  _const_spec((D_MODEL, D_MODEL)), _const_spec((1, D_MODEL)),
                  _const_spec((D_MODEL, D_FF)), _const_spec((D_MODEL, D_FF)), _const_spec((D_FF, D_MODEL)),
                  _const_spec((1, D_MODEL))],
        out_specs=tile(D_MODEL),
        out_shape=jax.ShapeDtypeStruct((b, SEQ, D_MODEL), F32),
        compiler_params=_cparams(("parallel", "parallel")),
        name="mix",
    )(x, yh, ya, gates, wuh, wua, wo, g2, wg, wu, wd, gf)


def _hyena_bias_blocks(bias):
    rows = jnp.concatenate([bias[None, :], jnp.zeros((7, D_HYENA), F32)], axis=0)
    return rows.reshape(8, N_CB, CB).transpose(1, 0, 2)


def _layer(x, tables, rope, kf, norm1_g, w_in, short_w, short_b, hyena_bias, sink_logit, w_up_hyena,
           w_up_attn, w_out, norm2_g, w_ff_gate, w_ff_up, w_ff_down, final_g):
    kr, ki = kf
    conv_w = jnp.concatenate([short_w, short_b[None, :], jnp.zeros((4, D_UH), F32)], axis=0)
    uh, q, k, v, gates = _proj_call(x, norm1_g[None, :], w_in.astype(BF16), conv_w, *rope)
    blocks = D_HYENA // LANES
    z1 = _hyena_call(uh, 0, uh, blocks, kr, ki, 0, tables, _hyena_bias_blocks(hyena_bias[0]))
    yh = _hyena_call(z1, 0, uh, 2 * blocks, kr, ki, 1, tables, _hyena_bias_blocks(hyena_bias[1]))
    ya = _attn_call(sink_logit, q, k, v)
    return _mix_call(x, yh, ya, gates, w_up_hyena.astype(BF16), w_up_attn.astype(BF16), w_out.astype(BF16),
                     norm2_g[None, :], w_ff_gate.astype(BF16), w_ff_up.astype(BF16), w_ff_down.astype(BF16),
                     final_g[None, :])


def kernel(x_prompt, x_sample, norm1_g, w_in, short_w, short_b, filt_w0, filt_b0, filt_w_inner, filt_b_inner,
           filt_w_out, filt_freq, hyena_bias, sink_logit, w_up_hyena, w_up_attn, w_out, norm2_g, w_ff_gate,
           w_ff_up, w_ff_down, final_g):
    tables = _dft_tables()
    rope = _rope_tables()
    z_emb, deltas = _filter_embedding()
    w0 = jnp.pad(filt_w0[0], ((0, LANES - FILTER_EMB), (0, 0)))
    h = _filter_call(z_emb, w0, filt_b0[0][None, :], filt_w_inner[0], filt_b_inner[0][:, None, :],
                     filt_w_out[0], filt_freq[0][None, :], deltas)
    kf = _kf_call(h, tables[0], tables[1])
    args = (norm1_g[0], w_in[0], short_w[0], short_b[0], hyena_bias[0], sink_logit[0], w_up_hyena[0],
            w_up_attn[0], w_out[0], norm2_g[0], w_ff_gate[0], w_ff_up[0], w_ff_down[0], final_g)
    y_prompt = _layer(x_prompt, tables, rope, kf, *args)
    y_sample = _layer(x_sample, tables, rope, kf, *args)
    return (y_prompt, y_sample)
```

```python
import functools
import math

import jax
import jax.numpy as jnp
from jax import lax
from jax.experimental import pallas as pl
from jax.experimental.pallas import tpu as pltpu

F32 = jnp.float32
BF16 = jnp.bfloat16

D_MODEL = 1024
SEQ = 4096
D_HYENA = 512
HYENA_ORDER = 2
SHORT_CONV = 3
FILTER_EMB = 33
FILTER_ORDER = 64
N_FILT_CH = 2 * HYENA_ORDER * D_HYENA
FAST_DECAY_PCT = 0.3
SLOW_DECAY_PCT = 1.5
DECAY_TARGET = 1e-2
N_Q_HEADS = 8
N_KV_HEADS = 2
HEAD_DIM = 64
D_ATTN = N_Q_HEADS * HEAD_DIM
D_KV = N_KV_HEADS * HEAD_DIM
WINDOW = 128
ROPE_THETA = 500000.0
ROT_DIM = HEAD_DIM // 4
D_UH = (HYENA_ORDER + 1) * D_HYENA
D_IN = D_UH + D_ATTN + 2 * D_KV + 2 * D_MODEL
D_FF = 2816
EPS = 1e-6
NEG_INF = -1e30

LANES = 128
SUBLANES = 8
N_FFT = 2 * SEQ
N1 = 128
N2 = N_FFT // N1
H1 = N1 // 2
Z_STRIDE = N2 + 1
A_STRIDE = N1 + 1
K_STRIDE = N2 + 1
CB = LANES
N_CB = D_HYENA // CB
VMEM_LIMIT = 56 * 1024 * 1024


def _cparams(sem):
    return pltpu.CompilerParams(dimension_semantics=sem, vmem_limit_bytes=VMEM_LIMIT)


def _const_spec(shape):
    nd = len(shape)
    return pl.BlockSpec(shape, lambda *_: (0,) * nd, pipeline_mode=pl.Buffered(1))


def _dft_tables():
    two_pi = 2.0 * math.pi
    n1 = jnp.arange(H1, dtype=jnp.int32)
    n2 = jnp.arange(N2, dtype=jnp.int32)
    k1 = jnp.arange(N1, dtype=jnp.int32)
    m = ((N2 * n1[None, None, :] + n2[:, None, None]) * k1[None, :, None]) % N_FFT
    ang = m.astype(F32) * (two_pi / N_FFT)
    er, ei = jnp.cos(ang), -jnp.sin(ang)
    f1 = jnp.concatenate([jnp.concatenate([er, -ei], axis=2),
                          jnp.concatenate([ei, er], axis=2)], axis=1)
    ert, eit = jnp.swapaxes(er, 1, 2), jnp.swapaxes(ei, 1, 2)
    i2 = jnp.concatenate([ert, eit], axis=1)
    mg = (n2[:, None] * n2[None, :]) % N2
    angg = mg.astype(F32) * (two_pi / N2)
    gr, gi = jnp.cos(angg), -jnp.sin(angg)
    f2 = jnp.concatenate([jnp.concatenate([gr, -gi], axis=1),
                          jnp.concatenate([gi, gr], axis=1)], axis=0)
    f2 = f2.reshape(2, N2 // SUBLANES, SUBLANES, 2 * N2).transpose(1, 0, 2, 3).reshape(2 * N2, 2 * N2)
    i1 = jnp.concatenate([jnp.concatenate([gr, gi], axis=1),
                          jnp.concatenate([-gi, gr], axis=1)], axis=0)
    return f1.astype(BF16), f2.astype(BF16), i1.astype(BF16), i2.astype(BF16)


def _rope_tables():
    half = ROT_DIM // 2
    pos = jnp.arange(SEQ, dtype=F32)
    inv = 1.0 / (ROPE_THETA ** (jnp.arange(0, ROT_DIM, 2, dtype=F32) / ROT_DIM))
    ang = pos[:, None] * inv[None, :]
    cos, sin = jnp.cos(ang), jnp.sin(ang)
    ones = jnp.ones((SEQ, HEAD_DIM - ROT_DIM), F32)
    zeros = jnp.zeros((SEQ, HEAD_DIM - ROT_DIM), F32)
    zh = jnp.zeros((SEQ, half), F32)
    c = jnp.concatenate([cos, cos, ones], axis=1)
    s_up = jnp.concatenate([-sin, zh, zeros], axis=1)
    s_dn = jnp.concatenate([zh, sin, zeros], axis=1)
    rep = LANES // HEAD_DIM
    return jnp.tile(c, (1, rep)), jnp.tile(s_up, (1, rep)), jnp.tile(s_dn, (1, rep))


def _filter_embedding():
    bands = (FILTER_EMB - 1) // 2
    t = jnp.linspace(0.0, 1.0, SEQ, dtype=F32)[:, None]
    w = 2.0 * math.pi * jnp.arange(SEQ, dtype=F32)[:, None] / SEQ
    f = jnp.linspace(1e-4, bands - 1, bands, dtype=F32)[None, :]
    z = jnp.concatenate([t, jnp.cos(f * w), jnp.sin(f * w)], axis=-1)
    z = jnp.pad(z, ((0, 0), (0, LANES - FILTER_EMB)))
    min_decay = math.log(DECAY_TARGET) / SLOW_DECAY_PCT
    max_decay = math.log(DECAY_TARGET) / FAST_DECAY_PCT
    deltas = jnp.tile(jnp.linspace(min_decay, max_decay, D_HYENA, dtype=F32), 2 * HYENA_ORDER)
    return z, jnp.abs(deltas)[None, :]


PROJ_TM = 512
PROJ_HALO = 16
LOG2E = math.log2(math.e)
Q_SCALE = HEAD_DIM ** -0.5 * LOG2E


def _proj_kernel(x_ref, xp_ref, xn_ref, g_ref, w_ref, cw_ref, c_ref, su_ref, sd_ref,
                 uh_ref, q_ref, k_ref, v_ref, gate_ref):
    def norm(xf):
        return xf * lax.rsqrt(jnp.mean(xf * xf, axis=-1, keepdims=True) + EPS) * g_ref[...]

    i = pl.program_id(1)
    hn = norm(x_ref[0]).astype(BF16)
    h_prev = (norm(xp_ref[0]) * (i > 0).astype(F32)).astype(BF16)
    h_next = (norm(xn_ref[0]) * (i < pl.num_programs(1) - 1).astype(F32)).astype(BF16)
    hn_ext = jnp.concatenate([h_prev, hn, h_next], axis=0)

    def proj(c0, width):
        return jnp.dot(hn, w_ref[:, c0:c0 + width], preferred_element_type=F32)

    rows = PROJ_TM + 2 * PROJ_HALO
    for j in range(D_UH // 512):
        cols = slice(j * 512, (j + 1) * 512)
        u = jnp.dot(hn_ext, w_ref[:, cols], preferred_element_type=F32)
        prev = pltpu.roll(u, 1, axis=0)[PROJ_HALO:PROJ_HALO + PROJ_TM]
        nxt = pltpu.roll(u, rows - 1, axis=0)[PROJ_HALO:PROJ_HALO + PROJ_TM]
        cur = u[PROJ_HALO:PROJ_HALO + PROJ_TM]
        uc = (prev * cw_ref[0:1, cols] + cur * cw_ref[1:2, cols] + nxt * cw_ref[2:3, cols] + cw_ref[3:4, cols])
        for c in range(512 // LANES):
            uh_ref[0, j * (512 // LANES) + c] = uc[:, c * LANES:(c + 1) * LANES].astype(BF16)

    def rope(xc):
        return (xc * c_ref[...] + pltpu.roll(xc, LANES - ROT_DIM // 2, axis=1) * su_ref[...]
                + pltpu.roll(xc, ROT_DIM // 2, axis=1) * sd_ref[...])

    qkv = proj(D_UH, D_ATTN + 2 * D_KV)
    for j in range(D_ATTN // LANES):
        qc = qkv[:, j * LANES:(j + 1) * LANES]
        q_ref[0, :, j * LANES:(j + 1) * LANES] = (rope(qc) * Q_SCALE).astype(BF16)
    lo = lax.broadcasted_iota(jnp.int32, (PROJ_TM, LANES), 1) < HEAD_DIM
    for val, ref in ((rope(qkv[:, D_ATTN:D_ATTN + D_KV]), k_ref), (qkv[:, D_ATTN + D_KV:], v_ref)):
        rolled = pltpu.roll(val, HEAD_DIM, axis=1)
        ref[0, 0] = jnp.where(lo, val, 0.0).astype(BF16)
        ref[0, 1] = jnp.where(lo, 0.0, rolled).astype(BF16)
        ref[0, 2] = jnp.where(lo, rolled, 0.0).astype(BF16)
        ref[0, 3] = jnp.where(lo, 0.0, val).astype(BF16)
    g0 = D_UH + D_ATTN + 2 * D_KV
    for j in range(2 * D_MODEL // 512):
        gate_ref[0, :, j * 512:(j + 1) * 512] = jax.nn.sigmoid(proj(g0 + j * 512, 512)).astype(BF16)


def _proj_call(x, g1, w_in, conv_w, rope_c, rope_su, rope_sd):
    b = x.shape[0]
    nt = SEQ // PROJ_TM
    per_tile = PROJ_TM // PROJ_HALO
    tile = lambda w: pl.BlockSpec((1, PROJ_TM, w), lambda bi, i: (bi, i, 0))
    halo_prev = pl.BlockSpec((1, PROJ_HALO, D_MODEL), lambda bi, i: (bi, jnp.maximum(i * per_tile - 1, 0), 0))
    halo_next = pl.BlockSpec((1, PROJ_HALO, D_MODEL),
                             lambda bi, i: (bi, jnp.minimum((i + 1) * per_tile, SEQ // PROJ_HALO - 1), 0))
    rope_spec = pl.BlockSpec((PROJ_TM, LANES), lambda bi, i: (i, 0))
    kv_spec = pl.BlockSpec((1, 2 * N_KV_HEADS, PROJ_TM, LANES), lambda bi, i: (bi, 0, i, 0))
    return pl.pallas_call(
        _proj_kernel,
        grid=(b, nt),
        in_specs=[tile(D_MODEL), halo_prev, halo_next, _const_spec((1, D_MODEL)), _const_spec((D_MODEL, D_IN)),
                  _const_spec((8, D_UH)), rope_spec, rope_spec, rope_spec],
        out_specs=[pl.BlockSpec((1, D_UH // LANES, PROJ_TM, LANES), lambda bi, i: (bi, 0, i, 0)),
                   tile(D_ATTN), kv_spec, kv_spec, tile(2 * D_MODEL)],
        out_shape=[jax.ShapeDtypeStruct((b, D_UH // LANES, SEQ, LANES), BF16),
                   jax.ShapeDtypeStruct((b, SEQ, D_ATTN), BF16),
                   jax.ShapeDtypeStruct((b, 2 * N_KV_HEADS, SEQ, LANES), BF16),
                   jax.ShapeDtypeStruct((b, 2 * N_KV_HEADS, SEQ, LANES), BF16),
                   jax.ShapeDtypeStruct((b, SEQ, 2 * D_MODEL), BF16)],
        compiler_params=_cparams(("parallel", "parallel")),
        name="proj",
    )(x, x, x, g1, w_in, conv_w, rope_c, rope_su, rope_sd)


FILT_TL = 512


def _filter_kernel(z_ref, w0_ref, b0_ref, wi_ref, bi_ref, wo_ref, fr_ref, dl_ref, h_ref):
    hi = lax.Precision.HIGHEST
    z = z_ref[...]
    fr = fr_ref[...]
    h = jnp.sin(fr * (jnp.dot(z, w0_ref[...], precision=hi, preferred_element_type=F32) + b0_ref[...]))
    for i in range(wi_ref.shape[0]):
        h = jnp.sin(fr * (jnp.dot(h, wi_ref[i], precision=hi, preferred_element_type=F32) + bi_ref[i]))
    def split(v):
        top = v.astype(BF16)
        return top, (v - top.astype(F32)).astype(BF16)
    h_hi, h_lo = split(h)
    w_hi, w_lo = split(wo_ref[...])
    out = (jnp.dot(h_hi, w_hi, preferred_element_type=F32) + jnp.dot(h_hi, w_lo, preferred_element_type=F32)
           + jnp.dot(h_lo, w_hi, preferred_element_type=F32))
    t = z[:, 0:1]
    h_ref[...] = out * jnp.exp(-t * dl_ref[...])


def _filter_call(z, w0, b0, wi, bi, wo, fr, deltas):
    n_inner = wi.shape[0]
    return pl.pallas_call(
        _filter_kernel,
        grid=(SEQ // FILT_TL,),
        in_specs=[pl.BlockSpec((FILT_TL, LANES), lambda i: (i, 0)),
                  _const_spec((LANES, FILTER_ORDER)), _const_spec((1, FILTER_ORDER)),
                  _const_spec((n_inner, FILTER_ORDER, FILTER_ORDER)), _const_spec((n_inner, 1, FILTER_ORDER)),
                  _const_spec((FILTER_ORDER, N_FILT_CH)), _const_spec((1, FILTER_ORDER)),
                  _const_spec((1, N_FILT_CH))],
        out_specs=pl.BlockSpec((FILT_TL, N_FILT_CH), lambda i: (i, 0)),
        out_shape=jax.ShapeDtypeStruct((SEQ, N_FILT_CH), F32),
        compiler_params=_cparams(("parallel",)),
        name="filt",
    )(z, w0, b0, wi, bi, wo, fr, deltas)


STAGE_UNROLL = 64
KB = 2
MID_UNROLL = 64


def _stage_f1(z_ref, a_ref, f1_ref, real_only):
    def body(n2, carry):
        zr = z_ref[0, pl.ds(n2, H1, stride=Z_STRIDE), :]
        if real_only:
            res = jnp.dot(f1_ref[n2][:, :H1], zr.astype(BF16), preferred_element_type=F32)
        else:
            zi = z_ref[1, pl.ds(n2, H1, stride=Z_STRIDE), :]
            st = jnp.concatenate([zr, zi], axis=0).astype(BF16)
            res = jnp.dot(f1_ref[n2], st, preferred_element_type=F32)
        base = n2 * A_STRIDE
        a_ref[0, pl.ds(base, N1), :] = res[:N1]
        a_ref[1, pl.ds(base, N1), :] = res[N1:]
        return carry
    lax.fori_loop(0, N2, body, 0, unroll=STAGE_UNROLL)


def _k1_rows(k1):
    return pl.ds(k1 * K_STRIDE, N2)


def _dft_blocks(mat_ref, vr, vi, interleaved):
    st = jnp.concatenate([vr, vi], axis=0).astype(BF16)
    x = jnp.dot(mat_ref[...], st, preferred_element_type=F32)
    if not interleaved:
        return x[:N2], x[N2:]
    x4 = x.reshape(N2 // SUBLANES, 2, SUBLANES, x.shape[-1])
    return x4[:, 0].reshape(N2, x.shape[-1]), x4[:, 1].reshape(N2, x.shape[-1])


def _load_spectrum_blocks(a_ref, f2_ref, k1b):
    ar = jnp.concatenate([a_ref[0, pl.ds(k1b + i, N2, stride=A_STRIDE), :] for i in range(KB)], axis=1)
    ai = jnp.concatenate([a_ref[1, pl.ds(k1b + i, N2, stride=A_STRIDE), :] for i in range(KB)], axis=1)
    return _dft_blocks(f2_ref, ar, ai, True)


def _spectrum_rows(k1b, i):
    return pl.ds(pl.multiple_of((k1b + i) * N2, N2), N2)


def _kf_kernel(hf_ref, hb_ref, f1_ref, f2_ref, kr_ref, ki_ref, z_ref, ab_ref, acc_ref):
    inv_n = 1.0 / N_FFT

    def load_time(src_ref, drop_first):
        def body(j, carry):
            rows = pl.multiple_of(j * N2, N2)
            v = src_ref[pl.ds(rows, N2), :]
            if drop_first:
                ridx = lax.broadcasted_iota(jnp.int32, (N2, CB), 0) + rows
                v = jnp.where(ridx == 0, 0.0, v)
            z_ref[0, pl.ds(j * Z_STRIDE, N2), :] = v
            return carry
        lax.fori_loop(0, H1, body, 0, unroll=4)

    def spectrum(accumulate):
        def body(it, carry):
            k1b = it * KB
            xr, xi = _load_spectrum_blocks(ab_ref, f2_ref, k1b)
            for i in range(KB):
                rows = _spectrum_rows(k1b, i)
                lanes = slice(i * CB, (i + 1) * CB)
                if accumulate:
                    kr_ref[rows, :] = (acc_ref[0, rows, :] + xr[:, lanes] * inv_n).astype(BF16)
                    ki_ref[rows, :] = (acc_ref[1, rows, :] - xi[:, lanes] * inv_n).astype(BF16)
                else:
                    acc_ref[0, rows, :] = xr[:, lanes] * inv_n
                    acc_ref[1, rows, :] = xi[:, lanes] * inv_n
            return carry
        lax.fori_loop(0, N1 // KB, body, 0, unroll=MID_UNROLL)

    load_time(hf_ref, False)
    _stage_f1(z_ref, ab_ref, f1_ref, True)
    spectrum(False)
    load_time(hb_ref, True)
    _stage_f1(z_ref, ab_ref, f1_ref, True)
    spectrum(True)


def _kf_call(h, f1, f2):
    blocks_per_order = N_CB
    fwd = pl.BlockSpec((SEQ, CB), lambda o, c: (0, o * blocks_per_order + c))
    bwd = pl.BlockSpec((SEQ, CB), lambda o, c: (0, (HYENA_ORDER + o) * blocks_per_order + c))
    out = pl.BlockSpec((None, N_FFT, CB), lambda o, c: (o, 0, c))
    return pl.pallas_call(
        _kf_kernel,
        grid=(HYENA_ORDER, N_CB),
        in_specs=[fwd, bwd, _const_spec((N2, 2 * N1, 2 * H1)), _const_spec((2 * N2, 2 * N2))],
        out_specs=[out, out],
        out_shape=[jax.ShapeDtypeStruct((HYENA_ORDER, N_FFT, D_HYENA), BF16)] * 2,
        scratch_shapes=[pltpu.VMEM((2, H1 * Z_STRIDE, CB), F32), pltpu.VMEM((2, N2 * A_STRIDE, CB), F32),
                        pltpu.VMEM((2, N_FFT, CB), F32)],
        compiler_params=_cparams(("parallel", "parallel")),
        name="kf",
    )(h, h, f1, f2)


TIME_ROWS = 256
GROUPS = TIME_ROWS // N2


def _group_rows(j, i):
    return pl.ds((j * GROUPS + i) * Z_STRIDE, N2)


def _hyena_kernel(a_ref, g_ref, kr_ref, ki_ref, f1_ref, f2_ref, i1_ref, i2_ref, w_ref, o_ref,
                  z_ref, s1_ref, s2_ref):
    def fill(j, carry):
        rows = pl.ds(pl.multiple_of(j * TIME_ROWS, TIME_ROWS), TIME_ROWS)
        for r in range(2):
            val = a_ref[r, rows, :].astype(F32)
            for i in range(GROUPS):
                z_ref[r, _group_rows(j, i), :] = val[i * N2:(i + 1) * N2]
        return carry
    lax.fori_loop(0, SEQ // TIME_ROWS, fill, 0, unroll=2)

    _stage_f1(z_ref, s1_ref, f1_ref, False)

    def store_blocks(k1b, vr, vi):
        for i in range(KB):
            lanes = slice(i * CB, (i + 1) * CB)
            s2_ref[0, _k1_rows(k1b + i), :] = vr[:, lanes]
            s2_ref[1, _k1_rows(k1b + i), :] = vi[:, lanes]

    def mid_fwd(it, carry):
        k1b = it * KB
        xr, xi = _load_spectrum_blocks(s1_ref, f2_ref, k1b)
        kr = jnp.concatenate([kr_ref[_spectrum_rows(k1b, i), :] for i in range(KB)], axis=1).astype(F32)
        ki = jnp.concatenate([ki_ref[_spectrum_rows(k1b, i), :] for i in range(KB)], axis=1).astype(F32)
        store_blocks(k1b, xr * kr - xi * ki, xr * ki + xi * kr)
        return carry
    lax.fori_loop(0, N1 // KB, mid_fwd, 0, unroll=MID_UNROLL)

    def mid_inv(it, carry):
        k1b = it * KB
        yr = jnp.concatenate([s2_ref[0, _k1_rows(k1b + i), :] for i in range(KB)], axis=1)
        yi = jnp.concatenate([s2_ref[1, _k1_rows(k1b + i), :] for i in range(KB)], axis=1)
        br, bi = _dft_blocks(i1_ref, yr, yi, False)
        store_blocks(k1b, br, bi)
        return carry
    lax.fori_loop(0, N1 // KB, mid_inv, 0, unroll=MID_UNROLL)

    y_ref = s1_ref

    def last(n2, carry):
        st = jnp.concatenate([s2_ref[0, pl.ds(n2, N1, stride=K_STRIDE), :],
                              s2_ref[1, pl.ds(n2, N1, stride=K_STRIDE), :]], axis=1).astype(BF16)
        p = jnp.dot(i2_ref[n2], st, preferred_element_type=F32)
        y_ref[0, pl.ds(n2, H1, stride=Z_STRIDE), :] = p[:H1, :CB] + p[H1:, CB:]
        y_ref[1, pl.ds(n2, H1, stride=Z_STRIDE), :] = p[:H1, CB:] - p[H1:, :CB]
        return carry
    lax.fori_loop(0, N2, last, 0, unroll=STAGE_UNROLL)

    def finish(j, carry):
        bias = w_ref[0:1]
        rows = pl.ds(pl.multiple_of(j * TIME_ROWS, TIME_ROWS), TIME_ROWS)
        for r in range(2):
            y = jnp.concatenate([y_ref[r, _group_rows(j, i), :] for i in range(GROUPS)], axis=0)
            z = jnp.concatenate([z_ref[r, _group_rows(j, i), :] for i in range(GROUPS)], axis=0)
            o_ref[r, rows, :] = (g_ref[r, rows, :].astype(F32) * (y + bias * z)).astype(BF16)
        return carry
    lax.fori_loop(0, SEQ // TIME_ROWS, finish, 0, unroll=2)


def _hyena_call(a, a_slot, g, g_slot, kr, ki, order, tables, bias):
    b = a.shape[0]
    f1, f2, i1, i2 = tables
    io = lambda slot: pl.BlockSpec((2, None, SEQ, CB), lambda c, p: (p, slot + c, 0, 0))
    kspec = pl.BlockSpec((None, N_FFT, CB), lambda c, p: (order, 0, c))
    assert N2 * A_STRIDE >= H1 * Z_STRIDE
    return pl.pallas_call(
        _hyena_kernel,
        grid=(N_CB, b // 2),
        in_specs=[io(a_slot), io(g_slot), kspec, kspec,
                  _const_spec((N2, 2 * N1, 2 * H1)), _const_spec((2 * N2, 2 * N2)),
                  _const_spec((2 * N2, 2 * N2)), _const_spec((N2, 2 * H1, N1)),
                  pl.BlockSpec((None, 8, CB), lambda c, p: (c, 0, 0))],
        out_specs=io(0),
        out_shape=jax.ShapeDtypeStruct((b, N_CB, SEQ, CB), BF16),
        scratch_shapes=[pltpu.VMEM((2, H1 * Z_STRIDE, CB), F32), pltpu.VMEM((2, N2 * A_STRIDE, CB), F32),
                        pltpu.VMEM((2, N1 * K_STRIDE, CB), F32)],
        compiler_params=_cparams(("parallel", "parallel")),
        name=f"hyena{order}",
    )(a, g, kr, ki, f1, f2, i1, i2, bias)


ATT_TQ = 128
ATT_KW = ATT_TQ + 2 * WINDOW


ATT_QB = 8


def _attn_kernel(sink_ref, q_ref, k_ref, v_ref, o_ref):
    lo_q = lax.broadcasted_iota(jnp.int32, (ATT_TQ, LANES), 1) < HEAD_DIM
    for qb in range(ATT_QB):
        i = pl.program_id(1) * ATT_QB + qb
        start = pl.multiple_of(jnp.clip(i * ATT_TQ - WINDOW, 0, SEQ - ATT_KW), LANES)
        win = pl.ds(start, ATT_KW)
        qrows = slice(qb * ATT_TQ, (qb + 1) * ATT_TQ)
        qpos = i * ATT_TQ + lax.broadcasted_iota(jnp.int32, (ATT_TQ, ATT_KW), 0)
        kpos = start + lax.broadcasted_iota(jnp.int32, (ATT_TQ, ATT_KW), 1)
        bias = jnp.where(jnp.abs(qpos - kpos) <= WINDOW, 0.0, NEG_INF).astype(BF16)
        pairs = [(h, j) for h in range(N_KV_HEADS) for j in range(2)]
        scores = []
        for h, j in pairs:
            kbd = jnp.concatenate([k_ref[0, 2 * h, win, :], k_ref[0, 2 * h + 1, win, :]], axis=0)
            q2 = q_ref[0, qrows, LANES * (2 * h + j):LANES * (2 * h + j + 1)]
            scores.append(lax.dot_general(q2, kbd, (((1,), (1,)), ((), ())), preferred_element_type=F32))
        probs, scales = [], []
        for (h, j), s in zip(pairs, scores):
            ps, inv = [], []
            for e in range(2):
                sk = sink_ref[4 * h + 2 * j + e] * LOG2E
                se = s[:, e * ATT_KW:(e + 1) * ATT_KW].astype(BF16) + bias
                m = jnp.maximum(jnp.max(se, axis=-1, keepdims=True).astype(F32), sk)
                p = jnp.exp2(se - m.astype(BF16))
                part = functools.reduce(lambda a, b: a + b,
                                        [p[:, c * LANES:(c + 1) * LANES] for c in range(ATT_KW // LANES)])
                inv.append(1.0 / (jnp.sum(part.astype(F32), axis=-1, keepdims=True) + jnp.exp2(sk - m)))
                ps.append(p)
            probs.append(jnp.concatenate(ps, axis=1))
            scales.append(jnp.where(lo_q, inv[0], inv[1]))
        for (h, j), p2, sc in zip(pairs, probs, scales):
            vbd = jnp.concatenate([v_ref[0, 2 * h, win, :], v_ref[0, 2 * h + 1, win, :]], axis=0)
            o2 = jnp.dot(p2, vbd, preferred_element_type=F32)
            c0 = LANES * (2 * h + j)
            o_ref[0, qrows, c0:c0 + LANES] = (o2 * sc).astype(BF16)


def _attn_call(sink, q, k, v):
    b = q.shape[0]
    full = pl.BlockSpec((1, 2 * N_KV_HEADS, SEQ, LANES), lambda bi, i: (bi, 0, 0, 0))
    tile = pl.BlockSpec((1, ATT_TQ * ATT_QB, D_ATTN), lambda bi, i: (bi, i, 0))
    return pl.pallas_call(
        _attn_kernel,
        grid=(b, SEQ // (ATT_TQ * ATT_QB)),
        in_specs=[pl.BlockSpec(memory_space=pltpu.SMEM), tile, full, full],
        out_specs=tile,
        out_shape=jax.ShapeDtypeStruct((b, SEQ, D_ATTN), BF16),
        compiler_params=_cparams(("parallel", "parallel")),
        name="attn",
    )(sink, q, k, v)


MIX_TM = 512
MXU_COLS = 256
FF_SPLITS = (0, 6 * MXU_COLS, D_FF)


def _rms(x, g):
    return x * lax.rsqrt(jnp.mean(x * x, axis=-1, keepdims=True) + EPS) * g


def _mix_kernel(x_ref, yh_ref, ya_ref, gate_ref, wuh_ref, wua_ref, wo_ref, g2_ref, wg_ref, wu_ref, wd_ref,
                gf_ref, o_ref):
    yh = jnp.concatenate([yh_ref[0, c] for c in range(N_CB)], axis=1)
    up_h = jnp.dot(yh, wuh_ref[...], preferred_element_type=F32)
    up_a = jnp.dot(ya_ref[0], wua_ref[...], preferred_element_type=F32)
    merged = (gate_ref[0, :, :D_MODEL].astype(F32) * up_h + gate_ref[0, :, D_MODEL:].astype(F32) * up_a)
    x1 = x_ref[0] + jnp.dot(merged.astype(BF16), wo_ref[...], preferred_element_type=F32)
    hn = _rms(x1, g2_ref[...]).astype(BF16)
    acc = x1
    for c in range(len(FF_SPLITS) - 1):
        cols = slice(FF_SPLITS[c], FF_SPLITS[c + 1])
        gate = jnp.dot(hn, wg_ref[:, cols], preferred_element_type=F32)
        up = jnp.dot(hn, wu_ref[:, cols], preferred_element_type=F32)
        act = (jax.nn.silu(gate) * up).astype(BF16)
        acc = acc + jnp.dot(act, wd_ref[cols, :], preferred_element_type=F32)
    o_ref[0] = _rms(acc, gf_ref[...])


def _mix_call(x, yh, ya, gates, wuh, wua, wo, g2, wg, wu, wd, gf):
    b = x.shape[0]
    tile = lambda w: pl.BlockSpec((1, MIX_TM, w), lambda bi, i: (bi, i, 0))
    return pl.pallas_call(
        _mix_kernel,
        grid=(b, SEQ // MIX_TM),
        in_specs=[tile(D_MODEL), pl.BlockSpec((1, N_CB, MIX_TM, CB), lambda bi, i: (bi, 0, i, 0)),
                  tile(D_ATTN), tile(2 * D_MODEL),
                  _const_spec((D_HYENA, D_MODEL)), _const_spec((D_ATTN, D_MODEL)),
                  _const_spec((D_MODEL, D_MODEL)), _const_spec((1, D_MODEL)),
                  _const_spec((D_MODEL, D_FF)), _const_spec((D_MODEL, D_FF)), _const_spec((D_FF, D_MODEL)),
                  _const_spec((1, D_MODEL))],
        out_specs=tile(D_MODEL),
        out_shape=jax.ShapeDtypeStruct((b, SEQ, D_MODEL), F32),
        compiler_params=_cparams(("parallel", "parallel")),
        name="mix",
    )(x, yh, ya, gates, wuh, wua, wo, g2, wg, wu, wd, gf)


def _hyena_bias_blocks(bias):
    rows = jnp.concatenate([bias[None, :], jnp.zeros((7, D_HYENA), F32)], axis=0)
    return rows.reshape(8, N_CB, CB).transpose(1, 0, 2)


def _layer(x, tables, rope, kf, norm1_g, w_in, short_w, short_b, hyena_bias, sink_logit, w_up_hyena,
           w_up_attn, w_out, norm2_g, w_ff_gate, w_ff_up, w_ff_down, final_g):
    kr, ki = kf
    conv_w = jnp.concatenate([short_w, short_b[None, :], jnp.zeros((4, D_UH), F32)], axis=0)
    uh, q, k, v, gates = _proj_call(x, norm1_g[None, :], w_in.astype(BF16), conv_w, *rope)
    blocks = D_HYENA // LANES
    z1 = _hyena_call(uh, 0, uh, blocks, kr, ki, 0, tables, _hyena_bias_blocks(hyena_bias[0]))
    yh = _hyena_call(z1, 0, uh, 2 * blocks, kr, ki, 1, tables, _hyena_bias_blocks(hyena_bias[1]))
    ya = _attn_call(sink_logit, q, k, v)
    return _mix_call(x, yh, ya, gates, w_up_hyena.astype(BF16), w_up_attn.astype(BF16), w_out.astype(BF16),
                     norm2_g[None, :], w_ff_gate.astype(BF16), w_ff_up.astype(BF16), w_ff_down.astype(BF16),
                     final_g[None, :])


def kernel(x_prompt, x_sample, norm1_g, w_in, short_w, short_b, filt_w0, filt_b0, filt_w_inner, filt_b_inner,
           filt_w_out, filt_freq, hyena_bias, sink_logit, w_up_hyena, w_up_attn, w_out, norm2_g, w_ff_gate,
           w_ff_up, w_ff_down, final_g):
    tables = _dft_tables()
    rope = _rope_tables()
    z_emb, deltas = _filter_embedding()
    w0 = jnp.pad(filt_w0[0], ((0, LANES - FILTER_EMB), (0, 0)))
    h = _filter_call(z_emb, w0, filt_b0[0][None, :], filt_w_inner[0], filt_b_inner[0][:, None, :],
                     filt_w_out[0], filt_freq[0][None, :], deltas)
    kf = _kf_call(h, tables[0], tables[1])
    args = (norm1_g[0], w_in[0], short_w[0], short_b[0], hyena_bias[0], sink_logit[0], w_up_hyena[0],
            w_up_attn[0], w_out[0], norm2_g[0], w_ff_gate[0], w_ff_up[0], w_ff_down[0], final_g)
    y_prompt = _layer(x_prompt, tables, rope, kf, *args)
    y_sample = _layer(x_sample, tables, rope, kf, *args)
    return (y_prompt, y_sample)
```

```python
import functools
import math

import jax
import jax.numpy as jnp
from jax import lax
from jax.experimental import pallas as pl
from jax.experimental.pallas import tpu as pltpu

F32 = jnp.float32
BF16 = jnp.bfloat16

D_MODEL = 1024
SEQ = 4096
D_HYENA = 512
HYENA_ORDER = 2
SHORT_CONV = 3
FILTER_EMB = 33
FILTER_ORDER = 64
N_FILT_CH = 2 * HYENA_ORDER * D_HYENA
FAST_DECAY_PCT = 0.3
SLOW_DECAY_PCT = 1.5
DECAY_TARGET = 1e-2
N_Q_HEADS = 8
N_KV_HEADS = 2
HEAD_DIM = 64
D_ATTN = N_Q_HEADS * HEAD_DIM
D_KV = N_KV_HEADS * HEAD_DIM
WINDOW = 128
ROPE_THETA = 500000.0
ROT_DIM = HEAD_DIM // 4
D_UH = (HYENA_ORDER + 1) * D_HYENA
D_IN = D_UH + D_ATTN + 2 * D_KV + 2 * D_MODEL
D_FF = 2816
EPS = 1e-6
NEG_INF = -1e30

LANES = 128
SUBLANES = 8
N_FFT = 2 * SEQ
N1 = 128
N2 = N_FFT // N1
H1 = N1 // 2
Z_STRIDE = N2 + 1
A_STRIDE = N1 + 1
K_STRIDE = N2 + 1
CB = LANES
N_CB = D_HYENA // CB
VMEM_LIMIT = 56 * 1024 * 1024


def _cparams(sem):
    return pltpu.CompilerParams(dimension_semantics=sem, vmem_limit_bytes=VMEM_LIMIT)


def _const_spec(shape):
    nd = len(shape)
    return pl.BlockSpec(shape, lambda *_: (0,) * nd, pipeline_mode=pl.Buffered(1))


def _dft_tables():
    two_pi = 2.0 * math.pi
    n1 = jnp.arange(H1, dtype=jnp.int32)
    n2 = jnp.arange(N2, dtype=jnp.int32)
    k1 = jnp.arange(N1, dtype=jnp.int32)
    ang_a = ((n1[None, :] * k1[:, None]) % N1).astype(F32) * (two_pi / N1)
    ang_t = (n2[:, None] * k1[None, :]).astype(F32) * (two_pi / N_FFT)
    ar, ai = jnp.cos(ang_a)[None], -jnp.sin(ang_a)[None]
    tr, ti = jnp.cos(ang_t)[:, :, None], -jnp.sin(ang_t)[:, :, None]
    er, ei = ar * tr - ai * ti, ar * ti + ai * tr
    f1 = jnp.concatenate([jnp.concatenate([er, -ei], axis=2),
                          jnp.concatenate([ei, er], axis=2)], axis=1)
    ert, eit = jnp.swapaxes(er, 1, 2), jnp.swapaxes(ei, 1, 2)
    i2 = jnp.concatenate([ert, eit], axis=1)
    mg = (n2[:, None] * n2[None, :]) % N2
    angg = mg.astype(F32) * (two_pi / N2)
    gr, gi = jnp.cos(angg), -jnp.sin(angg)
    f2 = jnp.concatenate([jnp.concatenate([gr, -gi], axis=1),
                          jnp.concatenate([gi, gr], axis=1)], axis=0)
    f2 = f2.reshape(2, N2 // SUBLANES, SUBLANES, 2 * N2).transpose(1, 0, 2, 3).reshape(2 * N2, 2 * N2)
    i1 = jnp.concatenate([jnp.concatenate([gr, gi], axis=1),
                          jnp.concatenate([-gi, gr], axis=1)], axis=0)
    return f1.astype(BF16), f2.astype(BF16), i1.astype(BF16), i2.astype(BF16)


def _rope_tables():
    half = ROT_DIM // 2
    pos = jnp.arange(SEQ, dtype=F32)
    inv = 1.0 / (ROPE_THETA ** (jnp.arange(0, ROT_DIM, 2, dtype=F32) / ROT_DIM))
    ang = pos[:, None] * inv[None, :]
    cos, sin = jnp.cos(ang), jnp.sin(ang)
    ones = jnp.ones((SEQ, HEAD_DIM - ROT_DIM), F32)
    zeros = jnp.zeros((SEQ, HEAD_DIM - ROT_DIM), F32)
    zh = jnp.zeros((SEQ, half), F32)
    c = jnp.concatenate([cos, cos, ones], axis=1)
    s_up = jnp.concatenate([-sin, zh, zeros], axis=1)
    s_dn = jnp.concatenate([zh, sin, zeros], axis=1)
    rep = LANES // HEAD_DIM
    return jnp.tile(c, (1, rep)), jnp.tile(s_up, (1, rep)), jnp.tile(s_dn, (1, rep))


def _filter_embedding():
    bands = (FILTER_EMB - 1) // 2
    t = jnp.linspace(0.0, 1.0, SEQ, dtype=F32)[:, None]
    w = 2.0 * math.pi * jnp.arange(SEQ, dtype=F32)[:, None] / SEQ
    f = jnp.linspace(1e-4, bands - 1, bands, dtype=F32)[None, :]
    z = jnp.concatenate([t, jnp.cos(f * w), jnp.sin(f * w)], axis=-1)
    z = jnp.pad(z, ((0, 0), (0, LANES - FILTER_EMB)))
    min_decay = math.log(DECAY_TARGET) / SLOW_DECAY_PCT
    max_decay = math.log(DECAY_TARGET) / FAST_DECAY_PCT
    deltas = jnp.tile(jnp.linspace(min_decay, max_decay, D_HYENA, dtype=F32), 2 * HYENA_ORDER)
    return z, jnp.abs(deltas)[None, :]


PROJ_TM = 1024
PROJ_HALO = 16
LOG2E = math.log2(math.e)
Q_SCALE = HEAD_DIM ** -0.5 * LOG2E


def _proj_kernel(x_ref, xp_ref, xn_ref, g_ref, w_ref, cw_ref, c_ref, su_ref, sd_ref,
                 uh_ref, q_ref, k_ref, v_ref, gate_ref):
    def norm(xf):
        return xf * lax.rsqrt(jnp.mean(xf * xf, axis=-1, keepdims=True) + EPS) * g_ref[...]

    i = pl.program_id(1)
    hn = norm(x_ref[0]).astype(BF16)
    h_prev = (norm(xp_ref[0]) * (i > 0).astype(F32)).astype(BF16)
    h_next = (norm(xn_ref[0]) * (i < pl.num_programs(1) - 1).astype(F32)).astype(BF16)
    hn_ext = jnp.concatenate([h_prev, hn, h_next], axis=0)

    def proj(c0, width):
        return jnp.dot(hn, w_ref[:, c0:c0 + width], preferred_element_type=F32)

    rows = PROJ_TM + 2 * PROJ_HALO
    for j in range(D_UH // 512):
        cols = slice(j * 512, (j + 1) * 512)
        u = jnp.dot(hn_ext, w_ref[:, cols], preferred_element_type=F32)
        prev = pltpu.roll(u, 1, axis=0)[PROJ_HALO:PROJ_HALO + PROJ_TM]
        nxt = pltpu.roll(u, rows - 1, axis=0)[PROJ_HALO:PROJ_HALO + PROJ_TM]
        cur = u[PROJ_HALO:PROJ_HALO + PROJ_TM]
        uc = (prev * cw_ref[0:1, cols] + cur * cw_ref[1:2, cols] + nxt * cw_ref[2:3, cols] + cw_ref[3:4, cols])
        for c in range(512 // LANES):
            uh_ref[0, j * (512 // LANES) + c] = uc[:, c * LANES:(c + 1) * LANES].astype(BF16)

    def rope(xc):
        return (xc * c_ref[...] + pltpu.roll(xc, LANES - ROT_DIM // 2, axis=1) * su_ref[...]
                + pltpu.roll(xc, ROT_DIM // 2, axis=1) * sd_ref[...])

    qkv = proj(D_UH, D_ATTN + 2 * D_KV)
    for j in range(D_ATTN // LANES):
        qc = qkv[:, j * LANES:(j + 1) * LANES]
        q_ref[0, :, j * LANES:(j + 1) * LANES] = (rope(qc) * Q_SCALE).astype(BF16)
    lo = lax.broadcasted_iota(jnp.int32, (PROJ_TM, LANES), 1) < HEAD_DIM
    for val, ref in ((rope(qkv[:, D_ATTN:D_ATTN + D_KV]), k_ref), (qkv[:, D_ATTN + D_KV:], v_ref)):
        rolled = pltpu.roll(val, HEAD_DIM, axis=1)
        ref[0, 0] = jnp.where(lo, val, 0.0).astype(BF16)
        ref[0, 1] = jnp.where(lo, 0.0, rolled).astype(BF16)
        ref[0, 2] = jnp.where(lo, rolled, 0.0).astype(BF16)
        ref[0, 3] = jnp.where(lo, 0.0, val).astype(BF16)
    g0 = D_UH + D_ATTN + 2 * D_KV
    for j in range(2 * D_MODEL // 512):
        gate_ref[0, :, j * 512:(j + 1) * 512] = jax.nn.sigmoid(proj(g0 + j * 512, 512)).astype(BF16)


def _proj_call(x, g1, w_in, conv_w, rope_c, rope_su, rope_sd):
    b = x.shape[0]
    nt = SEQ // PROJ_TM
    per_tile = PROJ_TM // PROJ_HALO
    tile = lambda w: pl.BlockSpec((1, PROJ_TM, w), lambda bi, i: (bi, i, 0))
    halo_prev = pl.BlockSpec((1, PROJ_HALO, D_MODEL), lambda bi, i: (bi, jnp.maximum(i * per_tile - 1, 0), 0))
    halo_next = pl.BlockSpec((1, PROJ_HALO, D_MODEL),
                             lambda bi, i: (bi, jnp.minimum((i + 1) * per_tile, SEQ // PROJ_HALO - 1), 0))
    rope_spec = pl.BlockSpec((PROJ_TM, LANES), lambda bi, i: (i, 0))
    kv_spec = pl.BlockSpec((1, 2 * N_KV_HEADS, PROJ_TM, LANES), lambda bi, i: (bi, 0, i, 0))
    return pl.pallas_call(
        _proj_kernel,
        grid=(b, nt),
        in_specs=[tile(D_MODEL), halo_prev, halo_next, _const_spec((1, D_MODEL)), _const_spec((D_MODEL, D_IN)),
                  _const_spec((8, D_UH)), rope_spec, rope_spec, rope_spec],
        out_specs=[pl.BlockSpec((1, D_UH // LANES, PROJ_TM, LANES), lambda bi, i: (bi, 0, i, 0)),
                   tile(D_ATTN), kv_spec, kv_spec, tile(2 * D_MODEL)],
        out_shape=[jax.ShapeDtypeStruct((b, D_UH // LANES, SEQ, LANES), BF16),
                   jax.ShapeDtypeStruct((b, SEQ, D_ATTN), BF16),
                   jax.ShapeDtypeStruct((b, 2 * N_KV_HEADS, SEQ, LANES), BF16),
                   jax.ShapeDtypeStruct((b, 2 * N_KV_HEADS, SEQ, LANES), BF16),
                   jax.ShapeDtypeStruct((b, SEQ, 2 * D_MODEL), BF16)],
        compiler_params=_cparams(("parallel", "parallel")),
        name="proj",
    )(x, x, x, g1, w_in, conv_w, rope_c, rope_su, rope_sd)


FILT_TL = 512


def _filter_kernel(z_ref, w0_ref, b0_ref, wi_ref, bi_ref, wo_ref, fr_ref, dl_ref, h_ref):
    hi = lax.Precision.HIGHEST
    half = FILT_TL // 2
    z = z_ref[...]
    z2 = jnp.concatenate([z[:half], z[half:]], axis=1)
    fr = fr_ref[...]
    h = jnp.sin(fr * (jnp.dot(z2, w0_ref[...], precision=hi, preferred_element_type=F32) + b0_ref[...]))
    for i in range(wi_ref.shape[0]):
        h = jnp.sin(fr * (jnp.dot(h, wi_ref[i], precision=hi, preferred_element_type=F32) + bi_ref[i]))
    def split(v):
        top = v.astype(BF16)
        return top, (v - top.astype(F32)).astype(BF16)
    h_hi, h_lo = split(h)
    w_hi, w_lo = split(wo_ref[...])
    out = (jnp.dot(h_hi, w_hi, preferred_element_type=F32) + jnp.dot(h_hi, w_lo, preferred_element_type=F32)
           + jnp.dot(h_lo, w_hi, preferred_element_type=F32))
    for r in range(2):
        rows = slice(r * half, (r + 1) * half)
        t = z[rows, 0:1]
        h_ref[rows, :] = out[:, r * N_FILT_CH:(r + 1) * N_FILT_CH] * jnp.exp(-t * dl_ref[...])


def _block_diag2(w):
    zeros = jnp.zeros_like(w)
    return jnp.concatenate([jnp.concatenate([w, zeros], axis=-1), jnp.concatenate([zeros, w], axis=-1)], axis=-2)


def _filter_call(z, w0, b0, wi, bi, wo, fr, deltas):
    n_inner = wi.shape[0]
    twice = lambda v: jnp.concatenate([v, v], axis=-1)
    w0, wi, wo = _block_diag2(w0), _block_diag2(wi), _block_diag2(wo)
    b0, bi, fr = twice(b0), twice(bi), twice(fr)
    wide = 2 * FILTER_ORDER
    return pl.pallas_call(
        _filter_kernel,
        grid=(SEQ // FILT_TL,),
        in_specs=[pl.BlockSpec((FILT_TL, LANES), lambda i: (i, 0)),
                  _const_spec((2 * LANES, wide)), _const_spec((1, wide)),
                  _const_spec((n_inner, wide, wide)), _const_spec((n_inner, 1, wide)),
                  _const_spec((wide, 2 * N_FILT_CH)), _const_spec((1, wide)),
                  _const_spec((1, N_FILT_CH))],
        out_specs=pl.BlockSpec((FILT_TL, N_FILT_CH), lambda i: (i, 0)),
        out_shape=jax.ShapeDtypeStruct((SEQ, N_FILT_CH), F32),
        compiler_params=_cparams(("parallel",)),
        name="filt",
    )(z, w0, b0, wi, bi, wo, fr, deltas)


STAGE_UNROLL = 64
KB = 2
MID_UNROLL = 64


def _stage_f1(z_ref, a_ref, f1_ref, real_only):
    def body(n2, carry):
        zr = z_ref[0, pl.ds(n2, H1, stride=Z_STRIDE), :]
        if real_only:
            res = jnp.dot(f1_ref[n2][:, :H1], zr.astype(BF16), preferred_element_type=F32)
        else:
            zi = z_ref[1, pl.ds(n2, H1, stride=Z_STRIDE), :]
            st = jnp.concatenate([zr, zi], axis=0).astype(BF16)
            res = jnp.dot(f1_ref[n2], st, preferred_element_type=F32)
        base = n2 * A_STRIDE
        a_ref[0, pl.ds(base, N1), :] = res[:N1]
        a_ref[1, pl.ds(base, N1), :] = res[N1:]
        return carry
    lax.fori_loop(0, N2, body, 0, unroll=STAGE_UNROLL)


def _k1_rows(k1):
    return pl.ds(k1 * K_STRIDE, N2)


def _dft_blocks(mat_ref, vr, vi, interleaved):
    st = jnp.concatenate([vr, vi], axis=0).astype(BF16)
    x = jnp.dot(mat_ref[...], st, preferred_element_type=F32)
    if not interleaved:
        return x[:N2], x[N2:]
    x4 = x.reshape(N2 // SUBLANES, 2, SUBLANES, x.shape[-1])
    return x4[:, 0].reshape(N2, x.shape[-1]), x4[:, 1].reshape(N2, x.shape[-1])


def _load_spectrum_blocks(a_ref, f2_ref, k1b):
    ar = jnp.concatenate([a_ref[0, pl.ds(k1b + i, N2, stride=A_STRIDE), :] for i in range(KB)], axis=1)
    ai = jnp.concatenate([a_ref[1, pl.ds(k1b + i, N2, stride=A_STRIDE), :] for i in range(KB)], axis=1)
    return _dft_blocks(f2_ref, ar, ai, True)


def _spectrum_rows(k1b, i):
    return pl.ds(pl.multiple_of((k1b + i) * N2, N2), N2)


def _kf_kernel(hf_ref, hb_ref, f1_ref, f2_ref, kr_ref, ki_ref, z_ref, ab_ref, acc_ref):
    inv_n = 1.0 / N_FFT

    def load_time(src_ref, drop_first):
        def body(j, carry):
            rows = pl.multiple_of(j * N2, N2)
            v = src_ref[pl.ds(rows, N2), :]
            if drop_first:
                ridx = lax.broadcasted_iota(jnp.int32, (N2, CB), 0) + rows
                v = jnp.where(ridx == 0, 0.0, v)
            z_ref[0, pl.ds(j * Z_STRIDE, N2), :] = v
            return carry
        lax.fori_loop(0, H1, body, 0, unroll=4)

    def spectrum(accumulate):
        def body(it, carry):
            k1b = it * KB
            xr, xi = _load_spectrum_blocks(ab_ref, f2_ref, k1b)
            for i in range(KB):
                rows = _spectrum_rows(k1b, i)
                lanes = slice(i * CB, (i + 1) * CB)
                if accumulate:
                    kr_ref[rows, :] = (acc_ref[0, rows, :] + xr[:, lanes] * inv_n).astype(BF16)
                    ki_ref[rows, :] = (acc_ref[1, rows, :] - xi[:, lanes] * inv_n).astype(BF16)
                else:
                    acc_ref[0, rows, :] = xr[:, lanes] * inv_n
                    acc_ref[1, rows, :] = xi[:, lanes] * inv_n
            return carry
        lax.fori_loop(0, N1 // KB, body, 0, unroll=MID_UNROLL)

    load_time(hf_ref, False)
    _stage_f1(z_ref, ab_ref, f1_ref, True)
    spectrum(False)
    load_time(hb_ref, True)
    _stage_f1(z_ref, ab_ref, f1_ref, True)
    spectrum(True)


def _kf_call(h, f1, f2):
    blocks_per_order = N_CB
    fwd = pl.BlockSpec((SEQ, CB), lambda o, c: (0, o * blocks_per_order + c))
    bwd = pl.BlockSpec((SEQ, CB), lambda o, c: (0, (HYENA_ORDER + o) * blocks_per_order + c))
    out = pl.BlockSpec((None, N_FFT, CB), lambda o, c: (o, 0, c))
    return pl.pallas_call(
        _kf_kernel,
        grid=(HYENA_ORDER, N_CB),
        in_specs=[fwd, bwd, _const_spec((N2, 2 * N1, 2 * H1)), _const_spec((2 * N2, 2 * N2))],
        out_specs=[out, out],
        out_shape=[jax.ShapeDtypeStruct((HYENA_ORDER, N_FFT, D_HYENA), BF16)] * 2,
        scratch_shapes=[pltpu.VMEM((2, H1 * Z_STRIDE, CB), F32), pltpu.VMEM((2, N2 * A_STRIDE, CB), F32),
                        pltpu.VMEM((2, N_FFT, CB), F32)],
        compiler_params=_cparams(("parallel", "parallel")),
        name="kf",
    )(h, h, f1, f2)


TIME_ROWS = 256
GROUPS = TIME_ROWS // N2


def _group_rows(j, i):
    return pl.ds((j * GROUPS + i) * Z_STRIDE, N2)


def _hyena_kernel(a_ref, g_ref, kr_ref, ki_ref, f1_ref, f2_ref, i1_ref, i2_ref, w_ref, o_ref,
                  z_ref, s1_ref, s2_ref):
    def fill(j, carry):
        rows = pl.ds(pl.multiple_of(j * TIME_ROWS, TIME_ROWS), TIME_ROWS)
        for r in range(2):
            val = a_ref[r, rows, :].astype(F32)
            for i in range(GROUPS):
                z_ref[r, _group_rows(j, i), :] = val[i * N2:(i + 1) * N2]
        return carry
    lax.fori_loop(0, SEQ // TIME_ROWS, fill, 0, unroll=2)

    _stage_f1(z_ref, s1_ref, f1_ref, False)

    def store_blocks(k1b, vr, vi):
        for i in range(KB):
            lanes = slice(i * CB, (i + 1) * CB)
            s2_ref[0, _k1_rows(k1b + i), :] = vr[:, lanes]
            s2_ref[1, _k1_rows(k1b + i), :] = vi[:, lanes]

    def mid_fwd(it, carry):
        k1b = it * KB
        xr, xi = _load_spectrum_blocks(s1_ref, f2_ref, k1b)
        kr = jnp.concatenate([kr_ref[_spectrum_rows(k1b, i), :] for i in range(KB)], axis=1).astype(F32)
        ki = jnp.concatenate([ki_ref[_spectrum_rows(k1b, i), :] for i in range(KB)], axis=1).astype(F32)
        store_blocks(k1b, xr * kr - xi * ki, xr * ki + xi * kr)
        return carry
    lax.fori_loop(0, N1 // KB, mid_fwd, 0, unroll=MID_UNROLL)

    def mid_inv(it, carry):
        k1b = it * KB
        yr = jnp.concatenate([s2_ref[0, _k1_rows(k1b + i), :] for i in range(KB)], axis=1)
        yi = jnp.concatenate([s2_ref[1, _k1_rows(k1b + i), :] for i in range(KB)], axis=1)
        br, bi = _dft_blocks(i1_ref, yr, yi, False)
        store_blocks(k1b, br, bi)
        return carry
    lax.fori_loop(0, N1 // KB, mid_inv, 0, unroll=MID_UNROLL)

    y_ref = s1_ref

    def last(n2, carry):
        st = jnp.concatenate([s2_ref[0, pl.ds(n2, N1, stride=K_STRIDE), :],
                              s2_ref[1, pl.ds(n2, N1, stride=K_STRIDE), :]], axis=1).astype(BF16)
        p = jnp.dot(i2_ref[n2], st, preferred_element_type=F32)
        y_ref[0, pl.ds(n2, H1, stride=Z_STRIDE), :] = p[:H1, :CB] + p[H1:, CB:]
        y_ref[1, pl.ds(n2, H1, stride=Z_STRIDE), :] = p[:H1, CB:] - p[H1:, :CB]
        return carry
    lax.fori_loop(0, N2, last, 0, unroll=STAGE_UNROLL)

    def finish(j, carry):
        bias = w_ref[0:1]
        rows = pl.ds(pl.multiple_of(j * TIME_ROWS, TIME_ROWS), TIME_ROWS)
        for r in range(2):
            y = jnp.concatenate([y_ref[r, _group_rows(j, i), :] for i in range(GROUPS)], axis=0)
            z = jnp.concatenate([z_ref[r, _group_rows(j, i), :] for i in range(GROUPS)], axis=0)
            o_ref[r, rows, :] = (g_ref[r, rows, :].astype(F32) * (y + bias * z)).astype(BF16)
        return carry
    lax.fori_loop(0, SEQ // TIME_ROWS, finish, 0, unroll=2)


def _hyena_call(a, a_slot, g, g_slot, kr, ki, order, tables, bias):
    b = a.shape[0]
    f1, f2, i1, i2 = tables
    io = lambda slot: pl.BlockSpec((2, None, SEQ, CB), lambda c, p: (p, slot + c, 0, 0))
    kspec = pl.BlockSpec((None, N_FFT, CB), lambda c, p: (order, 0, c))
    assert N2 * A_STRIDE >= H1 * Z_STRIDE
    return pl.pallas_call(
        _hyena_kernel,
        grid=(N_CB, b // 2),
        in_specs=[io(a_slot), io(g_slot), kspec, kspec,
                  _const_spec((N2, 2 * N1, 2 * H1)), _const_spec((2 * N2, 2 * N2)),
                  _const_spec((2 * N2, 2 * N2)), _const_spec((N2, 2 * H1, N1)),
                  pl.BlockSpec((None, 8, CB), lambda c, p: (c, 0, 0))],
        out_specs=io(0),
        out_shape=jax.ShapeDtypeStruct((b, N_CB, SEQ, CB), BF16),
        scratch_shapes=[pltpu.VMEM((2, H1 * Z_STRIDE, CB), F32), pltpu.VMEM((2, N2 * A_STRIDE, CB), F32),
                        pltpu.VMEM((2, N1 * K_STRIDE, CB), F32)],
        compiler_params=_cparams(("parallel", "parallel")),
        name=f"hyena{order}",
    )(a, g, kr, ki, f1, f2, i1, i2, bias)


ATT_TQ = 128
ATT_KW = ATT_TQ + 2 * WINDOW


ATT_QB = 8


def _attn_kernel(sink_ref, q_ref, k_ref, v_ref, o_ref):
    lo_q = lax.broadcasted_iota(jnp.int32, (ATT_TQ, LANES), 1) < HEAD_DIM
    for qb in range(ATT_QB):
        i = pl.program_id(1) * ATT_QB + qb
        start = pl.multiple_of(jnp.clip(i * ATT_TQ - WINDOW, 0, SEQ - ATT_KW), LANES)
        win = pl.ds(start, ATT_KW)
        qrows = slice(qb * ATT_TQ, (qb + 1) * ATT_TQ)
        qpos = i * ATT_TQ + lax.broadcasted_iota(jnp.int32, (ATT_TQ, ATT_KW), 0)
        kpos = start + lax.broadcasted_iota(jnp.int32, (ATT_TQ, ATT_KW), 1)
        bias = jnp.where(jnp.abs(qpos - kpos) <= WINDOW, 0.0, NEG_INF).astype(BF16)
        pairs = [(h, j) for h in range(N_KV_HEADS) for j in range(2)]
        scores = []
        for h, j in pairs:
            kbd = jnp.concatenate([k_ref[0, 2 * h, win, :], k_ref[0, 2 * h + 1, win, :]], axis=0)
            q2 = q_ref[0, qrows, LANES * (2 * h + j):LANES * (2 * h + j + 1)]
            scores.append(lax.dot_general(q2, kbd, (((1,), (1,)), ((), ())), preferred_element_type=F32))
        probs, scales = [], []
        for (h, j), s in zip(pairs, scores):
            ps, inv = [], []
            for e in range(2):
                sk = sink_ref[4 * h + 2 * j + e] * LOG2E
                se = s[:, e * ATT_KW:(e + 1) * ATT_KW].astype(BF16) + bias
                m = jnp.maximum(jnp.max(se, axis=-1, keepdims=True).astype(F32), sk)
                p = jnp.exp2(se - m.astype(BF16))
                part = functools.reduce(lambda a, b: a + b,
                                        [p[:, c * LANES:(c + 1) * LANES] for c in range(ATT_KW // LANES)])
                inv.append(1.0 / (jnp.sum(part.astype(F32), axis=-1, keepdims=True) + jnp.exp2(sk - m)))
                ps.append(p)
            probs.append(jnp.concatenate(ps, axis=1))
            scales.append(jnp.where(lo_q, inv[0], inv[1]))
        for (h, j), p2, sc in zip(pairs, probs, scales):
            vbd = jnp.concatenate([v_ref[0, 2 * h, win, :], v_ref[0, 2 * h + 1, win, :]], axis=0)
            o2 = jnp.dot(p2, vbd, preferred_element_type=F32)
            c0 = LANES * (2 * h + j)
            o_ref[0, qrows, c0:c0 + LANES] = (o2 * sc).astype(BF16)


def _attn_call(sink, q, k, v):
    b = q.shape[0]
    full = pl.BlockSpec((1, 2 * N_KV_HEADS, SEQ, LANES), lambda bi, i: (bi, 0, 0, 0))
    tile = pl.BlockSpec((1, ATT_TQ * ATT_QB, D_ATTN), lambda bi, i: (bi, i, 0))
    return pl.pallas_call(
        _attn_kernel,
        grid=(b, SEQ // (ATT_TQ * ATT_QB)),
        in_specs=[pl.BlockSpec(memory_space=pltpu.SMEM), tile, full, full],
        out_specs=tile,
        out_shape=jax.ShapeDtypeStruct((b, SEQ, D_ATTN), BF16),
        compiler_params=_cparams(("parallel", "parallel")),
        name="attn",
    )(sink, q, k, v)


MIX_TM = 512
MXU_COLS = 256
FF_SPLITS = (0, 6 * MXU_COLS, D_FF)


def _rms(x, g):
    return x * lax.rsqrt(jnp.mean(x * x, axis=-1, keepdims=True) + EPS) * g


def _mix_kernel(x_ref, yh_ref, ya_ref, gate_ref, wuh_ref, wua_ref, wo_ref, g2_ref, wg_ref, wu_ref, wd_ref,
                gf_ref, o_ref):
    yh = jnp.concatenate([yh_ref[0, c] for c in range(N_CB)], axis=1)
    up_h = jnp.dot(yh, wuh_ref[...], preferred_element_type=F32)
    up_a = jnp.dot(ya_ref[0], wua_ref[...], preferred_element_type=F32)
    merged = (gate_ref[0, :, :D_MODEL].astype(F32) * up_h + gate_ref[0, :, D_MODEL:].astype(F32) * up_a)
    x1 = x_ref[0] + jnp.dot(merged.astype(BF16), wo_ref[...], preferred_element_type=F32)
    hn = _rms(x1, g2_ref[...]).astype(BF16)
    acc = x1
    for c in range(len(FF_SPLITS) - 1):
        cols = slice(FF_SPLITS[c], FF_SPLITS[c + 1])
        gate = jnp.dot(hn, wg_ref[:, cols], preferred_element_type=F32)
        up = jnp.dot(hn, wu_ref[:, cols], preferred_element_type=F32)
        act = (jax.nn.silu(gate) * up).astype(BF16)
        acc = acc + jnp.dot(act, wd_ref[cols, :], preferred_element_type=F32)
    o_ref[0] = _rms(acc, gf_ref[...])


def _mix_call(x, yh, ya, gates, wuh, wua, wo, g2, wg, wu, wd, gf):
    b = x.shape[0]
    tile = lambda w: pl.BlockSpec((1, MIX_TM, w), lambda bi, i: (bi, i, 0))
    return pl.pallas_call(
        _mix_kernel,
        grid=(b, SEQ // MIX_TM),
        in_specs=[tile(D_MODEL), pl.BlockSpec((1, N_CB, MIX_TM, CB), lambda bi, i: (bi, 0, i, 0)),
                  tile(D_ATTN), tile(2 * D_MODEL),
                  _const_spec((D_HYENA, D_MODEL)), _const_spec((D_ATTN, D_MODEL)),
                  _const_spec((D_MODEL, D_MODEL)), _const_spec((1, D_MODEL)),
                  _const_spec((D_MODEL, D_FF)), _const_spec((D_MODEL, D_FF)), _const_spec((D_FF, D_MODEL)),
                  _const_spec((1, D_MODEL))],
        out_specs=tile(D_MODEL),
        out_shape=jax.ShapeDtypeStruct((b, SEQ, D_MODEL), F32),
        compiler_params=_cparams(("parallel", "parallel")),
        name="mix",
    )(x, yh, ya, gates, wuh, wua, wo, g2, wg, wu, wd, gf)


def _hyena_bias_blocks(bias):
    rows = jnp.concatenate([bias[None, :], jnp.zeros((7, D_HYENA), F32)], axis=0)
    return rows.reshape(8, N_CB, CB).transpose(1, 0, 2)


def _layer(x, tables, rope, kf, norm1_g, w_in, short_w, short_b, hyena_bias, sink_logit, w_up_hyena,
           w_up_attn, w_out, norm2_g, w_ff_gate, w_ff_up, w_ff_down, final_g):
    kr, ki = kf
    conv_w = jnp.concatenate([short_w, short_b[None, :], jnp.zeros((4, D_UH), F32)], axis=0)
    uh, q, k, v, gates = _proj_call(x, norm1_g[None, :], w_in.astype(BF16), conv_w, *rope)
    blocks = D_HYENA // LANES
    z1 = _hyena_call(uh, 0, uh, blocks, kr, ki, 0, tables, _hyena_bias_blocks(hyena_bias[0]))
    yh = _hyena_call(z1, 0, uh, 2 * blocks, kr, ki, 1, tables, _hyena_bias_blocks(hyena_bias[1]))
    ya = _attn_call(sink_logit, q, k, v)
    return _mix_call(x, yh, ya, gates, w_up_hyena.astype(BF16), w_up_attn.astype(BF16), w_out.astype(BF16),
                     norm2_g[None, :], w_ff_gate.astype(BF16), w_ff_up.astype(BF16), w_ff_down.astype(BF16),
                     final_g[None, :])


def kernel(x_prompt, x_sample, norm1_g, w_in, short_w, short_b, filt_w0, filt_b0, filt_w_inner, filt_b_inner,
           filt_w_out, filt_freq, hyena_bias, sink_logit, w_up_hyena, w_up_attn, w_out, norm2_g, w_ff_gate,
           w_ff_up, w_ff_down, final_g):
    tables = _dft_tables()
    rope = _rope_tables()
    z_emb, deltas = _filter_embedding()
    w0 = jnp.pad(filt_w0[0], ((0, LANES - FILTER_EMB), (0, 0)))
    h = _filter_call(z_emb, w0, filt_b0[0][None, :], filt_w_inner[0], filt_b_inner[0][:, None, :],
                     filt_w_out[0], filt_freq[0][None, :], deltas)
    kf = _kf_call(h, tables[0], tables[1])
    args = (norm1_g[0], w_in[0], short_w[0], short_b[0], hyena_bias[0], sink_logit[0], w_up_hyena[0],
            w_up_attn[0], w_out[0], norm2_g[0], w_ff_gate[0], w_ff_up[0], w_ff_down[0], final_g)
    y_prompt = _layer(x_prompt, tables, rope, kf, *args)
    y_sample = _layer(x_sample, tables, rope, kf, *args)
    return (y_prompt, y_sample)
```

```python
import functools
import math

import jax
import jax.numpy as jnp
from jax import lax
from jax.experimental import pallas as pl
from jax.experimental.pallas import tpu as pltpu

F32 = jnp.float32
BF16 = jnp.bfloat16

D_MODEL = 1024
SEQ = 4096
D_HYENA = 512
HYENA_ORDER = 2
SHORT_CONV = 3
FILTER_EMB = 33
FILTER_ORDER = 64
N_FILT_CH = 2 * HYENA_ORDER * D_HYENA
FAST_DECAY_PCT = 0.3
SLOW_DECAY_PCT = 1.5
DECAY_TARGET = 1e-2
N_Q_HEADS = 8
N_KV_HEADS = 2
HEAD_DIM = 64
D_ATTN = N_Q_HEADS * HEAD_DIM
D_KV = N_KV_HEADS * HEAD_DIM
WINDOW = 128
ROPE_THETA = 500000.0
ROT_DIM = HEAD_DIM // 4
D_UH = (HYENA_ORDER + 1) * D_HYENA
D_IN = D_UH + D_ATTN + 2 * D_KV + 2 * D_MODEL
D_FF = 2816
EPS = 1e-6
NEG_INF = -1e30

LANES = 128
SUBLANES = 8
N_FFT = 2 * SEQ
N1 = 128
N2 = N_FFT // N1
H1 = N1 // 2
Z_STRIDE = N2 + 1
A_STRIDE = N1 + 1
K_STRIDE = N2 + 1
CB = LANES
N_CB = D_HYENA // CB
VMEM_LIMIT = 56 * 1024 * 1024


def _cparams(sem):
    return pltpu.CompilerParams(dimension_semantics=sem, vmem_limit_bytes=VMEM_LIMIT)


def _const_spec(shape):
    nd = len(shape)
    return pl.BlockSpec(shape, lambda *_: (0,) * nd, pipeline_mode=pl.Buffered(1))


def _dft_tables():
    two_pi = 2.0 * math.pi
    n1 = jnp.arange(H1, dtype=jnp.int32)
    n2 = jnp.arange(N2, dtype=jnp.int32)
    k1 = jnp.arange(N1, dtype=jnp.int32)
    ang_a = ((n1[None, :] * k1[:, None]) % N1).astype(F32) * (two_pi / N1)
    ang_t = (n2[:, None] * k1[None, :]).astype(F32) * (two_pi / N_FFT)
    ar, ai = jnp.cos(ang_a)[None], -jnp.sin(ang_a)[None]
    tr, ti = jnp.cos(ang_t)[:, :, None], -jnp.sin(ang_t)[:, :, None]
    er, ei = ar * tr - ai * ti, ar * ti + ai * tr
    f1 = jnp.concatenate([jnp.concatenate([er, -ei], axis=2),
                          jnp.concatenate([ei, er], axis=2)], axis=1)
    ert, eit = jnp.swapaxes(er, 1, 2), jnp.swapaxes(ei, 1, 2)
    i2 = jnp.concatenate([ert, eit], axis=1)
    mg = (n2[:, None] * n2[None, :]) % N2
    angg = mg.astype(F32) * (two_pi / N2)
    gr, gi = jnp.cos(angg), -jnp.sin(angg)
    f2 = jnp.concatenate([jnp.concatenate([gr, -gi], axis=1),
                          jnp.concatenate([gi, gr], axis=1)], axis=0)
    f2 = f2.reshape(2, N2 // SUBLANES, SUBLANES, 2 * N2).transpose(1, 0, 2, 3).reshape(2 * N2, 2 * N2)
    i1 = jnp.concatenate([jnp.concatenate([gr, gi], axis=1),
                          jnp.concatenate([-gi, gr], axis=1)], axis=0)
    return f1.astype(BF16), f2.astype(BF16), i1.astype(BF16), i2.astype(BF16)


def _rope_tables():
    half = ROT_DIM // 2
    pos = jnp.arange(SEQ, dtype=F32)
    inv = 1.0 / (ROPE_THETA ** (jnp.arange(0, ROT_DIM, 2, dtype=F32) / ROT_DIM))
    ang = pos[:, None] * inv[None, :]
    cos, sin = jnp.cos(ang), jnp.sin(ang)
    ones = jnp.ones((SEQ, HEAD_DIM - ROT_DIM), F32)
    zeros = jnp.zeros((SEQ, HEAD_DIM - ROT_DIM), F32)
    zh = jnp.zeros((SEQ, half), F32)
    c = jnp.concatenate([cos, cos, ones], axis=1)
    s_up = jnp.concatenate([-sin, zh, zeros], axis=1)
    s_dn = jnp.concatenate([zh, sin, zeros], axis=1)
    rep = LANES // HEAD_DIM
    return jnp.tile(c, (1, rep)), jnp.tile(s_up, (1, rep)), jnp.tile(s_dn, (1, rep))


def _filter_embedding():
    bands = (FILTER_EMB - 1) // 2
    t = jnp.linspace(0.0, 1.0, SEQ, dtype=F32)[:, None]
    w = 2.0 * math.pi * jnp.arange(SEQ, dtype=F32)[:, None] / SEQ
    f = jnp.linspace(1e-4, bands - 1, bands, dtype=F32)[None, :]
    z = jnp.concatenate([t, jnp.cos(f * w), jnp.sin(f * w)], axis=-1)
    z = jnp.pad(z, ((0, 0), (0, LANES - FILTER_EMB)))
    min_decay = math.log(DECAY_TARGET) / SLOW_DECAY_PCT
    max_decay = math.log(DECAY_TARGET) / FAST_DECAY_PCT
    deltas = jnp.tile(jnp.linspace(min_decay, max_decay, D_HYENA, dtype=F32), 2 * HYENA_ORDER)
    return z, jnp.abs(deltas)[None, :]


PROJ_TM = 1024
PROJ_HALO = 16
LOG2E = math.log2(math.e)
Q_SCALE = HEAD_DIM ** -0.5 * LOG2E


def _proj_kernel(x_ref, xp_ref, xn_ref, g_ref, w_ref, cw_ref, c_ref, su_ref, sd_ref, *refs):
    uh_ref, q_ref, k_ref, v_ref, gate_ref = refs[-5:]

    def norm(xf):
        return xf * lax.rsqrt(jnp.mean(xf * xf, axis=-1, keepdims=True) + EPS) * g_ref[...]

    i = pl.program_id(1)
    hn = norm(x_ref[0]).astype(BF16)
    h_prev = (norm(xp_ref[0]) * (i > 0).astype(F32)).astype(BF16)
    h_next = (norm(xn_ref[0]) * (i < pl.num_programs(1) - 1).astype(F32)).astype(BF16)
    hn_ext = jnp.concatenate([h_prev, hn, h_next], axis=0)

    def proj(c0, width):
        return jnp.dot(hn, w_ref[:, c0:c0 + width], preferred_element_type=F32)

    rows = PROJ_TM + 2 * PROJ_HALO
    for j in range(D_UH // 512):
        cols = slice(j * 512, (j + 1) * 512)
        u = jnp.dot(hn_ext, w_ref[:, cols], preferred_element_type=F32)
        prev = pltpu.roll(u, 1, axis=0)[PROJ_HALO:PROJ_HALO + PROJ_TM]
        nxt = pltpu.roll(u, rows - 1, axis=0)[PROJ_HALO:PROJ_HALO + PROJ_TM]
        cur = u[PROJ_HALO:PROJ_HALO + PROJ_TM]
        uc = (prev * cw_ref[0:1, cols] + cur * cw_ref[1:2, cols] + nxt * cw_ref[2:3, cols] + cw_ref[3:4, cols])
        for c in range(512 // LANES):
            uh_ref[0, j * (512 // LANES) + c] = uc[:, c * LANES:(c + 1) * LANES].astype(BF16)

    def rope(xc):
        return (xc * c_ref[...] + pltpu.roll(xc, LANES - ROT_DIM // 2, axis=1) * su_ref[...]
                + pltpu.roll(xc, ROT_DIM // 2, axis=1) * sd_ref[...])

    qkv = proj(D_UH, D_ATTN + 2 * D_KV)
    for j in range(D_ATTN // LANES):
        qc = qkv[:, j * LANES:(j + 1) * LANES]
        q_ref[0, :, j * LANES:(j + 1) * LANES] = (rope(qc) * Q_SCALE).astype(BF16)
    lo = lax.broadcasted_iota(jnp.int32, (PROJ_TM, LANES), 1) < HEAD_DIM
    for val, ref in ((rope(qkv[:, D_ATTN:D_ATTN + D_KV]), k_ref), (qkv[:, D_ATTN + D_KV:], v_ref)):
        rolled = pltpu.roll(val, HEAD_DIM, axis=1)
        ref[0, 0] = jnp.where(lo, val, 0.0).astype(BF16)
        ref[0, 1] = jnp.where(lo, 0.0, rolled).astype(BF16)
        ref[0, 2] = jnp.where(lo, rolled, 0.0).astype(BF16)
        ref[0, 3] = jnp.where(lo, 0.0, val).astype(BF16)
    g0 = D_UH + D_ATTN + 2 * D_KV
    for j in range(2 * D_MODEL // 512):
        gate_ref[0, :, j * 512:(j + 1) * 512] = jax.nn.sigmoid(proj(g0 + j * 512, 512)).astype(BF16)


def _proj_call(x, batch0, total, stacked, g1, w_in, conv_w, rope_c, rope_su, rope_sd):
    b = x.shape[0]
    nt = SEQ // PROJ_TM
    per_tile = PROJ_TM // PROJ_HALO
    tile = lambda w: pl.BlockSpec((1, PROJ_TM, w), lambda bi, i: (bi, i, 0))
    halo_prev = pl.BlockSpec((1, PROJ_HALO, D_MODEL), lambda bi, i: (bi, jnp.maximum(i * per_tile - 1, 0), 0))
    halo_next = pl.BlockSpec((1, PROJ_HALO, D_MODEL),
                             lambda bi, i: (bi, jnp.minimum((i + 1) * per_tile, SEQ // PROJ_HALO - 1), 0))
    rope_spec = pl.BlockSpec((PROJ_TM, LANES), lambda bi, i: (i, 0))
    out_tile = lambda w: pl.BlockSpec((1, PROJ_TM, w), lambda bi, i: (batch0 + bi, i, 0))
    out_slabs = lambda n: pl.BlockSpec((1, n, PROJ_TM, LANES), lambda bi, i: (batch0 + bi, 0, i, 0))
    in_specs = [tile(D_MODEL), halo_prev, halo_next, _const_spec((1, D_MODEL)), _const_spec((D_MODEL, D_IN)),
                _const_spec((8, D_UH)), rope_spec, rope_spec, rope_spec]
    operands = [x, x, x, g1, w_in, conv_w, rope_c, rope_su, rope_sd]
    aliases = {}
    if stacked is not None:
        aliases = {len(operands) + n: n for n in range(len(stacked))}
        in_specs += [pl.BlockSpec(memory_space=pl.ANY)] * len(stacked)
        operands += list(stacked)
    return pl.pallas_call(
        _proj_kernel,
        grid=(b, nt),
        in_specs=in_specs,
        out_specs=[out_slabs(D_UH // LANES), out_tile(D_ATTN), out_slabs(2 * N_KV_HEADS),
                   out_slabs(2 * N_KV_HEADS), out_tile(2 * D_MODEL)],
        out_shape=[jax.ShapeDtypeStruct((total, D_UH // LANES, SEQ, LANES), BF16),
                   jax.ShapeDtypeStruct((total, SEQ, D_ATTN), BF16),
                   jax.ShapeDtypeStruct((total, 2 * N_KV_HEADS, SEQ, LANES), BF16),
                   jax.ShapeDtypeStruct((total, 2 * N_KV_HEADS, SEQ, LANES), BF16),
                   jax.ShapeDtypeStruct((total, SEQ, 2 * D_MODEL), BF16)],
        input_output_aliases=aliases,
        compiler_params=_cparams(("parallel", "parallel")),
        name="proj",
    )(*operands)


FILT_TL = 512


def _filter_kernel(z_ref, w0_ref, b0_ref, wi_ref, bi_ref, wo_ref, fr_ref, dl_ref, h_ref):
    hi = lax.Precision.HIGHEST
    half = FILT_TL // 2
    z = z_ref[...]
    z2 = jnp.concatenate([z[:half], z[half:]], axis=1)
    fr = fr_ref[...]
    h = jnp.sin(fr * (jnp.dot(z2, w0_ref[...], precision=hi, preferred_element_type=F32) + b0_ref[...]))
    for i in range(wi_ref.shape[0]):
        h = jnp.sin(fr * (jnp.dot(h, wi_ref[i], precision=hi, preferred_element_type=F32) + bi_ref[i]))
    def split(v):
        top = v.astype(BF16)
        return top, (v - top.astype(F32)).astype(BF16)
    h_hi, h_lo = split(h)
    w_hi, w_lo = split(wo_ref[...])
    out = (jnp.dot(h_hi, w_hi, preferred_element_type=F32) + jnp.dot(h_hi, w_lo, preferred_element_type=F32)
           + jnp.dot(h_lo, w_hi, preferred_element_type=F32))
    for r in range(2):
        rows = slice(r * half, (r + 1) * half)
        t = z[rows, 0:1]
        h_ref[rows, :] = out[:, r * N_FILT_CH:(r + 1) * N_FILT_CH] * jnp.exp(-t * dl_ref[...])


def _block_diag2(w):
    zeros = jnp.zeros_like(w)
    return jnp.concatenate([jnp.concatenate([w, zeros], axis=-1), jnp.concatenate([zeros, w], axis=-1)], axis=-2)


def _filter_call(z, w0, b0, wi, bi, wo, fr, deltas):
    n_inner = wi.shape[0]
    twice = lambda v: jnp.concatenate([v, v], axis=-1)
    w0, wi, wo = _block_diag2(w0), _block_diag2(wi), _block_diag2(wo)
    b0, bi, fr = twice(b0), twice(bi), twice(fr)
    wide = 2 * FILTER_ORDER
    return pl.pallas_call(
        _filter_kernel,
        grid=(SEQ // FILT_TL,),
        in_specs=[pl.BlockSpec((FILT_TL, LANES), lambda i: (i, 0)),
                  _const_spec((2 * LANES, wide)), _const_spec((1, wide)),
                  _const_spec((n_inner, wide, wide)), _const_spec((n_inner, 1, wide)),
                  _const_spec((wide, 2 * N_FILT_CH)), _const_spec((1, wide)),
                  _const_spec((1, N_FILT_CH))],
        out_specs=pl.BlockSpec((FILT_TL, N_FILT_CH), lambda i: (i, 0)),
        out_shape=jax.ShapeDtypeStruct((SEQ, N_FILT_CH), F32),
        compiler_params=_cparams(("parallel",)),
        name="filt",
    )(z, w0, b0, wi, bi, wo, fr, deltas)


STAGE_UNROLL = 64
KB = 2
MID_UNROLL = 64


def _stage_f1(z_ref, a_ref, f1_ref, real_only):
    def body(n2, carry):
        zr = z_ref[0, pl.ds(n2, H1, stride=Z_STRIDE), :]
        if real_only:
            res = jnp.dot(f1_ref[n2][:, :H1], zr.astype(BF16), preferred_element_type=F32)
        else:
            zi = z_ref[1, pl.ds(n2, H1, stride=Z_STRIDE), :]
            st = jnp.concatenate([zr, zi], axis=0).astype(BF16)
            res = jnp.dot(f1_ref[n2], st, preferred_element_type=F32)
        base = n2 * A_STRIDE
        a_ref[0, pl.ds(base, N1), :] = res[:N1]
        a_ref[1, pl.ds(base, N1), :] = res[N1:]
        return carry
    lax.fori_loop(0, N2, body, 0, unroll=STAGE_UNROLL)


def _k1_rows(k1):
    return pl.ds(k1 * K_STRIDE, N2)


def _dft_blocks(mat_ref, vr, vi, interleaved):
    st = jnp.concatenate([vr, vi], axis=0).astype(BF16)
    x = jnp.dot(mat_ref[...], st, preferred_element_type=F32)
    if not interleaved:
        return x[:N2], x[N2:]
    x4 = x.reshape(N2 // SUBLANES, 2, SUBLANES, x.shape[-1])
    return x4[:, 0].reshape(N2, x.shape[-1]), x4[:, 1].reshape(N2, x.shape[-1])


def _load_spectrum_blocks(a_ref, f2_ref, k1b):
    ar = jnp.concatenate([a_ref[0, pl.ds(k1b + i, N2, stride=A_STRIDE), :] for i in range(KB)], axis=1)
    ai = jnp.concatenate([a_ref[1, pl.ds(k1b + i, N2, stride=A_STRIDE), :] for i in range(KB)], axis=1)
    return _dft_blocks(f2_ref, ar, ai, True)


def _spectrum_rows(k1b, i):
    return pl.ds(pl.multiple_of((k1b + i) * N2, N2), N2)


def _kf_kernel(hf_ref, hb_ref, b_ref, f1_ref, f2_ref, kr_ref, ki_ref, z_ref, ab_ref, acc_ref):
    inv_n = 1.0 / N_FFT

    def load_time(src_ref, backward):
        def body(j, carry):
            rows = pl.multiple_of(j * N2, N2)
            v = src_ref[pl.ds(rows, N2), :]
            ridx = lax.broadcasted_iota(jnp.int32, (N2, CB), 0) + rows
            v = jnp.where(ridx == 0, 0.0 if backward else v + b_ref[...], v)
            z_ref[0, pl.ds(j * Z_STRIDE, N2), :] = v
            return carry
        lax.fori_loop(0, H1, body, 0, unroll=4)

    def spectrum(accumulate):
        def body(it, carry):
            k1b = it * KB
            xr, xi = _load_spectrum_blocks(ab_ref, f2_ref, k1b)
            for i in range(KB):
                rows = _spectrum_rows(k1b, i)
                lanes = slice(i * CB, (i + 1) * CB)
                if accumulate:
                    kr_ref[rows, :] = (acc_ref[0, rows, :] + xr[:, lanes] * inv_n).astype(BF16)
                    ki_ref[rows, :] = (acc_ref[1, rows, :] - xi[:, lanes] * inv_n).astype(BF16)
                else:
                    acc_ref[0, rows, :] = xr[:, lanes] * inv_n
                    acc_ref[1, rows, :] = xi[:, lanes] * inv_n
            return carry
        lax.fori_loop(0, N1 // KB, body, 0, unroll=MID_UNROLL)

    load_time(hf_ref, False)
    _stage_f1(z_ref, ab_ref, f1_ref, True)
    spectrum(False)
    load_time(hb_ref, True)
    _stage_f1(z_ref, ab_ref, f1_ref, True)
    spectrum(True)


def _kf_call(h, bias, f1, f2):
    blocks_per_order = N_CB
    bias_spec = pl.BlockSpec((None, 1, CB), lambda o, c: (o, 0, c))
    fwd = pl.BlockSpec((SEQ, CB), lambda o, c: (0, o * blocks_per_order + c))
    bwd = pl.BlockSpec((SEQ, CB), lambda o, c: (0, (HYENA_ORDER + o) * blocks_per_order + c))
    out = pl.BlockSpec((None, N_FFT, CB), lambda o, c: (o, 0, c))
    return pl.pallas_call(
        _kf_kernel,
        grid=(HYENA_ORDER, N_CB),
        in_specs=[fwd, bwd, bias_spec, _const_spec((N2, 2 * N1, 2 * H1)), _const_spec((2 * N2, 2 * N2))],
        out_specs=[out, out],
        out_shape=[jax.ShapeDtypeStruct((HYENA_ORDER, N_FFT, D_HYENA), BF16)] * 2,
        scratch_shapes=[pltpu.VMEM((2, H1 * Z_STRIDE, CB), F32), pltpu.VMEM((2, N2 * A_STRIDE, CB), F32),
                        pltpu.VMEM((2, N_FFT, CB), F32)],
        compiler_params=_cparams(("parallel", "parallel")),
        name="kf",
    )(h, h, bias, f1, f2)


TIME_ROWS = 256
GROUPS = TIME_ROWS // N2


def _group_rows(j, i):
    return pl.ds((j * GROUPS + i) * Z_STRIDE, N2)


def _hyena_kernel(a_ref, g_ref, kr_ref, ki_ref, f1_ref, f2_ref, i1_ref, i2_ref, o_ref,
                  z_ref, s1_ref, s2_ref):
    def fill(j, carry):
        rows = pl.ds(pl.multiple_of(j * TIME_ROWS, TIME_ROWS), TIME_ROWS)
        for r in range(2):
            val = a_ref[r, rows, :].astype(F32)
            for i in range(GROUPS):
                z_ref[r, _group_rows(j, i), :] = val[i * N2:(i + 1) * N2]
        return carry
    lax.fori_loop(0, SEQ // TIME_ROWS, fill, 0, unroll=2)

    _stage_f1(z_ref, s1_ref, f1_ref, False)

    def store_blocks(k1b, vr, vi):
        for i in range(KB):
            lanes = slice(i * CB, (i + 1) * CB)
            s2_ref[0, _k1_rows(k1b + i), :] = vr[:, lanes]
            s2_ref[1, _k1_rows(k1b + i), :] = vi[:, lanes]

    def mid_fwd(it, carry):
        k1b = it * KB
        xr, xi = _load_spectrum_blocks(s1_ref, f2_ref, k1b)
        kr = jnp.concatenate([kr_ref[_spectrum_rows(k1b, i), :] for i in range(KB)], axis=1).astype(F32)
        ki = jnp.concatenate([ki_ref[_spectrum_rows(k1b, i), :] for i in range(KB)], axis=1).astype(F32)
        store_blocks(k1b, xr * kr - xi * ki, xr * ki + xi * kr)
        return carry
    lax.fori_loop(0, N1 // KB, mid_fwd, 0, unroll=MID_UNROLL)

    def mid_inv(it, carry):
        k1b = it * KB
        yr = jnp.concatenate([s2_ref[0, _k1_rows(k1b + i), :] for i in range(KB)], axis=1)
        yi = jnp.concatenate([s2_ref[1, _k1_rows(k1b + i), :] for i in range(KB)], axis=1)
        br, bi = _dft_blocks(i1_ref, yr, yi, False)
        store_blocks(k1b, br, bi)
        return carry
    lax.fori_loop(0, N1 // KB, mid_inv, 0, unroll=MID_UNROLL)

    y_ref = s1_ref

    def last(n2, carry):
        st = jnp.concatenate([s2_ref[0, pl.ds(n2, N1, stride=K_STRIDE), :],
                              s2_ref[1, pl.ds(n2, N1, stride=K_STRIDE), :]], axis=1).astype(BF16)
        p = jnp.dot(i2_ref[n2], st, preferred_element_type=F32)
        y_ref[0, pl.ds(n2, H1, stride=Z_STRIDE), :] = p[:H1, :CB] + p[H1:, CB:]
        y_ref[1, pl.ds(n2, H1, stride=Z_STRIDE), :] = p[:H1, CB:] - p[H1:, :CB]
        return carry
    lax.fori_loop(0, N2, last, 0, unroll=STAGE_UNROLL)

    def finish(j, carry):
        rows = pl.ds(pl.multiple_of(j * TIME_ROWS, TIME_ROWS), TIME_ROWS)
        for r in range(2):
            y = jnp.concatenate([y_ref[r, _group_rows(j, i), :] for i in range(GROUPS)], axis=0)
            o_ref[r, rows, :] = (g_ref[r, rows, :].astype(F32) * y).astype(BF16)
        return carry
    lax.fori_loop(0, SEQ // TIME_ROWS, finish, 0, unroll=2)


def _hyena_call(a, a_slot, g, g_slot, kr, ki, order, tables):
    b = a.shape[0]
    f1, f2, i1, i2 = tables
    io = lambda slot: pl.BlockSpec((2, None, SEQ, CB), lambda c, p: (p, slot + c, 0, 0))
    kspec = pl.BlockSpec((None, N_FFT, CB), lambda c, p: (order, 0, c))
    assert N2 * A_STRIDE >= H1 * Z_STRIDE
    return pl.pallas_call(
        _hyena_kernel,
        grid=(N_CB, b // 2),
        in_specs=[io(a_slot), io(g_slot), kspec, kspec,
                  _const_spec((N2, 2 * N1, 2 * H1)), _const_spec((2 * N2, 2 * N2)),
                  _const_spec((2 * N2, 2 * N2)), _const_spec((N2, 2 * H1, N1))],
        out_specs=io(0),
        out_shape=jax.ShapeDtypeStruct((b, N_CB, SEQ, CB), BF16),
        scratch_shapes=[pltpu.VMEM((2, H1 * Z_STRIDE, CB), F32), pltpu.VMEM((2, N2 * A_STRIDE, CB), F32),
                        pltpu.VMEM((2, N1 * K_STRIDE, CB), F32)],
        compiler_params=_cparams(("parallel", "parallel")),
        name=f"hyena{order}",
    )(a, g, kr, ki, f1, f2, i1, i2)


ATT_TQ = 128
ATT_KW = ATT_TQ + 2 * WINDOW


ATT_QB = 8


def _attn_kernel(sink_ref, q_ref, k_ref, v_ref, o_ref):
    lo_q = lax.broadcasted_iota(jnp.int32, (ATT_TQ, LANES), 1) < HEAD_DIM
    for qb in range(ATT_QB):
        i = pl.program_id(1) * ATT_QB + qb
        start = pl.multiple_of(jnp.clip(i * ATT_TQ - WINDOW, 0, SEQ - ATT_KW), LANES)
        win = pl.ds(start, ATT_KW)
        qrows = slice(qb * ATT_TQ, (qb + 1) * ATT_TQ)
        qpos = i * ATT_TQ + lax.broadcasted_iota(jnp.int32, (ATT_TQ, ATT_KW), 0)
        kpos = start + lax.broadcasted_iota(jnp.int32, (ATT_TQ, ATT_KW), 1)
        bias = jnp.where(jnp.abs(qpos - kpos) <= WINDOW, 0.0, NEG_INF).astype(BF16)
        pairs = [(h, j) for h in range(N_KV_HEADS) for j in range(2)]
        scores = []
        for h, j in pairs:
            kbd = jnp.concatenate([k_ref[0, 2 * h, win, :], k_ref[0, 2 * h + 1, win, :]], axis=0)
            q2 = q_ref[0, qrows, LANES * (2 * h + j):LANES * (2 * h + j + 1)]
            scores.append(lax.dot_general(q2, kbd, (((1,), (1,)), ((), ())), preferred_element_type=F32))
        probs, scales = [], []
        for (h, j), s in zip(pairs, scores):
            ps, inv = [], []
            for e in range(2):
                sk = sink_ref[4 * h + 2 * j + e] * LOG2E
                se = s[:, e * ATT_KW:(e + 1) * ATT_KW].astype(BF16) + bias
                m = jnp.maximum(jnp.max(se, axis=-1, keepdims=True).astype(F32), sk)
                p = jnp.exp2(se - m.astype(BF16))
                part = functools.reduce(lambda a, b: a + b,
                                        [p[:, c * LANES:(c + 1) * LANES] for c in range(ATT_KW // LANES)])
                inv.append(1.0 / (jnp.sum(part.astype(F32), axis=-1, keepdims=True) + jnp.exp2(sk - m)))
                ps.append(p)
            probs.append(jnp.concatenate(ps, axis=1))
            scales.append(jnp.where(lo_q, inv[0], inv[1]))
        for (h, j), p2, sc in zip(pairs, probs, scales):
            vbd = jnp.concatenate([v_ref[0, 2 * h, win, :], v_ref[0, 2 * h + 1, win, :]], axis=0)
            o2 = jnp.dot(p2, vbd, preferred_element_type=F32)
            c0 = LANES * (2 * h + j)
            o_ref[0, qrows, c0:c0 + LANES] = (o2 * sc).astype(BF16)


def _attn_call(sink, q, k, v):
    b = q.shape[0]
    full = pl.BlockSpec((1, 2 * N_KV_HEADS, SEQ, LANES), lambda bi, i: (bi, 0, 0, 0))
    tile = pl.BlockSpec((1, ATT_TQ * ATT_QB, D_ATTN), lambda bi, i: (bi, i, 0))
    return pl.pallas_call(
        _attn_kernel,
        grid=(b, SEQ // (ATT_TQ * ATT_QB)),
        in_specs=[pl.BlockSpec(memory_space=pltpu.SMEM), tile, full, full],
        out_specs=tile,
        out_shape=jax.ShapeDtypeStruct((b, SEQ, D_ATTN), BF16),
        compiler_params=_cparams(("parallel", "parallel")),
        name="attn",
    )(sink, q, k, v)


MIX_TM = 512
MXU_COLS = 256
FF_SPLITS = (0, 6 * MXU_COLS, D_FF)


def _rms(x, g):
    return x * lax.rsqrt(jnp.mean(x * x, axis=-1, keepdims=True) + EPS) * g


def _mix_kernel(x_ref, yh_ref, ya_ref, gate_ref, wuh_ref, wua_ref, wo_ref, g2_ref, wg_ref, wu_ref, wd_ref,
                gf_ref, o_ref):
    yh = jnp.concatenate([yh_ref[0, c] for c in range(N_CB)], axis=1)
    up_h = jnp.dot(yh, wuh_ref[...], preferred_element_type=F32)
    up_a = jnp.dot(ya_ref[0], wua_ref[...], preferred_element_type=F32)
    merged = (gate_ref[0, :, :D_MODEL].astype(F32) * up_h + gate_ref[0, :, D_MODEL:].astype(F32) * up_a)
    x1 = x_ref[0] + jnp.dot(merged.astype(BF16), wo_ref[...], preferred_element_type=F32)
    hn = _rms(x1, g2_ref[...]).astype(BF16)
    acc = x1
    for c in range(len(FF_SPLITS) - 1):
        cols = slice(FF_SPLITS[c], FF_SPLITS[c + 1])
        gate = jnp.dot(hn, wg_ref[:, cols], preferred_element_type=F32)
        up = jnp.dot(hn, wu_ref[:, cols], preferred_element_type=F32)
        act = (jax.nn.silu(gate) * up).astype(BF16)
        acc = acc + jnp.dot(act, wd_ref[cols, :], preferred_element_type=F32)
    o_ref[0] = _rms(acc, gf_ref[...])


def _mix_call(x, batch0, yh, ya, gates, wuh, wua, wo, g2, wg, wu, wd, gf):
    b = x.shape[0]
    tile = lambda w: pl.BlockSpec((1, MIX_TM, w), lambda bi, i: (bi, i, 0))
    stacked_tile = lambda w: pl.BlockSpec((1, MIX_TM, w), lambda bi, i: (batch0 + bi, i, 0))
    return pl.pallas_call(
        _mix_kernel,
        grid=(b, SEQ // MIX_TM),
        in_specs=[tile(D_MODEL), pl.BlockSpec((1, N_CB, MIX_TM, CB), lambda bi, i: (batch0 + bi, 0, i, 0)),
                  stacked_tile(D_ATTN), stacked_tile(2 * D_MODEL),
                  _const_spec((D_HYENA, D_MODEL)), _const_spec((D_ATTN, D_MODEL)),
                  _const_spec((D_MODEL, D_MODEL)), _const_spec((1, D_MODEL)),
                  _const_spec((D_MODEL, D_FF)), _const_spec((D_MODEL, D_FF)), _const_spec((D_FF, D_MODEL)),
                  _const_spec((1, D_MODEL))],
        out_specs=tile(D_MODEL),
        out_shape=jax.ShapeDtypeStruct((b, SEQ, D_MODEL), F32),
        compiler_params=_cparams(("parallel", "parallel")),
        name="mix",
    )(x, yh, ya, gates, wuh, wua, wo, g2, wg, wu, wd, gf)


def _layers(xs, tables, rope, kf, norm1_g, w_in, short_w, short_b, sink_logit, w_up_hyena,
            w_up_attn, w_out, norm2_g, w_ff_gate, w_ff_up, w_ff_down, final_g):
    kr, ki = kf
    total = sum(x.shape[0] for x in xs)
    conv_w = jnp.concatenate([short_w, short_b[None, :], jnp.zeros((4, D_UH), F32)], axis=0)
    w_in = w_in.astype(BF16)
    stacked, offsets, off = None, [], 0
    for x in xs:
        stacked = _proj_call(x, off, total, stacked, norm1_g[None, :], w_in, conv_w, *rope)
        offsets.append(off)
        off += x.shape[0]
    uh, q, k, v, gates = stacked
    blocks = D_HYENA // LANES
    z1 = _hyena_call(uh, 0, uh, blocks, kr, ki, 0, tables)
    yh = _hyena_call(z1, 0, uh, 2 * blocks, kr, ki, 1, tables)
    ya = _attn_call(sink_logit, q, k, v)
    weights = (w_up_hyena.astype(BF16), w_up_attn.astype(BF16), w_out.astype(BF16), norm2_g[None, :],
               w_ff_gate.astype(BF16), w_ff_up.astype(BF16), w_ff_down.astype(BF16), final_g[None, :])
    return tuple(_mix_call(x, o, yh, ya, gates, *weights) for x, o in zip(xs, offsets))


def kernel(x_prompt, x_sample, norm1_g, w_in, short_w, short_b, filt_w0, filt_b0, filt_w_inner, filt_b_inner,
           filt_w_out, filt_freq, hyena_bias, sink_logit, w_up_hyena, w_up_attn, w_out, norm2_g, w_ff_gate,
           w_ff_up, w_ff_down, final_g):
    tables = _dft_tables()
    rope = _rope_tables()
    z_emb, deltas = _filter_embedding()
    w0 = jnp.pad(filt_w0[0], ((0, LANES - FILTER_EMB), (0, 0)))
    h = _filter_call(z_emb, w0, filt_b0[0][None, :], filt_w_inner[0], filt_b_inner[0][:, None, :],
                     filt_w_out[0], filt_freq[0][None, :], deltas)
    kf = _kf_call(h, hyena_bias[0][:, None, :], tables[0], tables[1])
    args = (norm1_g[0], w_in[0], short_w[0], short_b[0], sink_logit[0], w_up_hyena[0],
            w_up_attn[0], w_out[0], norm2_g[0], w_ff_gate[0], w_ff_up[0], w_ff_down[0], final_g)
    y_prompt, y_sample = _layers((x_prompt, x_sample), tables, rope, kf, *args)
    return (y_prompt, y_sample)
```

```python
import functools
import math

import jax
import jax.numpy as jnp
from jax import lax
from jax.experimental import pallas as pl
from jax.experimental.pallas import tpu as pltpu

F32 = jnp.float32
BF16 = jnp.bfloat16

D_MODEL = 1024
SEQ = 4096
D_HYENA = 512
HYENA_ORDER = 2
SHORT_CONV = 3
FILTER_EMB = 33
FILTER_ORDER = 64
N_FILT_CH = 2 * HYENA_ORDER * D_HYENA
FAST_DECAY_PCT = 0.3
SLOW_DECAY_PCT = 1.5
DECAY_TARGET = 1e-2
N_Q_HEADS = 8
N_KV_HEADS = 2
HEAD_DIM = 64
D_ATTN = N_Q_HEADS * HEAD_DIM
D_KV = N_KV_HEADS * HEAD_DIM
WINDOW = 128
ROPE_THETA = 500000.0
ROT_DIM = HEAD_DIM // 4
D_UH = (HYENA_ORDER + 1) * D_HYENA
D_IN = D_UH + D_ATTN + 2 * D_KV + 2 * D_MODEL
D_FF = 2816
EPS = 1e-6
NEG_INF = -1e30

LANES = 128
SUBLANES = 8
N_FFT = 2 * SEQ
N1 = 128
N2 = N_FFT // N1
H1 = N1 // 2
Z_STRIDE = N2 + 1
A_STRIDE = N1 + 1
K_STRIDE = N2 + 1
CB = LANES
N_CB = D_HYENA // CB
VMEM_LIMIT = 56 * 1024 * 1024


def _cparams(sem):
    return pltpu.CompilerParams(dimension_semantics=sem, vmem_limit_bytes=VMEM_LIMIT)


def _cast_rider_specs(weights, n_chunks, flat_step):
    specs = []
    for w in weights:
        rows = w.shape[0] // n_chunks
        assert rows * n_chunks == w.shape[0] and rows % 16 == 0
        specs.append(pl.BlockSpec((rows, w.shape[1]),
                                  lambda *g: (jnp.minimum(flat_step(*g), n_chunks - 1), 0)))
    return specs


def _cast_rider(step, n_chunks, src_refs, dst_refs):
    @pl.when(step < n_chunks)
    def _():
        for src, dst in zip(src_refs, dst_refs):
            dst[...] = src[...].astype(BF16)


def _const_spec(shape):
    nd = len(shape)
    return pl.BlockSpec(shape, lambda *_: (0,) * nd, pipeline_mode=pl.Buffered(1))


def _dft_tables():
    two_pi = 2.0 * math.pi
    n1 = jnp.arange(H1, dtype=jnp.int32)
    n2 = jnp.arange(N2, dtype=jnp.int32)
    k1 = jnp.arange(N1, dtype=jnp.int32)
    ang_a = ((n1[None, :] * k1[:, None]) % N1).astype(F32) * (two_pi / N1)
    ang_t = (n2[:, None] * k1[None, :]).astype(F32) * (two_pi / N_FFT)
    ar, ai = jnp.cos(ang_a)[None], -jnp.sin(ang_a)[None]
    tr, ti = jnp.cos(ang_t)[:, :, None], -jnp.sin(ang_t)[:, :, None]
    er, ei = ar * tr - ai * ti, ar * ti + ai * tr
    f1 = jnp.concatenate([jnp.concatenate([er, -ei], axis=2),
                          jnp.concatenate([ei, er], axis=2)], axis=1)
    ert, eit = jnp.swapaxes(er, 1, 2), jnp.swapaxes(ei, 1, 2)
    i2 = jnp.concatenate([ert, eit], axis=1)
    mg = (n2[:, None] * n2[None, :]) % N2
    angg = mg.astype(F32) * (two_pi / N2)
    gr, gi = jnp.cos(angg), -jnp.sin(angg)
    f2 = jnp.concatenate([jnp.concatenate([gr, -gi], axis=1),
                          jnp.concatenate([gi, gr], axis=1)], axis=0)
    f2 = f2.reshape(2, N2 // SUBLANES, SUBLANES, 2 * N2).transpose(1, 0, 2, 3).reshape(2 * N2, 2 * N2)
    i1 = jnp.concatenate([jnp.concatenate([gr, gi], axis=1),
                          jnp.concatenate([-gi, gr], axis=1)], axis=0)
    return f1.astype(BF16), f2.astype(BF16), i1.astype(BF16), i2.astype(BF16)


def _rope_tables():
    half = ROT_DIM // 2
    pos = jnp.arange(SEQ, dtype=F32)
    inv = 1.0 / (ROPE_THETA ** (jnp.arange(0, ROT_DIM, 2, dtype=F32) / ROT_DIM))
    ang = pos[:, None] * inv[None, :]
    cos, sin = jnp.cos(ang), jnp.sin(ang)
    ones = jnp.ones((SEQ, HEAD_DIM - ROT_DIM), F32)
    zeros = jnp.zeros((SEQ, HEAD_DIM - ROT_DIM), F32)
    zh = jnp.zeros((SEQ, half), F32)
    c = jnp.concatenate([cos, cos, ones], axis=1)
    s_up = jnp.concatenate([-sin, zh, zeros], axis=1)
    s_dn = jnp.concatenate([zh, sin, zeros], axis=1)
    rep = LANES // HEAD_DIM
    return jnp.tile(c, (1, rep)), jnp.tile(s_up, (1, rep)), jnp.tile(s_dn, (1, rep))


def _filter_embedding():
    bands = (FILTER_EMB - 1) // 2
    t = jnp.linspace(0.0, 1.0, SEQ, dtype=F32)[:, None]
    w = 2.0 * math.pi * jnp.arange(SEQ, dtype=F32)[:, None] / SEQ
    f = jnp.linspace(1e-4, bands - 1, bands, dtype=F32)[None, :]
    z = jnp.concatenate([t, jnp.cos(f * w), jnp.sin(f * w)], axis=-1)
    z = jnp.pad(z, ((0, 0), (0, LANES - FILTER_EMB)))
    min_decay = math.log(DECAY_TARGET) / SLOW_DECAY_PCT
    max_decay = math.log(DECAY_TARGET) / FAST_DECAY_PCT
    deltas = jnp.tile(jnp.linspace(min_decay, max_decay, D_HYENA, dtype=F32), 2 * HYENA_ORDER)
    return z, jnp.abs(deltas)[None, :]


PROJ_TM = 1024
PROJ_HALO = 16
LOG2E = math.log2(math.e)
Q_SCALE = HEAD_DIM ** -0.5 * LOG2E


def _proj_kernel(x_ref, xp_ref, xn_ref, g_ref, w_ref, cw_ref, c_ref, su_ref, sd_ref, *refs):
    uh_ref, q_ref, k_ref, v_ref, gate_ref = refs[-5:]

    def norm(xf):
        return xf * lax.rsqrt(jnp.mean(xf * xf, axis=-1, keepdims=True) + EPS) * g_ref[...]

    i = pl.program_id(1)
    hn = norm(x_ref[0]).astype(BF16)
    h_prev = (norm(xp_ref[0]) * (i > 0).astype(F32)).astype(BF16)
    h_next = (norm(xn_ref[0]) * (i < pl.num_programs(1) - 1).astype(F32)).astype(BF16)
    hn_ext = jnp.concatenate([h_prev, hn, h_next], axis=0)

    def proj(c0, width):
        return jnp.dot(hn, w_ref[:, c0:c0 + width], preferred_element_type=F32)

    rows = PROJ_TM + 2 * PROJ_HALO
    for j in range(D_UH // 512):
        cols = slice(j * 512, (j + 1) * 512)
        u = jnp.dot(hn_ext, w_ref[:, cols], preferred_element_type=F32)
        prev = pltpu.roll(u, 1, axis=0)[PROJ_HALO:PROJ_HALO + PROJ_TM]
        nxt = pltpu.roll(u, rows - 1, axis=0)[PROJ_HALO:PROJ_HALO + PROJ_TM]
        cur = u[PROJ_HALO:PROJ_HALO + PROJ_TM]
        uc = (prev * cw_ref[0:1, cols] + cur * cw_ref[1:2, cols] + nxt * cw_ref[2:3, cols] + cw_ref[3:4, cols])
        for c in range(512 // LANES):
            uh_ref[0, j * (512 // LANES) + c] = uc[:, c * LANES:(c + 1) * LANES].astype(BF16)

    def rope(xc):
        return (xc * c_ref[...] + pltpu.roll(xc, LANES - ROT_DIM // 2, axis=1) * su_ref[...]
                + pltpu.roll(xc, ROT_DIM // 2, axis=1) * sd_ref[...])

    qkv = proj(D_UH, D_ATTN + 2 * D_KV)
    for j in range(D_ATTN // LANES):
        qc = qkv[:, j * LANES:(j + 1) * LANES]
        q_ref[0, :, j * LANES:(j + 1) * LANES] = (rope(qc) * Q_SCALE).astype(BF16)
    lo = lax.broadcasted_iota(jnp.int32, (PROJ_TM, LANES), 1) < HEAD_DIM
    for val, ref in ((rope(qkv[:, D_ATTN:D_ATTN + D_KV]), k_ref), (qkv[:, D_ATTN + D_KV:], v_ref)):
        rolled = pltpu.roll(val, HEAD_DIM, axis=1)
        ref[0, 0] = jnp.where(lo, val, 0.0).astype(BF16)
        ref[0, 1] = jnp.where(lo, 0.0, rolled).astype(BF16)
        ref[0, 2] = jnp.where(lo, rolled, 0.0).astype(BF16)
        ref[0, 3] = jnp.where(lo, 0.0, val).astype(BF16)
    g0 = D_UH + D_ATTN + 2 * D_KV
    for j in range(2 * D_MODEL // 512):
        gate_ref[0, :, j * 512:(j + 1) * 512] = jax.nn.sigmoid(proj(g0 + j * 512, 512)).astype(BF16)


def _proj_call(x, batch0, total, stacked, g1, w_in, conv_w, rope_c, rope_su, rope_sd):
    b = x.shape[0]
    nt = SEQ // PROJ_TM
    per_tile = PROJ_TM // PROJ_HALO
    tile = lambda w: pl.BlockSpec((1, PROJ_TM, w), lambda bi, i: (bi, i, 0))
    halo_prev = pl.BlockSpec((1, PROJ_HALO, D_MODEL), lambda bi, i: (bi, jnp.maximum(i * per_tile - 1, 0), 0))
    halo_next = pl.BlockSpec((1, PROJ_HALO, D_MODEL),
                             lambda bi, i: (bi, jnp.minimum((i + 1) * per_tile, SEQ // PROJ_HALO - 1), 0))
    rope_spec = pl.BlockSpec((PROJ_TM, LANES), lambda bi, i: (i, 0))
    out_tile = lambda w: pl.BlockSpec((1, PROJ_TM, w), lambda bi, i: (batch0 + bi, i, 0))
    out_slabs = lambda n: pl.BlockSpec((1, n, PROJ_TM, LANES), lambda bi, i: (batch0 + bi, 0, i, 0))
    in_specs = [tile(D_MODEL), halo_prev, halo_next, _const_spec((1, D_MODEL)), _const_spec((D_MODEL, D_IN)),
                _const_spec((8, D_UH)), rope_spec, rope_spec, rope_spec]
    operands = [x, x, x, g1, w_in, conv_w, rope_c, rope_su, rope_sd]
    aliases = {}
    if stacked is not None:
        aliases = {len(operands) + n: n for n in range(len(stacked))}
        in_specs += [pl.BlockSpec(memory_space=pl.ANY)] * len(stacked)
        operands += list(stacked)
    return pl.pallas_call(
        _proj_kernel,
        grid=(b, nt),
        in_specs=in_specs,
        out_specs=[out_slabs(D_UH // LANES), out_tile(D_ATTN), out_slabs(2 * N_KV_HEADS),
                   out_slabs(2 * N_KV_HEADS), out_tile(2 * D_MODEL)],
        out_shape=[jax.ShapeDtypeStruct((total, D_UH // LANES, SEQ, LANES), BF16),
                   jax.ShapeDtypeStruct((total, SEQ, D_ATTN), BF16),
                   jax.ShapeDtypeStruct((total, 2 * N_KV_HEADS, SEQ, LANES), BF16),
                   jax.ShapeDtypeStruct((total, 2 * N_KV_HEADS, SEQ, LANES), BF16),
                   jax.ShapeDtypeStruct((total, SEQ, 2 * D_MODEL), BF16)],
        input_output_aliases=aliases,
        compiler_params=_cparams(("parallel", "parallel")),
        name="proj",
    )(*operands)


FILT_TL = 512


def _filter_kernel(z_ref, w0_ref, b0_ref, wi_ref, bi_ref, wo_ref, fr_ref, dl_ref, ride_src, h_ref, ride_dst):
    _cast_rider(pl.program_id(0), pl.num_programs(0), [ride_src], [ride_dst])
    hi = lax.Precision.HIGHEST
    half = FILT_TL // 2
    z = z_ref[...]
    z2 = jnp.concatenate([z[:half], z[half:]], axis=1)
    fr = fr_ref[...]
    h = jnp.sin(fr * (jnp.dot(z2, w0_ref[...], precision=hi, preferred_element_type=F32) + b0_ref[...]))
    for i in range(wi_ref.shape[0]):
        h = jnp.sin(fr * (jnp.dot(h, wi_ref[i], precision=hi, preferred_element_type=F32) + bi_ref[i]))
    def split(v):
        top = v.astype(BF16)
        return top, (v - top.astype(F32)).astype(BF16)
    h_hi, h_lo = split(h)
    w_hi, w_lo = split(wo_ref[...])
    out = (jnp.dot(h_hi, w_hi, preferred_element_type=F32) + jnp.dot(h_hi, w_lo, preferred_element_type=F32)
           + jnp.dot(h_lo, w_hi, preferred_element_type=F32))
    for r in range(2):
        rows = slice(r * half, (r + 1) * half)
        t = z[rows, 0:1]
        h_ref[rows, :] = out[:, r * N_FILT_CH:(r + 1) * N_FILT_CH] * jnp.exp(-t * dl_ref[...])


def _block_diag2(w):
    zeros = jnp.zeros_like(w)
    return jnp.concatenate([jnp.concatenate([w, zeros], axis=-1), jnp.concatenate([zeros, w], axis=-1)], axis=-2)


def _filter_call(z, w0, b0, wi, bi, wo, fr, deltas, rider):
    n_inner = wi.shape[0]
    twice = lambda v: jnp.concatenate([v, v], axis=-1)
    w0, wi, wo = _block_diag2(w0), _block_diag2(wi), _block_diag2(wo)
    b0, bi, fr = twice(b0), twice(bi), twice(fr)
    wide = 2 * FILTER_ORDER
    steps = SEQ // FILT_TL
    ride_in, = _cast_rider_specs([rider], steps, lambda i: i)
    ride_out, = _cast_rider_specs([rider], steps, lambda i: i)
    return pl.pallas_call(
        _filter_kernel,
        grid=(steps,),
        in_specs=[pl.BlockSpec((FILT_TL, LANES), lambda i: (i, 0)),
                  _const_spec((2 * LANES, wide)), _const_spec((1, wide)),
                  _const_spec((n_inner, wide, wide)), _const_spec((n_inner, 1, wide)),
                  _const_spec((wide, 2 * N_FILT_CH)), _const_spec((1, wide)),
                  _const_spec((1, N_FILT_CH)), ride_in],
        out_specs=[pl.BlockSpec((FILT_TL, N_FILT_CH), lambda i: (i, 0)), ride_out],
        out_shape=[jax.ShapeDtypeStruct((SEQ, N_FILT_CH), F32), jax.ShapeDtypeStruct(rider.shape, BF16)],
        compiler_params=_cparams(("arbitrary",)),
        name="filt",
    )(z, w0, b0, wi, bi, wo, fr, deltas, rider)


STAGE_UNROLL = 64
KB = 2
MID_UNROLL = 64


def _stage_f1(z_ref, a_ref, f1_ref, real_only):
    def body(n2, carry):
        zr = z_ref[0, pl.ds(n2, H1, stride=Z_STRIDE), :]
        if real_only:
            res = jnp.dot(f1_ref[n2][:, :H1], zr.astype(BF16), preferred_element_type=F32)
        else:
            zi = z_ref[1, pl.ds(n2, H1, stride=Z_STRIDE), :]
            st = jnp.concatenate([zr, zi], axis=0).astype(BF16)
            res = jnp.dot(f1_ref[n2], st, preferred_element_type=F32)
        base = n2 * A_STRIDE
        a_ref[0, pl.ds(base, N1), :] = res[:N1]
        a_ref[1, pl.ds(base, N1), :] = res[N1:]
        return carry
    lax.fori_loop(0, N2, body, 0, unroll=STAGE_UNROLL)


def _k1_rows(k1):
    return pl.ds(k1 * K_STRIDE, N2)


def _dft_blocks(mat_ref, vr, vi, interleaved):
    st = jnp.concatenate([vr, vi], axis=0).astype(BF16)
    x = jnp.dot(mat_ref[...], st, preferred_element_type=F32)
    if not interleaved:
        return x[:N2], x[N2:]
    x4 = x.reshape(N2 // SUBLANES, 2, SUBLANES, x.shape[-1])
    return x4[:, 0].reshape(N2, x.shape[-1]), x4[:, 1].reshape(N2, x.shape[-1])


def _load_spectrum_blocks(a_ref, f2_ref, k1b):
    ar = jnp.concatenate([a_ref[0, pl.ds(k1b + i, N2, stride=A_STRIDE), :] for i in range(KB)], axis=1)
    ai = jnp.concatenate([a_ref[1, pl.ds(k1b + i, N2, stride=A_STRIDE), :] for i in range(KB)], axis=1)
    return _dft_blocks(f2_ref, ar, ai, True)


def _spectrum_rows(k1b, i):
    return pl.ds(pl.multiple_of((k1b + i) * N2, N2), N2)


def _kf_kernel(hf_ref, hb_ref, b_ref, f1_ref, f2_ref, kr_ref, ki_ref, z_ref, ab_ref, acc_ref):
    inv_n = 1.0 / N_FFT

    def load_time(src_ref, backward):
        def body(j, carry):
            rows = pl.multiple_of(j * N2, N2)
            v = src_ref[pl.ds(rows, N2), :]
            ridx = lax.broadcasted_iota(jnp.int32, (N2, CB), 0) + rows
            v = jnp.where(ridx == 0, 0.0 if backward else v + b_ref[...], v)
            z_ref[0, pl.ds(j * Z_STRIDE, N2), :] = v
            return carry
        lax.fori_loop(0, H1, body, 0, unroll=4)

    def spectrum(accumulate):
        def body(it, carry):
            k1b = it * KB
            xr, xi = _load_spectrum_blocks(ab_ref, f2_ref, k1b)
            for i in range(KB):
                rows = _spectrum_rows(k1b, i)
                lanes = slice(i * CB, (i + 1) * CB)
                if accumulate:
                    kr_ref[rows, :] = (acc_ref[0, rows, :] + xr[:, lanes] * inv_n).astype(BF16)
                    ki_ref[rows, :] = (acc_ref[1, rows, :] - xi[:, lanes] * inv_n).astype(BF16)
                else:
                    acc_ref[0, rows, :] = xr[:, lanes] * inv_n
                    acc_ref[1, rows, :] = xi[:, lanes] * inv_n
            return carry
        lax.fori_loop(0, N1 // KB, body, 0, unroll=MID_UNROLL)

    load_time(hf_ref, False)
    _stage_f1(z_ref, ab_ref, f1_ref, True)
    spectrum(False)
    load_time(hb_ref, True)
    _stage_f1(z_ref, ab_ref, f1_ref, True)
    spectrum(True)


def _kf_call(h, bias, f1, f2):
    blocks_per_order = N_CB
    bias_spec = pl.BlockSpec((None, 1, CB), lambda o, c: (o, 0, c))
    fwd = pl.BlockSpec((SEQ, CB), lambda o, c: (0, o * blocks_per_order + c))
    bwd = pl.BlockSpec((SEQ, CB), lambda o, c: (0, (HYENA_ORDER + o) * blocks_per_order + c))
    out = pl.BlockSpec((None, N_FFT, CB), lambda o, c: (o, 0, c))
    return pl.pallas_call(
        _kf_kernel,
        grid=(HYENA_ORDER, N_CB),
        in_specs=[fwd, bwd, bias_spec, _const_spec((N2, 2 * N1, 2 * H1)), _const_spec((2 * N2, 2 * N2))],
        out_specs=[out, out],
        out_shape=[jax.ShapeDtypeStruct((HYENA_ORDER, N_FFT, D_HYENA), BF16)] * 2,
        scratch_shapes=[pltpu.VMEM((2, H1 * Z_STRIDE, CB), F32), pltpu.VMEM((2, N2 * A_STRIDE, CB), F32),
                        pltpu.VMEM((2, N_FFT, CB), F32)],
        compiler_params=_cparams(("parallel", "parallel")),
        name="kf",
    )(h, h, bias, f1, f2)


TIME_ROWS = 256
GROUPS = TIME_ROWS // N2


def _group_rows(j, i):
    return pl.ds((j * GROUPS + i) * Z_STRIDE, N2)


def _hyena_kernel(a_ref, g_ref, kr_ref, ki_ref, f1_ref, f2_ref, i1_ref, i2_ref, o_ref,
                  z_ref, s1_ref, s2_ref):
    def fill(j, carry):
        rows = pl.ds(pl.multiple_of(j * TIME_ROWS, TIME_ROWS), TIME_ROWS)
        for r in range(2):
            val = a_ref[r, rows, :].astype(F32)
            for i in range(GROUPS):
                z_ref[r, _group_rows(j, i), :] = val[i * N2:(i + 1) * N2]
        return carry
    lax.fori_loop(0, SEQ // TIME_ROWS, fill, 0, unroll=2)

    _stage_f1(z_ref, s1_ref, f1_ref, False)

    def store_blocks(k1b, vr, vi):
        for i in range(KB):
            lanes = slice(i * CB, (i + 1) * CB)
            s2_ref[0, _k1_rows(k1b + i), :] = vr[:, lanes]
            s2_ref[1, _k1_rows(k1b + i), :] = vi[:, lanes]

    def mid_fwd(it, carry):
        k1b = it * KB
        xr, xi = _load_spectrum_blocks(s1_ref, f2_ref, k1b)
        kr = jnp.concatenate([kr_ref[_spectrum_rows(k1b, i), :] for i in range(KB)], axis=1).astype(F32)
        ki = jnp.concatenate([ki_ref[_spectrum_rows(k1b, i), :] for i in range(KB)], axis=1).astype(F32)
        store_blocks(k1b, xr * kr - xi * ki, xr * ki + xi * kr)
        return carry
    lax.fori_loop(0, N1 // KB, mid_fwd, 0, unroll=MID_UNROLL)

    def mid_inv(it, carry):
        k1b = it * KB
        yr = jnp.concatenate([s2_ref[0, _k1_rows(k1b + i), :] for i in range(KB)], axis=1)
        yi = jnp.concatenate([s2_ref[1, _k1_rows(k1b + i), :] for i in range(KB)], axis=1)
        br, bi = _dft_blocks(i1_ref, yr, yi, False)
        store_blocks(k1b, br, bi)
        return carry
    lax.fori_loop(0, N1 // KB, mid_inv, 0, unroll=MID_UNROLL)

    y_ref = s1_ref

    def last(n2, carry):
        st = jnp.concatenate([s2_ref[0, pl.ds(n2, N1, stride=K_STRIDE), :],
                              s2_ref[1, pl.ds(n2, N1, stride=K_STRIDE), :]], axis=1).astype(BF16)
        p = jnp.dot(i2_ref[n2], st, preferred_element_type=F32)
        y_ref[0, pl.ds(n2, H1, stride=Z_STRIDE), :] = p[:H1, :CB] + p[H1:, CB:]
        y_ref[1, pl.ds(n2, H1, stride=Z_STRIDE), :] = p[:H1, CB:] - p[H1:, :CB]
        return carry
    lax.fori_loop(0, N2, last, 0, unroll=STAGE_UNROLL)

    def finish(j, carry):
        rows = pl.ds(pl.multiple_of(j * TIME_ROWS, TIME_ROWS), TIME_ROWS)
        for r in range(2):
            y = jnp.concatenate([y_ref[r, _group_rows(j, i), :] for i in range(GROUPS)], axis=0)
            o_ref[r, rows, :] = (g_ref[r, rows, :].astype(F32) * y).astype(BF16)
        return carry
    lax.fori_loop(0, SEQ // TIME_ROWS, finish, 0, unroll=2)


def _hyena_call(a, a_slot, g, g_slot, kr, ki, order, tables):
    b = a.shape[0]
    f1, f2, i1, i2 = tables
    io = lambda slot: pl.BlockSpec((2, None, SEQ, CB), lambda c, p: (p, slot + c, 0, 0))
    kspec = pl.BlockSpec((None, N_FFT, CB), lambda c, p: (order, 0, c))
    assert N2 * A_STRIDE >= H1 * Z_STRIDE
    return pl.pallas_call(
        _hyena_kernel,
        grid=(N_CB, b // 2),
        in_specs=[io(a_slot), io(g_slot), kspec, kspec,
                  _const_spec((N2, 2 * N1, 2 * H1)), _const_spec((2 * N2, 2 * N2)),
                  _const_spec((2 * N2, 2 * N2)), _const_spec((N2, 2 * H1, N1))],
        out_specs=io(0),
        out_shape=jax.ShapeDtypeStruct((b, N_CB, SEQ, CB), BF16),
        scratch_shapes=[pltpu.VMEM((2, H1 * Z_STRIDE, CB), F32), pltpu.VMEM((2, N2 * A_STRIDE, CB), F32),
                        pltpu.VMEM((2, N1 * K_STRIDE, CB), F32)],
        compiler_params=_cparams(("parallel", "parallel")),
        name=f"hyena{order}",
    )(a, g, kr, ki, f1, f2, i1, i2)


ATT_TQ = 128
ATT_KW = ATT_TQ + 2 * WINDOW


ATT_QB = 8
ATT_RIDER_CHUNKS = 16


def _attn_kernel(sink_ref, q_ref, k_ref, v_ref, *refs):
    n_riders = (len(refs) - 1) // 2
    o_ref = refs[n_riders]
    _cast_rider(pl.program_id(0) * pl.num_programs(1) + pl.program_id(1), ATT_RIDER_CHUNKS,
                refs[:n_riders], refs[n_riders + 1:])
    lo_q = lax.broadcasted_iota(jnp.int32, (ATT_TQ, LANES), 1) < HEAD_DIM
    for qb in range(ATT_QB):
        i = pl.program_id(1) * ATT_QB + qb
        start = pl.multiple_of(jnp.clip(i * ATT_TQ - WINDOW, 0, SEQ - ATT_KW), LANES)
        win = pl.ds(start, ATT_KW)
        qrows = slice(qb * ATT_TQ, (qb + 1) * ATT_TQ)
        qpos = i * ATT_TQ + lax.broadcasted_iota(jnp.int32, (ATT_TQ, ATT_KW), 0)
        kpos = start + lax.broadcasted_iota(jnp.int32, (ATT_TQ, ATT_KW), 1)
        bias = jnp.where(jnp.abs(qpos - kpos) <= WINDOW, 0.0, NEG_INF).astype(BF16)
        pairs = [(h, j) for h in range(N_KV_HEADS) for j in range(2)]
        scores = []
        for h, j in pairs:
            kbd = jnp.concatenate([k_ref[0, 2 * h, win, :], k_ref[0, 2 * h + 1, win, :]], axis=0)
            q2 = q_ref[0, qrows, LANES * (2 * h + j):LANES * (2 * h + j + 1)]
            scores.append(lax.dot_general(q2, kbd, (((1,), (1,)), ((), ())), preferred_element_type=F32))
        probs, scales = [], []
        for (h, j), s in zip(pairs, scores):
            ps, inv = [], []
            for e in range(2):
                sk = sink_ref[4 * h + 2 * j + e] * LOG2E
                se = s[:, e * ATT_KW:(e + 1) * ATT_KW].astype(BF16) + bias
                m = jnp.maximum(jnp.max(se, axis=-1, keepdims=True).astype(F32), sk)
                p = jnp.exp2(se - m.astype(BF16))
                part = functools.reduce(lambda a, b: a + b,
                                        [p[:, c * LANES:(c + 1) * LANES] for c in range(ATT_KW // LANES)])
                inv.append(1.0 / (jnp.sum(part.astype(F32), axis=-1, keepdims=True) + jnp.exp2(sk - m)))
                ps.append(p)
            probs.append(jnp.concatenate(ps, axis=1))
            scales.append(jnp.where(lo_q, inv[0], inv[1]))
        for (h, j), p2, sc in zip(pairs, probs, scales):
            vbd = jnp.concatenate([v_ref[0, 2 * h, win, :], v_ref[0, 2 * h + 1, win, :]], axis=0)
            o2 = jnp.dot(p2, vbd, preferred_element_type=F32)
            c0 = LANES * (2 * h + j)
            o_ref[0, qrows, c0:c0 + LANES] = (o2 * sc).astype(BF16)


def _attn_call(sink, q, k, v, riders):
    b = q.shape[0]
    nt = SEQ // (ATT_TQ * ATT_QB)
    assert not riders or b * nt >= ATT_RIDER_CHUNKS
    full = pl.BlockSpec((1, 2 * N_KV_HEADS, SEQ, LANES), lambda bi, i: (bi, 0, 0, 0))
    tile = pl.BlockSpec((1, ATT_TQ * ATT_QB, D_ATTN), lambda bi, i: (bi, i, 0))
    flat = lambda bi, i: bi * nt + i
    out = pl.pallas_call(
        _attn_kernel,
        grid=(b, nt),
        in_specs=[pl.BlockSpec(memory_space=pltpu.SMEM), tile, full, full,
                  *_cast_rider_specs(riders, ATT_RIDER_CHUNKS, flat)],
        out_specs=[tile, *_cast_rider_specs(riders, ATT_RIDER_CHUNKS, flat)],
        out_shape=[jax.ShapeDtypeStruct((b, SEQ, D_ATTN), BF16),
                   *[jax.ShapeDtypeStruct(w.shape, BF16) for w in riders]],
        compiler_params=_cparams(("arbitrary", "arbitrary")),
        name="attn",
    )(sink, q, k, v, *riders)
    return out[0], out[1:]


MIX_TM = 512
MXU_COLS = 256
FF_SPLITS = (0, 6 * MXU_COLS, D_FF)


def _rms(x, g):
    return x * lax.rsqrt(jnp.mean(x * x, axis=-1, keepdims=True) + EPS) * g


def _mix_kernel(x_ref, yh_ref, ya_ref, gate_ref, wuh_ref, wua_ref, wo_ref, g2_ref, wg_ref, wu_ref, wd_ref,
                gf_ref, o_ref):
    yh = jnp.concatenate([yh_ref[0, c] for c in range(N_CB)], axis=1)
    up_h = jnp.dot(yh, wuh_ref[...], preferred_element_type=F32)
    up_a = jnp.dot(ya_ref[0], wua_ref[...], preferred_element_type=F32)
    merged = (gate_ref[0, :, :D_MODEL].astype(F32) * up_h + gate_ref[0, :, D_MODEL:].astype(F32) * up_a)
    x1 = x_ref[0] + jnp.dot(merged.astype(BF16), wo_ref[...], preferred_element_type=F32)
    hn = _rms(x1, g2_ref[...]).astype(BF16)
    acc = x1
    for c in range(len(FF_SPLITS) - 1):
        cols = slice(FF_SPLITS[c], FF_SPLITS[c + 1])
        gate = jnp.dot(hn, wg_ref[:, cols], preferred_element_type=F32)
        up = jnp.dot(hn, wu_ref[:, cols], preferred_element_type=F32)
        act = (jax.nn.silu(gate) * up).astype(BF16)
        acc = acc + jnp.dot(act, wd_ref[cols, :], preferred_element_type=F32)
    o_ref[0] = _rms(acc, gf_ref[...])


def _mix_call(x, batch0, yh, ya, gates, wuh, wua, wo, g2, wg, wu, wd, gf):
    b = x.shape[0]
    tile = lambda w: pl.BlockSpec((1, MIX_TM, w), lambda bi, i: (bi, i, 0))
    stacked_tile = lambda w: pl.BlockSpec((1, MIX_TM, w), lambda bi, i: (batch0 + bi, i, 0))
    return pl.pallas_call(
        _mix_kernel,
        grid=(b, SEQ // MIX_TM),
        in_specs=[tile(D_MODEL), pl.BlockSpec((1, N_CB, MIX_TM, CB), lambda bi, i: (batch0 + bi, 0, i, 0)),
                  stacked_tile(D_ATTN), stacked_tile(2 * D_MODEL),
                  _const_spec((D_HYENA, D_MODEL)), _const_spec((D_ATTN, D_MODEL)),
                  _const_spec((D_MODEL, D_MODEL)), _const_spec((1, D_MODEL)),
                  _const_spec((D_MODEL, D_FF)), _const_spec((D_MODEL, D_FF)), _const_spec((D_FF, D_MODEL)),
                  _const_spec((1, D_MODEL))],
        out_specs=tile(D_MODEL),
        out_shape=jax.ShapeDtypeStruct((b, SEQ, D_MODEL), F32),
        compiler_params=_cparams(("parallel", "parallel")),
        name="mix",
    )(x, yh, ya, gates, wuh, wua, wo, g2, wg, wu, wd, gf)


def _layers(xs, tables, rope, kf, norm1_g, w_in, short_w, short_b, sink_logit, w_up_hyena,
            w_up_attn, w_out, norm2_g, w_ff_gate, w_ff_up, w_ff_down, final_g):
    kr, ki = kf
    total = sum(x.shape[0] for x in xs)
    conv_w = jnp.concatenate([short_w, short_b[None, :], jnp.zeros((4, D_UH), F32)], axis=0)
    stacked, offsets, off = None, [], 0
    for x in xs:
        stacked = _proj_call(x, off, total, stacked, norm1_g[None, :], w_in, conv_w, *rope)
        offsets.append(off)
        off += x.shape[0]
    uh, q, k, v, gates = stacked
    blocks = D_HYENA // LANES
    z1 = _hyena_call(uh, 0, uh, blocks, kr, ki, 0, tables)
    yh = _hyena_call(z1, 0, uh, 2 * blocks, kr, ki, 1, tables)
    ya, (wuh, wua, wo, wg, wu, wd) = _attn_call(sink_logit, q, k, v,
                                                (w_up_hyena, w_up_attn, w_out, w_ff_gate, w_ff_up, w_ff_down))
    weights = (wuh, wua, wo, norm2_g[None, :], wg, wu, wd, final_g[None, :])
    return tuple(_mix_call(x, o, yh, ya, gates, *weights) for x, o in zip(xs, offsets))


def kernel(x_prompt, x_sample, norm1_g, w_in, short_w, short_b, filt_w0, filt_b0, filt_w_inner, filt_b_inner,
           filt_w_out, filt_freq, hyena_bias, sink_logit, w_up_hyena, w_up_attn, w_out, norm2_g, w_ff_gate,
           w_ff_up, w_ff_down, final_g):
    tables = _dft_tables()
    rope = _rope_tables()
    z_emb, deltas = _filter_embedding()
    w0 = jnp.pad(filt_w0[0], ((0, LANES - FILTER_EMB), (0, 0)))
    h, w_in_bf16 = _filter_call(z_emb, w0, filt_b0[0][None, :], filt_w_inner[0], filt_b_inner[0][:, None, :],
                                filt_w_out[0], filt_freq[0][None, :], deltas, w_in[0])
    kf = _kf_call(h, hyena_bias[0][:, None, :], tables[0], tables[1])
    args = (norm1_g[0], w_in_bf16, short_w[0], short_b[0], sink_logit[0], w_up_hyena[0],
            w_up_attn[0], w_out[0], norm2_g[0], w_ff_gate[0], w_ff_up[0], w_ff_down[0], final_g)
    y_prompt, y_sample = _layers((x_prompt, x_sample), tables, rope, kf, *args)
    return (y_prompt, y_sample)
```

```python
import functools
import math

import jax
import jax.numpy as jnp
from jax import lax
from jax.experimental import pallas as pl
from jax.experimental.pallas import tpu as pltpu

F32 = jnp.float32
BF16 = jnp.bfloat16

D_MODEL = 1024
SEQ = 4096
D_HYENA = 512
HYENA_ORDER = 2
SHORT_CONV = 3
FILTER_EMB = 33
FILTER_ORDER = 64
N_FILT_CH = 2 * HYENA_ORDER * D_HYENA
FAST_DECAY_PCT = 0.3
SLOW_DECAY_PCT = 1.5
DECAY_TARGET = 1e-2
N_Q_HEADS = 8
N_KV_HEADS = 2
HEAD_DIM = 64
D_ATTN = N_Q_HEADS * HEAD_DIM
D_KV = N_KV_HEADS * HEAD_DIM
WINDOW = 128
ROPE_THETA = 500000.0
ROT_DIM = HEAD_DIM // 4
D_UH = (HYENA_ORDER + 1) * D_HYENA
D_IN = D_UH + D_ATTN + 2 * D_KV + 2 * D_MODEL
D_FF = 2816
EPS = 1e-6
NEG_INF = -1e30

LANES = 128
SUBLANES = 8
N_FFT = 2 * SEQ
N1 = 128
N2 = N_FFT // N1
H1 = N1 // 2
Z_STRIDE = N2 + 1
A_STRIDE = N1 + 1
K_STRIDE = N2 + 1
CB = LANES
N_CB = D_HYENA // CB
VMEM_LIMIT = 56 * 1024 * 1024


def _cparams(sem):
    return pltpu.CompilerParams(dimension_semantics=sem, vmem_limit_bytes=VMEM_LIMIT)


def _cast_rider_specs(weights, n_chunks, flat_step):
    specs = []
    for w in weights:
        rows = w.shape[0] // n_chunks
        assert rows * n_chunks == w.shape[0] and rows % 16 == 0
        specs.append(pl.BlockSpec((rows, w.shape[1]),
                                  lambda *g: (jnp.minimum(flat_step(*g), n_chunks - 1), 0)))
    return specs


def _cast_rider(step, n_chunks, src_refs, dst_refs):
    @pl.when(step < n_chunks)
    def _():
        for src, dst in zip(src_refs, dst_refs):
            dst[...] = src[...].astype(BF16)


def _const_spec(shape):
    nd = len(shape)
    return pl.BlockSpec(shape, lambda *_: (0,) * nd, pipeline_mode=pl.Buffered(1))


def _dft_tables():
    two_pi = 2.0 * math.pi
    n1 = jnp.arange(H1, dtype=jnp.int32)
    n2 = jnp.arange(N2, dtype=jnp.int32)
    k1 = jnp.arange(N1, dtype=jnp.int32)
    ang_a = ((n1[None, :] * k1[:, None]) % N1).astype(F32) * (two_pi / N1)
    ang_t = (n2[:, None] * k1[None, :]).astype(F32) * (two_pi / N_FFT)
    ar, ai = jnp.cos(ang_a)[None], -jnp.sin(ang_a)[None]
    tr, ti = jnp.cos(ang_t)[:, :, None], -jnp.sin(ang_t)[:, :, None]
    er, ei = ar * tr - ai * ti, ar * ti + ai * tr
    f1 = jnp.concatenate([jnp.concatenate([er, -ei], axis=2),
                          jnp.concatenate([ei, er], axis=2)], axis=1)
    ert, eit = jnp.swapaxes(er, 1, 2), jnp.swapaxes(ei, 1, 2)
    i2 = jnp.concatenate([ert, eit], axis=1)
    mg = (n2[:, None] * n2[None, :]) % N2
    angg = mg.astype(F32) * (two_pi / N2)
    gr, gi = jnp.cos(angg), -jnp.sin(angg)
    f2 = jnp.concatenate([jnp.concatenate([gr, -gi], axis=1),
                          jnp.concatenate([gi, gr], axis=1)], axis=0)
    f2 = f2.reshape(2, N2 // SUBLANES, SUBLANES, 2 * N2).transpose(1, 0, 2, 3).reshape(2 * N2, 2 * N2)
    i1 = jnp.concatenate([jnp.concatenate([gr, gi], axis=1),
                          jnp.concatenate([-gi, gr], axis=1)], axis=0)
    return f1.astype(BF16), f2.astype(BF16), i1.astype(BF16), i2.astype(BF16)


def _rope_tables():
    half = ROT_DIM // 2
    pos = jnp.arange(SEQ, dtype=F32)
    inv = 1.0 / (ROPE_THETA ** (jnp.arange(0, ROT_DIM, 2, dtype=F32) / ROT_DIM))
    ang = pos[:, None] * inv[None, :]
    cos, sin = jnp.cos(ang), jnp.sin(ang)
    ones = jnp.ones((SEQ, HEAD_DIM - ROT_DIM), F32)
    zeros = jnp.zeros((SEQ, HEAD_DIM - ROT_DIM), F32)
    zh = jnp.zeros((SEQ, half), F32)
    c = jnp.concatenate([cos, cos, ones], axis=1)
    s_up = jnp.concatenate([-sin, zh, zeros], axis=1)
    s_dn = jnp.concatenate([zh, sin, zeros], axis=1)
    rep = LANES // HEAD_DIM
    return jnp.tile(c, (1, rep)), jnp.tile(s_up, (1, rep)), jnp.tile(s_dn, (1, rep))


def _filter_embedding():
    bands = (FILTER_EMB - 1) // 2
    t = jnp.linspace(0.0, 1.0, SEQ, dtype=F32)[:, None]
    w = 2.0 * math.pi * jnp.arange(SEQ, dtype=F32)[:, None] / SEQ
    f = jnp.linspace(1e-4, bands - 1, bands, dtype=F32)[None, :]
    z = jnp.concatenate([t, jnp.cos(f * w), jnp.sin(f * w)], axis=-1)
    z = jnp.pad(z, ((0, 0), (0, LANES - FILTER_EMB)))
    min_decay = math.log(DECAY_TARGET) / SLOW_DECAY_PCT
    max_decay = math.log(DECAY_TARGET) / FAST_DECAY_PCT
    deltas = jnp.tile(jnp.linspace(min_decay, max_decay, D_HYENA, dtype=F32), 2 * HYENA_ORDER)
    return z, jnp.abs(deltas)[None, :]


PROJ_TM = 1024
PROJ_HALO = 16
LOG2E = math.log2(math.e)
Q_SCALE = HEAD_DIM ** -0.5 * LOG2E


def _proj_kernel(x_ref, xp_ref, xn_ref, g_ref, w_ref, cw_ref, c_ref, su_ref, sd_ref, *refs):
    uh_ref, q_ref, k_ref, v_ref, gate_ref = refs[-5:]

    def norm(xf):
        return xf * lax.rsqrt(jnp.mean(xf * xf, axis=-1, keepdims=True) + EPS) * g_ref[...]

    i = pl.program_id(1)
    hn = norm(x_ref[0]).astype(BF16)
    h_prev = (norm(xp_ref[0]) * (i > 0).astype(F32)).astype(BF16)
    h_next = (norm(xn_ref[0]) * (i < pl.num_programs(1) - 1).astype(F32)).astype(BF16)
    hn_ext = jnp.concatenate([h_prev, hn, h_next], axis=0)

    def proj(c0, width):
        return jnp.dot(hn, w_ref[:, c0:c0 + width], preferred_element_type=F32)

    rows = PROJ_TM + 2 * PROJ_HALO
    for j in range(D_UH // 512):
        cols = slice(j * 512, (j + 1) * 512)
        u = jnp.dot(hn_ext, w_ref[:, cols], preferred_element_type=F32)
        prev = pltpu.roll(u, 1, axis=0)[PROJ_HALO:PROJ_HALO + PROJ_TM]
        nxt = pltpu.roll(u, rows - 1, axis=0)[PROJ_HALO:PROJ_HALO + PROJ_TM]
        cur = u[PROJ_HALO:PROJ_HALO + PROJ_TM]
        uc = (prev * cw_ref[0:1, cols] + cur * cw_ref[1:2, cols] + nxt * cw_ref[2:3, cols] + cw_ref[3:4, cols])
        for c in range(512 // LANES):
            uh_ref[0, j * (512 // LANES) + c] = uc[:, c * LANES:(c + 1) * LANES].astype(BF16)

    def rope(xc):
        return (xc * c_ref[...] + pltpu.roll(xc, LANES - ROT_DIM // 2, axis=1) * su_ref[...]
                + pltpu.roll(xc, ROT_DIM // 2, axis=1) * sd_ref[...])

    qkv = proj(D_UH, D_ATTN + 2 * D_KV)
    for j in range(D_ATTN // LANES):
        qc = qkv[:, j * LANES:(j + 1) * LANES]
        q_ref[0, :, j * LANES:(j + 1) * LANES] = (rope(qc) * Q_SCALE).astype(BF16)
    lo = lax.broadcasted_iota(jnp.int32, (PROJ_TM, LANES), 1) < HEAD_DIM
    for val, ref in ((rope(qkv[:, D_ATTN:D_ATTN + D_KV]), k_ref), (qkv[:, D_ATTN + D_KV:], v_ref)):
        rolled = pltpu.roll(val, HEAD_DIM, axis=1)
        ref[0, 0] = jnp.where(lo, val, 0.0).astype(BF16)
        ref[0, 1] = jnp.where(lo, 0.0, rolled).astype(BF16)
        ref[0, 2] = jnp.where(lo, rolled, 0.0).astype(BF16)
        ref[0, 3] = jnp.where(lo, 0.0, val).astype(BF16)
    g0 = D_UH + D_ATTN + 2 * D_KV
    for j in range(2 * D_MODEL // 512):
        gate_ref[0, :, j * 512:(j + 1) * 512] = jax.nn.sigmoid(proj(g0 + j * 512, 512)).astype(BF16)


def _proj_call(x, batch0, total, stacked, g1, w_in, conv_w, rope_c, rope_su, rope_sd):
    b = x.shape[0]
    nt = SEQ // PROJ_TM
    per_tile = PROJ_TM // PROJ_HALO
    tile = lambda w: pl.BlockSpec((1, PROJ_TM, w), lambda bi, i: (bi, i, 0))
    halo_prev = pl.BlockSpec((1, PROJ_HALO, D_MODEL), lambda bi, i: (bi, jnp.maximum(i * per_tile - 1, 0), 0))
    halo_next = pl.BlockSpec((1, PROJ_HALO, D_MODEL),
                             lambda bi, i: (bi, jnp.minimum((i + 1) * per_tile, SEQ // PROJ_HALO - 1), 0))
    rope_spec = pl.BlockSpec((PROJ_TM, LANES), lambda bi, i: (i, 0))
    out_tile = lambda w: pl.BlockSpec((1, PROJ_TM, w), lambda bi, i: (batch0 + bi, i, 0))
    out_slabs = lambda n: pl.BlockSpec((1, n, PROJ_TM, LANES), lambda bi, i: (batch0 + bi, 0, i, 0))
    in_specs = [tile(D_MODEL), halo_prev, halo_next, _const_spec((1, D_MODEL)), _const_spec((D_MODEL, D_IN)),
                _const_spec((8, D_UH)), rope_spec, rope_spec, rope_spec]
    operands = [x, x, x, g1, w_in, conv_w, rope_c, rope_su, rope_sd]
    aliases = {}
    if stacked is not None:
        aliases = {len(operands) + n: n for n in range(len(stacked))}
        in_specs += [pl.BlockSpec(memory_space=pl.ANY)] * len(stacked)
        operands += list(stacked)
    return pl.pallas_call(
        _proj_kernel,
        grid=(b, nt),
        in_specs=in_specs,
        out_specs=[out_slabs(D_UH // LANES), out_tile(D_ATTN), out_slabs(2 * N_KV_HEADS),
                   out_slabs(2 * N_KV_HEADS), out_tile(2 * D_MODEL)],
        out_shape=[jax.ShapeDtypeStruct((total, D_UH // LANES, SEQ, LANES), BF16),
                   jax.ShapeDtypeStruct((total, SEQ, D_ATTN), BF16),
                   jax.ShapeDtypeStruct((total, 2 * N_KV_HEADS, SEQ, LANES), BF16),
                   jax.ShapeDtypeStruct((total, 2 * N_KV_HEADS, SEQ, LANES), BF16),
                   jax.ShapeDtypeStruct((total, SEQ, 2 * D_MODEL), BF16)],
        input_output_aliases=aliases,
        compiler_params=_cparams(("parallel", "parallel")),
        name="proj",
    )(*operands)


FILT_TL = 512


def _filter_kernel(z_ref, w0_ref, b0_ref, wi_ref, bi_ref, wo_ref, fr_ref, dl_ref, ride_src, h_ref, ride_dst):
    _cast_rider(pl.program_id(0), pl.num_programs(0), [ride_src], [ride_dst])
    hi = lax.Precision.HIGHEST
    half = FILT_TL // 2
    z = z_ref[...]
    z2 = jnp.concatenate([z[:half], z[half:]], axis=1)
    fr = fr_ref[...]
    h = jnp.sin(fr * (jnp.dot(z2, w0_ref[...], precision=hi, preferred_element_type=F32) + b0_ref[...]))
    for i in range(wi_ref.shape[0]):
        h = jnp.sin(fr * (jnp.dot(h, wi_ref[i], precision=hi, preferred_element_type=F32) + bi_ref[i]))
    def split(v):
        top = v.astype(BF16)
        return top, (v - top.astype(F32)).astype(BF16)
    h_hi, h_lo = split(h)
    w_hi, w_lo = split(wo_ref[...])
    out = (jnp.dot(h_hi, w_hi, preferred_element_type=F32) + jnp.dot(h_hi, w_lo, preferred_element_type=F32)
           + jnp.dot(h_lo, w_hi, preferred_element_type=F32))
    for r in range(2):
        rows = slice(r * half, (r + 1) * half)
        t = z[rows, 0:1]
        h_ref[rows, :] = out[:, r * N_FILT_CH:(r + 1) * N_FILT_CH] * jnp.exp(-t * dl_ref[...])


def _block_diag2(w):
    zeros = jnp.zeros_like(w)
    return jnp.concatenate([jnp.concatenate([w, zeros], axis=-1), jnp.concatenate([zeros, w], axis=-1)], axis=-2)


def _filter_call(z, w0, b0, wi, bi, wo, fr, deltas, rider):
    n_inner = wi.shape[0]
    twice = lambda v: jnp.concatenate([v, v], axis=-1)
    w0, wi, wo = _block_diag2(w0), _block_diag2(wi), _block_diag2(wo)
    b0, bi, fr = twice(b0), twice(bi), twice(fr)
    wide = 2 * FILTER_ORDER
    steps = SEQ // FILT_TL
    ride_in, = _cast_rider_specs([rider], steps, lambda i: i)
    ride_out, = _cast_rider_specs([rider], steps, lambda i: i)
    return pl.pallas_call(
        _filter_kernel,
        grid=(steps,),
        in_specs=[pl.BlockSpec((FILT_TL, LANES), lambda i: (i, 0)),
                  _const_spec((2 * LANES, wide)), _const_spec((1, wide)),
                  _const_spec((n_inner, wide, wide)), _const_spec((n_inner, 1, wide)),
                  _const_spec((wide, 2 * N_FILT_CH)), _const_spec((1, wide)),
                  _const_spec((1, N_FILT_CH)), ride_in],
        out_specs=[pl.BlockSpec((FILT_TL, N_FILT_CH), lambda i: (i, 0)), ride_out],
        out_shape=[jax.ShapeDtypeStruct((SEQ, N_FILT_CH), F32), jax.ShapeDtypeStruct(rider.shape, BF16)],
        compiler_params=_cparams(("arbitrary",)),
        name="filt",
    )(z, w0, b0, wi, bi, wo, fr, deltas, rider)


STAGE_UNROLL = 64
KB = 2
MID_UNROLL = 64


def _stage_f1(z_ref, a_ref, f1_ref, real_only):
    def body(n2, carry):
        zr = z_ref[0, pl.ds(n2, H1, stride=Z_STRIDE), :]
        if real_only:
            res = jnp.dot(f1_ref[n2][:, :H1], zr.astype(BF16), preferred_element_type=F32)
        else:
            zi = z_ref[1, pl.ds(n2, H1, stride=Z_STRIDE), :]
            st = jnp.concatenate([zr, zi], axis=0).astype(BF16)
            res = jnp.dot(f1_ref[n2], st, preferred_element_type=F32)
        base = n2 * A_STRIDE
        a_ref[0, pl.ds(base, N1), :] = res[:N1]
        a_ref[1, pl.ds(base, N1), :] = res[N1:]
        return carry
    lax.fori_loop(0, N2, body, 0, unroll=STAGE_UNROLL)


def _k1_rows(k1):
    return pl.ds(k1 * K_STRIDE, N2)


def _dft_blocks(mat_ref, vr, vi, interleaved):
    st = jnp.concatenate([vr, vi], axis=0).astype(BF16)
    x = jnp.dot(mat_ref[...], st, preferred_element_type=F32)
    if not interleaved:
        return x[:N2], x[N2:]
    x4 = x.reshape(N2 // SUBLANES, 2, SUBLANES, x.shape[-1])
    return x4[:, 0].reshape(N2, x.shape[-1]), x4[:, 1].reshape(N2, x.shape[-1])


def _load_spectrum_blocks(a_ref, f2_ref, k1b):
    ar = jnp.concatenate([a_ref[0, pl.ds(k1b + i, N2, stride=A_STRIDE), :] for i in range(KB)], axis=1)
    ai = jnp.concatenate([a_ref[1, pl.ds(k1b + i, N2, stride=A_STRIDE), :] for i in range(KB)], axis=1)
    return _dft_blocks(f2_ref, ar, ai, True)


def _spectrum_rows(k1b, i):
    return pl.ds(pl.multiple_of((k1b + i) * N2, N2), N2)


def _kf_kernel(hf_ref, hb_ref, b_ref, f1_ref, f2_ref, kr_ref, ki_ref, z_ref, ab_ref, acc_ref):
    inv_n = 1.0 / N_FFT

    def load_time(src_ref, backward):
        def body(j, carry):
            rows = pl.multiple_of(j * N2, N2)
            v = src_ref[pl.ds(rows, N2), :]
            ridx = lax.broadcasted_iota(jnp.int32, (N2, CB), 0) + rows
            v = jnp.where(ridx == 0, 0.0 if backward else v + b_ref[...], v)
            z_ref[0, pl.ds(j * Z_STRIDE, N2), :] = v
            return carry
        lax.fori_loop(0, H1, body, 0, unroll=4)

    def spectrum(accumulate):
        def body(it, carry):
            k1b = it * KB
            xr, xi = _load_spectrum_blocks(ab_ref, f2_ref, k1b)
            for i in range(KB):
                rows = _spectrum_rows(k1b, i)
                lanes = slice(i * CB, (i + 1) * CB)
                if accumulate:
                    kr_ref[rows, :] = (acc_ref[0, rows, :] + xr[:, lanes] * inv_n).astype(BF16)
                    ki_ref[rows, :] = (acc_ref[1, rows, :] - xi[:, lanes] * inv_n).astype(BF16)
                else:
                    acc_ref[0, rows, :] = xr[:, lanes] * inv_n
                    acc_ref[1, rows, :] = xi[:, lanes] * inv_n
            return carry
        lax.fori_loop(0, N1 // KB, body, 0, unroll=MID_UNROLL)

    load_time(hf_ref, False)
    _stage_f1(z_ref, ab_ref, f1_ref, True)
    spectrum(False)
    load_time(hb_ref, True)
    _stage_f1(z_ref, ab_ref, f1_ref, True)
    spectrum(True)


def _kf_call(h, bias, f1, f2):
    blocks_per_order = N_CB
    bias_spec = pl.BlockSpec((None, 1, CB), lambda o, c: (o, 0, c))
    fwd = pl.BlockSpec((SEQ, CB), lambda o, c: (0, o * blocks_per_order + c))
    bwd = pl.BlockSpec((SEQ, CB), lambda o, c: (0, (HYENA_ORDER + o) * blocks_per_order + c))
    out = pl.BlockSpec((None, N_FFT, CB), lambda o, c: (o, 0, c))
    return pl.pallas_call(
        _kf_kernel,
        grid=(HYENA_ORDER, N_CB),
        in_specs=[fwd, bwd, bias_spec, _const_spec((N2, 2 * N1, 2 * H1)), _const_spec((2 * N2, 2 * N2))],
        out_specs=[out, out],
        out_shape=[jax.ShapeDtypeStruct((HYENA_ORDER, N_FFT, D_HYENA), BF16)] * 2,
        scratch_shapes=[pltpu.VMEM((2, H1 * Z_STRIDE, CB), F32), pltpu.VMEM((2, N2 * A_STRIDE, CB), F32),
                        pltpu.VMEM((2, N_FFT, CB), F32)],
        compiler_params=_cparams(("parallel", "parallel")),
        name="kf",
    )(h, h, bias, f1, f2)


TIME_ROWS = 256
GROUPS = TIME_ROWS // N2


def _group_rows(j, i):
    return pl.ds((j * GROUPS + i) * Z_STRIDE, N2)


def _hyena_kernel(a_ref, g_ref, kr_ref, ki_ref, f1_ref, f2_ref, i1_ref, i2_ref, o_ref,
                  z_ref, s1_ref, s2_ref):
    def fill(j, carry):
        rows = pl.ds(pl.multiple_of(j * TIME_ROWS, TIME_ROWS), TIME_ROWS)
        for r in range(2):
            val = a_ref[r, rows, :].astype(F32)
            for i in range(GROUPS):
                z_ref[r, _group_rows(j, i), :] = val[i * N2:(i + 1) * N2]
        return carry
    lax.fori_loop(0, SEQ // TIME_ROWS, fill, 0, unroll=2)

    _stage_f1(z_ref, s1_ref, f1_ref, False)

    def store_blocks(k1b, vr, vi):
        for i in range(KB):
            lanes = slice(i * CB, (i + 1) * CB)
            s2_ref[0, _k1_rows(k1b + i), :] = vr[:, lanes]
            s2_ref[1, _k1_rows(k1b + i), :] = vi[:, lanes]

    def mid_fwd(it, carry):
        k1b = it * KB
        xr, xi = _load_spectrum_blocks(s1_ref, f2_ref, k1b)
        kr = jnp.concatenate([kr_ref[_spectrum_rows(k1b, i), :] for i in range(KB)], axis=1).astype(F32)
        ki = jnp.concatenate([ki_ref[_spectrum_rows(k1b, i), :] for i in range(KB)], axis=1).astype(F32)
        store_blocks(k1b, xr * kr - xi * ki, xr * ki + xi * kr)
        return carry
    lax.fori_loop(0, N1 // KB, mid_fwd, 0, unroll=MID_UNROLL)

    def mid_inv(it, carry):
        k1b = it * KB
        yr = jnp.concatenate([s2_ref[0, _k1_rows(k1b + i), :] for i in range(KB)], axis=1)
        yi = jnp.concatenate([s2_ref[1, _k1_rows(k1b + i), :] for i in range(KB)], axis=1)
        br, bi = _dft_blocks(i1_ref, yr, yi, False)
        store_blocks(k1b, br, bi)
        return carry
    lax.fori_loop(0, N1 // KB, mid_inv, 0, unroll=MID_UNROLL)

    y_ref = s1_ref

    def last(n2, carry):
        st = jnp.concatenate([s2_ref[0, pl.ds(n2, N1, stride=K_STRIDE), :],
                              s2_ref[1, pl.ds(n2, N1, stride=K_STRIDE), :]], axis=1).astype(BF16)
        p = jnp.dot(i2_ref[n2], st, preferred_element_type=F32)
        y_ref[0, pl.ds(n2, H1, stride=Z_STRIDE), :] = p[:H1, :CB] + p[H1:, CB:]
        y_ref[1, pl.ds(n2, H1, stride=Z_STRIDE), :] = p[:H1, CB:] - p[H1:, :CB]
        return carry
    lax.fori_loop(0, N2, last, 0, unroll=STAGE_UNROLL)

    def finish(j, carry):
        rows = pl.ds(pl.multiple_of(j * TIME_ROWS, TIME_ROWS), TIME_ROWS)
        for r in range(2):
            y = jnp.concatenate([y_ref[r, _group_rows(j, i), :] for i in range(GROUPS)], axis=0)
            o_ref[r, rows, :] = (g_ref[r, rows, :].astype(F32) * y).astype(BF16)
        return carry
    lax.fori_loop(0, SEQ // TIME_ROWS, finish, 0, unroll=2)


def _hyena_call(a, a_slot, g, g_slot, kr, ki, order, tables):
    b = a.shape[0]
    f1, f2, i1, i2 = tables
    io = lambda slot: pl.BlockSpec((2, None, SEQ, CB), lambda c, p: (p, slot + c, 0, 0))
    kspec = pl.BlockSpec((None, N_FFT, CB), lambda c, p: (order, 0, c))
    assert N2 * A_STRIDE >= H1 * Z_STRIDE
    return pl.pallas_call(
        _hyena_kernel,
        grid=(N_CB, b // 2),
        in_specs=[io(a_slot), io(g_slot), kspec, kspec,
                  _const_spec((N2, 2 * N1, 2 * H1)), _const_spec((2 * N2, 2 * N2)),
                  _const_spec((2 * N2, 2 * N2)), _const_spec((N2, 2 * H1, N1))],
        out_specs=io(0),
        out_shape=jax.ShapeDtypeStruct((b, N_CB, SEQ, CB), BF16),
        scratch_shapes=[pltpu.VMEM((2, H1 * Z_STRIDE, CB), F32), pltpu.VMEM((2, N2 * A_STRIDE, CB), F32),
                        pltpu.VMEM((2, N1 * K_STRIDE, CB), F32)],
        compiler_params=_cparams(("parallel", "parallel")),
        name=f"hyena{order}",
    )(a, g, kr, ki, f1, f2, i1, i2)


ATT_TQ = 128
ATT_KW = ATT_TQ + 2 * WINDOW


ATT_QB = 8
ATT_RIDER_CHUNKS = 16


def _attn_kernel(sink_ref, q_ref, k_ref, v_ref, *refs):
    n_riders = (len(refs) - 1) // 2
    o_ref = refs[n_riders]
    _cast_rider(pl.program_id(0) * pl.num_programs(1) + pl.program_id(1), ATT_RIDER_CHUNKS,
                refs[:n_riders], refs[n_riders + 1:])
    lo_q = lax.broadcasted_iota(jnp.int32, (ATT_TQ, LANES), 1) < HEAD_DIM
    for qb in range(ATT_QB):
        i = pl.program_id(1) * ATT_QB + qb
        start = pl.multiple_of(jnp.clip(i * ATT_TQ - WINDOW, 0, SEQ - ATT_KW), LANES)
        win = pl.ds(start, ATT_KW)
        qrows = slice(qb * ATT_TQ, (qb + 1) * ATT_TQ)
        qpos = i * ATT_TQ + lax.broadcasted_iota(jnp.int32, (ATT_TQ, ATT_KW), 0)
        kpos = start + lax.broadcasted_iota(jnp.int32, (ATT_TQ, ATT_KW), 1)
        bias = jnp.where(jnp.abs(qpos - kpos) <= WINDOW, 0.0, NEG_INF).astype(BF16)
        pairs = [(h, j) for h in range(N_KV_HEADS) for j in range(2)]
        scores = []
        for h, j in pairs:
            kbd = jnp.concatenate([k_ref[0, 2 * h, win, :], k_ref[0, 2 * h + 1, win, :]], axis=0)
            q2 = q_ref[0, qrows, LANES * (2 * h + j):LANES * (2 * h + j + 1)]
            scores.append(lax.dot_general(q2, kbd, (((1,), (1,)), ((), ())), preferred_element_type=F32))
        probs, scales = [], []
        for (h, j), s in zip(pairs, scores):
            ps, inv = [], []
            for e in range(2):
                sk = sink_ref[4 * h + 2 * j + e] * LOG2E
                se = s[:, e * ATT_KW:(e + 1) * ATT_KW].astype(BF16) + bias
                m = jnp.maximum(jnp.max(se, axis=-1, keepdims=True).astype(F32), sk)
                p = jnp.exp2(se - m.astype(BF16))
                part = functools.reduce(lambda a, b: a + b,
                                        [p[:, c * LANES:(c + 1) * LANES] for c in range(ATT_KW // LANES)])
                inv.append(1.0 / (jnp.sum(part.astype(F32), axis=-1, keepdims=True) + jnp.exp2(sk - m)))
                ps.append(p)
            probs.append(jnp.concatenate(ps, axis=1))
            scales.append(jnp.where(lo_q, inv[0], inv[1]))
        for (h, j), p2, sc in zip(pairs, probs, scales):
            vbd = jnp.concatenate([v_ref[0, 2 * h, win, :], v_ref[0, 2 * h + 1, win, :]], axis=0)
            o2 = jnp.dot(p2, vbd, preferred_element_type=F32)
            c0 = LANES * (2 * h + j)
            o_ref[0, qrows, c0:c0 + LANES] = (o2 * sc).astype(BF16)


def _attn_call(sink, q, k, v, riders):
    b = q.shape[0]
    nt = SEQ // (ATT_TQ * ATT_QB)
    assert not riders or b * nt >= ATT_RIDER_CHUNKS
    full = pl.BlockSpec((1, 2 * N_KV_HEADS, SEQ, LANES), lambda bi, i: (bi, 0, 0, 0))
    tile = pl.BlockSpec((1, ATT_TQ * ATT_QB, D_ATTN), lambda bi, i: (bi, i, 0))
    flat = lambda bi, i: bi * nt + i
    out = pl.pallas_call(
        _attn_kernel,
        grid=(b, nt),
        in_specs=[pl.BlockSpec(memory_space=pltpu.SMEM), tile, full, full,
                  *_cast_rider_specs(riders, ATT_RIDER_CHUNKS, flat)],
        out_specs=[tile, *_cast_rider_specs(riders, ATT_RIDER_CHUNKS, flat)],
        out_shape=[jax.ShapeDtypeStruct((b, SEQ, D_ATTN), BF16),
                   *[jax.ShapeDtypeStruct(w.shape, BF16) for w in riders]],
        compiler_params=_cparams(("arbitrary", "arbitrary")),
        name="attn",
    )(sink, q, k, v, *riders)
    return out[0], out[1:]


MIX_TM = 512
MIX_SPLIT = 2
MXU_COLS = 256
FF_SPLITS = (0, 6 * MXU_COLS, D_FF)


def _rms(x, g):
    return x * lax.rsqrt(jnp.mean(x * x, axis=-1, keepdims=True) + EPS) * g


def _mix_kernel(x_ref, yh_ref, ya_ref, gate_ref, wuh_ref, wua_ref, wo_ref, g2_ref, wg_ref, wu_ref, wd_ref,
                gf_ref, o_ref):
    groups = [slice(r * (MIX_TM // MIX_SPLIT), (r + 1) * (MIX_TM // MIX_SPLIT)) for r in range(MIX_SPLIT)]
    ups = []
    for rows in groups:
        yh = jnp.concatenate([yh_ref[0, c, rows, :] for c in range(N_CB)], axis=1)
        ups.append((jnp.dot(yh, wuh_ref[...], preferred_element_type=F32),
                    jnp.dot(ya_ref[0, rows, :], wua_ref[...], preferred_element_type=F32)))
    merged = [(gate_ref[0, rows, :D_MODEL].astype(F32) * up_h
               + gate_ref[0, rows, D_MODEL:].astype(F32) * up_a).astype(BF16)
              for rows, (up_h, up_a) in zip(groups, ups)]
    x1s = [x_ref[0, rows, :] + jnp.dot(m, wo_ref[...], preferred_element_type=F32)
           for rows, m in zip(groups, merged)]
    hns = [_rms(x1, g2_ref[...]).astype(BF16) for x1 in x1s]
    accs = list(x1s)
    for c in range(len(FF_SPLITS) - 1):
        cols = slice(FF_SPLITS[c], FF_SPLITS[c + 1])
        gates = [jnp.dot(hn, wg_ref[:, cols], preferred_element_type=F32) for hn in hns]
        ups = [jnp.dot(hn, wu_ref[:, cols], preferred_element_type=F32) for hn in hns]
        acts = [(jax.nn.silu(g) * u).astype(BF16) for g, u in zip(gates, ups)]
        accs = [acc + jnp.dot(act, wd_ref[cols, :], preferred_element_type=F32) for acc, act in zip(accs, acts)]
    for rows, acc in zip(groups, accs):
        o_ref[0, rows, :] = _rms(acc, gf_ref[...])


def _mix_call(x, batch0, yh, ya, gates, wuh, wua, wo, g2, wg, wu, wd, gf):
    b = x.shape[0]
    tile = lambda w: pl.BlockSpec((1, MIX_TM, w), lambda bi, i: (bi, i, 0))
    stacked_tile = lambda w: pl.BlockSpec((1, MIX_TM, w), lambda bi, i: (batch0 + bi, i, 0))
    return pl.pallas_call(
        _mix_kernel,
        grid=(b, SEQ // MIX_TM),
        in_specs=[tile(D_MODEL), pl.BlockSpec((1, N_CB, MIX_TM, CB), lambda bi, i: (batch0 + bi, 0, i, 0)),
                  stacked_tile(D_ATTN), stacked_tile(2 * D_MODEL),
                  _const_spec((D_HYENA, D_MODEL)), _const_spec((D_ATTN, D_MODEL)),
                  _const_spec((D_MODEL, D_MODEL)), _const_spec((1, D_MODEL)),
                  _const_spec((D_MODEL, D_FF)), _const_spec((D_MODEL, D_FF)), _const_spec((D_FF, D_MODEL)),
                  _const_spec((1, D_MODEL))],
        out_specs=tile(D_MODEL),
        out_shape=jax.ShapeDtypeStruct((b, SEQ, D_MODEL), F32),
        compiler_params=_cparams(("parallel", "parallel")),
        name="mix",
    )(x, yh, ya, gates, wuh, wua, wo, g2, wg, wu, wd, gf)


def _layers(xs, tables, rope, kf, norm1_g, w_in, short_w, short_b, sink_logit, w_up_hyena,
            w_up_attn, w_out, norm2_g, w_ff_gate, w_ff_up, w_ff_down, final_g):
    kr, ki = kf
    total = sum(x.shape[0] for x in xs)
    conv_w = jnp.concatenate([short_w, short_b[None, :], jnp.zeros((4, D_UH), F32)], axis=0)
    stacked, offsets, off = None, [], 0
    for x in xs:
        stacked = _proj_call(x, off, total, stacked, norm1_g[None, :], w_in, conv_w, *rope)
        offsets.append(off)
        off += x.shape[0]
    uh, q, k, v, gates = stacked
    blocks = D_HYENA // LANES
    z1 = _hyena_call(uh, 0, uh, blocks, kr, ki, 0, tables)
    yh = _hyena_call(z1, 0, uh, 2 * blocks, kr, ki, 1, tables)
    ya, (wuh, wua, wo, wg, wu, wd) = _attn_call(sink_logit, q, k, v,
                                                (w_up_hyena, w_up_attn, w_out, w_ff_gate, w_ff_up, w_ff_down))
    weights = (wuh, wua, wo, norm2_g[None, :], wg, wu, wd, final_g[None, :])
    return tuple(_mix_call(x, o, yh, ya, gates, *weights) for x, o in zip(xs, offsets))


def kernel(x_prompt, x_sample, norm1_g, w_in, short_w, short_b, filt_w0, filt_b0, filt_w_inner, filt_b_inner,
           filt_w_out, filt_freq, hyena_bias, sink_logit, w_up_hyena, w_up_attn, w_out, norm2_g, w_ff_gate,
           w_ff_up, w_ff_down, final_g):
    tables = _dft_tables()
    rope = _rope_tables()
    z_emb, deltas = _filter_embedding()
    w0 = jnp.pad(filt_w0[0], ((0, LANES - FILTER_EMB), (0, 0)))
    h, w_in_bf16 = _filter_call(z_emb, w0, filt_b0[0][None, :], filt_w_inner[0], filt_b_inner[0][:, None, :],
                                filt_w_out[0], filt_freq[0][None, :], deltas, w_in[0])
    kf = _kf_call(h, hyena_bias[0][:, None, :], tables[0], tables[1])
    args = (norm1_g[0], w_in_bf16, short_w[0], short_b[0], sink_logit[0], w_up_hyena[0],
            w_up_attn[0], w_out[0], norm2_g[0], w_ff_gate[0], w_ff_up[0], w_ff_down[0], final_g)
    y_prompt, y_sample = _layers((x_prompt, x_sample), tables, rope, kf, *args)
    return (y_prompt, y_sample)
```

```python
import functools
import math

import jax
import jax.numpy as jnp
from jax import lax
from jax.experimental import pallas as pl
from jax.experimental.pallas import tpu as pltpu

F32 = jnp.float32
BF16 = jnp.bfloat16

D_MODEL = 1024
SEQ = 4096
D_HYENA = 512
HYENA_ORDER = 2
FILTER_EMB = 33
FILTER_ORDER = 64
N_FILT_CH = 2 * HYENA_ORDER * D_HYENA
FAST_DECAY_PCT = 0.3
SLOW_DECAY_PCT = 1.5
DECAY_TARGET = 1e-2
N_Q_HEADS = 8
N_KV_HEADS = 2
HEAD_DIM = 64
D_ATTN = N_Q_HEADS * HEAD_DIM
D_KV = N_KV_HEADS * HEAD_DIM
WINDOW = 128
ROPE_THETA = 500000.0
ROT_DIM = HEAD_DIM // 4
D_UH = (HYENA_ORDER + 1) * D_HYENA
D_IN = D_UH + D_ATTN + 2 * D_KV + 2 * D_MODEL
D_FF = 2816
EPS = 1e-6
NEG_INF = -1e30

LANES = 128
SUBLANES = 8
BF16_ROWS = 2 * SUBLANES
N_FFT = 2 * SEQ
N1 = 128
N2 = N_FFT // N1
H1 = N1 // 2
Z_STRIDE = N2 + 1
A_STRIDE = N1 + 1
K_STRIDE = N2 + 1
CB = LANES
N_CB = D_HYENA // CB
VMEM_LIMIT = 56 * 1024 * 1024


def _cparams(sem):
    return pltpu.CompilerParams(dimension_semantics=sem, vmem_limit_bytes=VMEM_LIMIT)


def _cast_rider_specs(weights, n_chunks, flat_step):
    specs = []
    for w in weights:
        rows = w.shape[0] // n_chunks
        assert rows * n_chunks == w.shape[0] and rows % BF16_ROWS == 0
        specs.append(pl.BlockSpec((rows, w.shape[1]),
                                  lambda *g: (jnp.minimum(flat_step(*g), n_chunks - 1), 0)))
    return specs


def _cast_rider(step, n_chunks, src_refs, dst_refs):
    @pl.when(step < n_chunks)
    def _():
        for src, dst in zip(src_refs, dst_refs):
            dst[...] = src[...].astype(BF16)


def _const_spec(shape):
    nd = len(shape)
    return pl.BlockSpec(shape, lambda *_: (0,) * nd, pipeline_mode=pl.Buffered(1))


def _dft_tables():
    two_pi = 2.0 * math.pi
    n1 = jnp.arange(H1, dtype=jnp.int32)
    n2 = jnp.arange(N2, dtype=jnp.int32)
    k1 = jnp.arange(N1, dtype=jnp.int32)
    ang_a = ((n1[None, :] * k1[:, None]) % N1).astype(F32) * (two_pi / N1)
    ang_t = (n2[:, None] * k1[None, :]).astype(F32) * (two_pi / N_FFT)
    ar, ai = jnp.cos(ang_a)[None], -jnp.sin(ang_a)[None]
    tr, ti = jnp.cos(ang_t)[:, :, None], -jnp.sin(ang_t)[:, :, None]
    er, ei = ar * tr - ai * ti, ar * ti + ai * tr
    f1 = jnp.concatenate([jnp.concatenate([er, -ei], axis=2),
                          jnp.concatenate([ei, er], axis=2)], axis=1)
    ert, eit = jnp.swapaxes(er, 1, 2), jnp.swapaxes(ei, 1, 2)
    i2 = jnp.concatenate([ert, eit], axis=1)
    mg = (n2[:, None] * n2[None, :]) % N2
    angg = mg.astype(F32) * (two_pi / N2)
    gr, gi = jnp.cos(angg), -jnp.sin(angg)
    f2 = jnp.concatenate([jnp.concatenate([gr, -gi], axis=1),
                          jnp.concatenate([gi, gr], axis=1)], axis=0)
    f2 = f2.reshape(2, N2 // SUBLANES, SUBLANES, 2 * N2).transpose(1, 0, 2, 3).reshape(2 * N2, 2 * N2)
    i1 = jnp.concatenate([jnp.concatenate([gr, gi], axis=1),
                          jnp.concatenate([-gi, gr], axis=1)], axis=0)
    return f1.astype(BF16), f2.astype(BF16), i1.astype(BF16), i2.astype(BF16)


def _rope_tables():
    half = ROT_DIM // 2
    pos = jnp.arange(SEQ, dtype=F32)
    inv = 1.0 / (ROPE_THETA ** (jnp.arange(0, ROT_DIM, 2, dtype=F32) / ROT_DIM))
    ang = pos[:, None] * inv[None, :]
    cos, sin = jnp.cos(ang), jnp.sin(ang)
    ones = jnp.ones((SEQ, HEAD_DIM - ROT_DIM), F32)
    zeros = jnp.zeros((SEQ, HEAD_DIM - ROT_DIM), F32)
    zh = jnp.zeros((SEQ, half), F32)
    c = jnp.concatenate([cos, cos, ones], axis=1)
    s_up = jnp.concatenate([-sin, zh, zeros], axis=1)
    s_dn = jnp.concatenate([zh, sin, zeros], axis=1)
    rep = LANES // HEAD_DIM
    return jnp.tile(c, (1, rep)), jnp.tile(s_up, (1, rep)), jnp.tile(s_dn, (1, rep))


def _filter_embedding():
    bands = (FILTER_EMB - 1) // 2
    t = jnp.linspace(0.0, 1.0, SEQ, dtype=F32)[:, None]
    w = 2.0 * math.pi * jnp.arange(SEQ, dtype=F32)[:, None] / SEQ
    f = jnp.linspace(1e-4, bands - 1, bands, dtype=F32)[None, :]
    z = jnp.concatenate([t, jnp.cos(f * w), jnp.sin(f * w)], axis=-1)
    z = jnp.pad(z, ((0, 0), (0, LANES - FILTER_EMB)))
    min_decay = math.log(DECAY_TARGET) / SLOW_DECAY_PCT
    max_decay = math.log(DECAY_TARGET) / FAST_DECAY_PCT
    deltas = jnp.tile(jnp.linspace(min_decay, max_decay, D_HYENA, dtype=F32), 2 * HYENA_ORDER)
    return z, jnp.abs(deltas)[None, :]


PROJ_TM = 1024
PROJ_COLS = 512
PROJ_HALO = BF16_ROWS
LOG2E = math.log2(math.e)
Q_SCALE = HEAD_DIM ** -0.5 * LOG2E


def _proj_kernel(x_ref, xp_ref, xn_ref, g_ref, w_ref, cw_ref, c_ref, su_ref, sd_ref, *refs):
    uh_ref, q_ref, k_ref, v_ref, gate_ref = refs[-5:]

    def norm(xf):
        return xf * lax.rsqrt(jnp.mean(xf * xf, axis=-1, keepdims=True) + EPS) * g_ref[...]

    i = pl.program_id(1)
    hn = norm(x_ref[0]).astype(BF16)
    h_prev = (norm(xp_ref[0]) * (i > 0).astype(F32)).astype(BF16)
    h_next = (norm(xn_ref[0]) * (i < pl.num_programs(1) - 1).astype(F32)).astype(BF16)
    hn_ext = jnp.concatenate([h_prev, hn, h_next], axis=0)

    def proj(c0, width):
        return jnp.dot(hn, w_ref[:, c0:c0 + width], preferred_element_type=F32)

    rows = PROJ_TM + 2 * PROJ_HALO
    for j in range(D_UH // PROJ_COLS):
        cols = slice(j * PROJ_COLS, (j + 1) * PROJ_COLS)
        u = jnp.dot(hn_ext, w_ref[:, cols], preferred_element_type=F32)
        prev = pltpu.roll(u, 1, axis=0)[PROJ_HALO:PROJ_HALO + PROJ_TM]
        nxt = pltpu.roll(u, rows - 1, axis=0)[PROJ_HALO:PROJ_HALO + PROJ_TM]
        cur = u[PROJ_HALO:PROJ_HALO + PROJ_TM]
        uc = (prev * cw_ref[0:1, cols] + cur * cw_ref[1:2, cols] + nxt * cw_ref[2:3, cols] + cw_ref[3:4, cols])
        for c in range(PROJ_COLS // LANES):
            uh_ref[0, j * (PROJ_COLS // LANES) + c] = uc[:, c * LANES:(c + 1) * LANES].astype(BF16)

    def rope(xc):
        return (xc * c_ref[...] + pltpu.roll(xc, LANES - ROT_DIM // 2, axis=1) * su_ref[...]
                + pltpu.roll(xc, ROT_DIM // 2, axis=1) * sd_ref[...])

    qkv = proj(D_UH, D_ATTN + 2 * D_KV)
    for j in range(D_ATTN // LANES):
        qc = qkv[:, j * LANES:(j + 1) * LANES]
        q_ref[0, :, j * LANES:(j + 1) * LANES] = (rope(qc) * Q_SCALE).astype(BF16)
    lo = lax.broadcasted_iota(jnp.int32, (PROJ_TM, LANES), 1) < HEAD_DIM
    for val, ref in ((rope(qkv[:, D_ATTN:D_ATTN + D_KV]), k_ref), (qkv[:, D_ATTN + D_KV:], v_ref)):
        rolled = pltpu.roll(val, HEAD_DIM, axis=1)
        ref[0, 0] = jnp.where(lo, val, 0.0).astype(BF16)
        ref[0, 1] = jnp.where(lo, 0.0, rolled).astype(BF16)
        ref[0, 2] = jnp.where(lo, rolled, 0.0).astype(BF16)
        ref[0, 3] = jnp.where(lo, 0.0, val).astype(BF16)
    g0 = D_UH + D_ATTN + 2 * D_KV
    for j in range(2 * D_MODEL // PROJ_COLS):
        cols = slice(j * PROJ_COLS, (j + 1) * PROJ_COLS)
        gate_ref[0, :, cols] = jax.nn.sigmoid(proj(g0 + j * PROJ_COLS, PROJ_COLS)).astype(BF16)


def _proj_call(x, batch0, total, stacked, g1, w_in, conv_w, rope_c, rope_su, rope_sd):
    b = x.shape[0]
    nt = SEQ // PROJ_TM
    per_tile = PROJ_TM // PROJ_HALO
    tile = lambda w: pl.BlockSpec((1, PROJ_TM, w), lambda bi, i: (bi, i, 0))
    halo_prev = pl.BlockSpec((1, PROJ_HALO, D_MODEL), lambda bi, i: (bi, jnp.maximum(i * per_tile - 1, 0), 0))
    halo_next = pl.BlockSpec((1, PROJ_HALO, D_MODEL),
                             lambda bi, i: (bi, jnp.minimum((i + 1) * per_tile, SEQ // PROJ_HALO - 1), 0))
    rope_spec = pl.BlockSpec((PROJ_TM, LANES), lambda bi, i: (i, 0))
    out_tile = lambda w: pl.BlockSpec((1, PROJ_TM, w), lambda bi, i: (batch0 + bi, i, 0))
    out_slabs = lambda n: pl.BlockSpec((1, n, PROJ_TM, LANES), lambda bi, i: (batch0 + bi, 0, i, 0))
    in_specs = [tile(D_MODEL), halo_prev, halo_next, _const_spec((1, D_MODEL)), _const_spec((D_MODEL, D_IN)),
                _const_spec((8, D_UH)), rope_spec, rope_spec, rope_spec]
    operands = [x, x, x, g1, w_in, conv_w, rope_c, rope_su, rope_sd]
    aliases = {}
    if stacked is not None:
        aliases = {len(operands) + n: n for n in range(len(stacked))}
        in_specs += [pl.BlockSpec(memory_space=pl.ANY)] * len(stacked)
        operands += list(stacked)
    return pl.pallas_call(
        _proj_kernel,
        grid=(b, nt),
        in_specs=in_specs,
        out_specs=[out_slabs(D_UH // LANES), out_tile(D_ATTN), out_slabs(2 * N_KV_HEADS),
                   out_slabs(2 * N_KV_HEADS), out_tile(2 * D_MODEL)],
        out_shape=[jax.ShapeDtypeStruct((total, D_UH // LANES, SEQ, LANES), BF16),
                   jax.ShapeDtypeStruct((total, SEQ, D_ATTN), BF16),
                   jax.ShapeDtypeStruct((total, 2 * N_KV_HEADS, SEQ, LANES), BF16),
                   jax.ShapeDtypeStruct((total, 2 * N_KV_HEADS, SEQ, LANES), BF16),
                   jax.ShapeDtypeStruct((total, SEQ, 2 * D_MODEL), BF16)],
        input_output_aliases=aliases,
        compiler_params=_cparams(("parallel", "parallel")),
        name="proj",
    )(*operands)


FILT_TL = 512


def _filter_kernel(z_ref, w0_ref, b0_ref, wi_ref, bi_ref, wo_ref, fr_ref, dl_ref, ride_src, h_ref, ride_dst):
    _cast_rider(pl.program_id(0), pl.num_programs(0), [ride_src], [ride_dst])
    hi = lax.Precision.HIGHEST
    half = FILT_TL // 2
    z = z_ref[...]
    z2 = jnp.concatenate([z[:half], z[half:]], axis=1)
    fr = fr_ref[...]
    h = jnp.sin(fr * (jnp.dot(z2, w0_ref[...], precision=hi, preferred_element_type=F32) + b0_ref[...]))
    for i in range(wi_ref.shape[0]):
        h = jnp.sin(fr * (jnp.dot(h, wi_ref[i], precision=hi, preferred_element_type=F32) + bi_ref[i]))
    def split(v):
        top = v.astype(BF16)
        return top, (v - top.astype(F32)).astype(BF16)
    h_hi, h_lo = split(h)
    w_hi, w_lo = split(wo_ref[...])
    out = (jnp.dot(h_hi, w_hi, preferred_element_type=F32) + jnp.dot(h_hi, w_lo, preferred_element_type=F32)
           + jnp.dot(h_lo, w_hi, preferred_element_type=F32))
    for r in range(2):
        rows = slice(r * half, (r + 1) * half)
        t = z[rows, 0:1]
        h_ref[rows, :] = out[:, r * N_FILT_CH:(r + 1) * N_FILT_CH] * jnp.exp(-t * dl_ref[...])


def _block_diag2(w):
    zeros = jnp.zeros_like(w)
    return jnp.concatenate([jnp.concatenate([w, zeros], axis=-1), jnp.concatenate([zeros, w], axis=-1)], axis=-2)


def _filter_call(z, w0, b0, wi, bi, wo, fr, deltas, rider):
    n_inner = wi.shape[0]
    twice = lambda v: jnp.concatenate([v, v], axis=-1)
    w0, wi, wo = _block_diag2(w0), _block_diag2(wi), _block_diag2(wo)
    b0, bi, fr = twice(b0), twice(bi), twice(fr)
    wide = 2 * FILTER_ORDER
    steps = SEQ // FILT_TL
    ride_in, = _cast_rider_specs([rider], steps, lambda i: i)
    ride_out, = _cast_rider_specs([rider], steps, lambda i: i)
    return pl.pallas_call(
        _filter_kernel,
        grid=(steps,),
        in_specs=[pl.BlockSpec((FILT_TL, LANES), lambda i: (i, 0)),
                  _const_spec((2 * LANES, wide)), _const_spec((1, wide)),
                  _const_spec((n_inner, wide, wide)), _const_spec((n_inner, 1, wide)),
                  _const_spec((wide, 2 * N_FILT_CH)), _const_spec((1, wide)),
                  _const_spec((1, N_FILT_CH)), ride_in],
        out_specs=[pl.BlockSpec((FILT_TL, N_FILT_CH), lambda i: (i, 0)), ride_out],
        out_shape=[jax.ShapeDtypeStruct((SEQ, N_FILT_CH), F32), jax.ShapeDtypeStruct(rider.shape, BF16)],
        compiler_params=_cparams(("arbitrary",)),
        name="filt",
    )(z, w0, b0, wi, bi, wo, fr, deltas, rider)


STAGE_UNROLL = 64
KB = 2
MID_UNROLL = 64


def _stage_f1(z_ref, a_ref, f1_ref, real_only):
    def body(n2, carry):
        zr = z_ref[0, pl.ds(n2, H1, stride=Z_STRIDE), :]
        if real_only:
            res = jnp.dot(f1_ref[n2][:, :H1], zr.astype(BF16), preferred_element_type=F32)
        else:
            zi = z_ref[1, pl.ds(n2, H1, stride=Z_STRIDE), :]
            st = jnp.concatenate([zr, zi], axis=0).astype(BF16)
            res = jnp.dot(f1_ref[n2], st, preferred_element_type=F32)
        base = n2 * A_STRIDE
        a_ref[0, pl.ds(base, N1), :] = res[:N1]
        a_ref[1, pl.ds(base, N1), :] = res[N1:]
        return carry
    lax.fori_loop(0, N2, body, 0, unroll=STAGE_UNROLL)


def _k1_rows(k1):
    return pl.ds(k1 * K_STRIDE, N2)


def _dft_blocks(mat_ref, vr, vi, interleaved):
    st = jnp.concatenate([vr, vi], axis=0).astype(BF16)
    x = jnp.dot(mat_ref[...], st, preferred_element_type=F32)
    if not interleaved:
        return x[:N2], x[N2:]
    x4 = x.reshape(N2 // SUBLANES, 2, SUBLANES, x.shape[-1])
    return x4[:, 0].reshape(N2, x.shape[-1]), x4[:, 1].reshape(N2, x.shape[-1])


def _load_spectrum_blocks(a_ref, f2_ref, k1b):
    ar = jnp.concatenate([a_ref[0, pl.ds(k1b + i, N2, stride=A_STRIDE), :] for i in range(KB)], axis=1)
    ai = jnp.concatenate([a_ref[1, pl.ds(k1b + i, N2, stride=A_STRIDE), :] for i in range(KB)], axis=1)
    return _dft_blocks(f2_ref, ar, ai, True)


def _spectrum_rows(k1b, i):
    return pl.ds(pl.multiple_of((k1b + i) * N2, N2), N2)


def _kf_kernel(hf_ref, hb_ref, b_ref, f1_ref, f2_ref, kr_ref, ki_ref, z_ref, ab_ref, acc_ref):
    inv_n = 1.0 / N_FFT

    def load_time(src_ref, backward):
        def body(j, carry):
            rows = pl.multiple_of(j * N2, N2)
            v = src_ref[pl.ds(rows, N2), :]
            ridx = lax.broadcasted_iota(jnp.int32, (N2, CB), 0) + rows
            v = jnp.where(ridx == 0, 0.0 if backward else v + b_ref[...], v)
            z_ref[0, pl.ds(j * Z_STRIDE, N2), :] = v
            return carry
        lax.fori_loop(0, H1, body, 0, unroll=4)

    def spectrum(accumulate):
        def body(it, carry):
            k1b = it * KB
            xr, xi = _load_spectrum_blocks(ab_ref, f2_ref, k1b)
            for i in range(KB):
                rows = _spectrum_rows(k1b, i)
                lanes = slice(i * CB, (i + 1) * CB)
                if accumulate:
                    kr_ref[rows, :] = (acc_ref[0, rows, :] + xr[:, lanes] * inv_n).astype(BF16)
                    ki_ref[rows, :] = (acc_ref[1, rows, :] - xi[:, lanes] * inv_n).astype(BF16)
                else:
                    acc_ref[0, rows, :] = xr[:, lanes] * inv_n
                    acc_ref[1, rows, :] = xi[:, lanes] * inv_n
            return carry
        lax.fori_loop(0, N1 // KB, body, 0, unroll=MID_UNROLL)

    load_time(hf_ref, False)
    _stage_f1(z_ref, ab_ref, f1_ref, True)
    spectrum(False)
    load_time(hb_ref, True)
    _stage_f1(z_ref, ab_ref, f1_ref, True)
    spectrum(True)


def _kf_call(h, bias, f1, f2):
    blocks_per_order = N_CB
    bias_spec = pl.BlockSpec((None, 1, CB), lambda o, c: (o, 0, c))
    fwd = pl.BlockSpec((SEQ, CB), lambda o, c: (0, o * blocks_per_order + c))
    bwd = pl.BlockSpec((SEQ, CB), lambda o, c: (0, (HYENA_ORDER + o) * blocks_per_order + c))
    out = pl.BlockSpec((None, N_FFT, CB), lambda o, c: (o, 0, c))
    return pl.pallas_call(
        _kf_kernel,
        grid=(HYENA_ORDER, N_CB),
        in_specs=[fwd, bwd, bias_spec, _const_spec((N2, 2 * N1, 2 * H1)), _const_spec((2 * N2, 2 * N2))],
        out_specs=[out, out],
        out_shape=[jax.ShapeDtypeStruct((HYENA_ORDER, N_FFT, D_HYENA), BF16)] * 2,
        scratch_shapes=[pltpu.VMEM((2, H1 * Z_STRIDE, CB), F32), pltpu.VMEM((2, N2 * A_STRIDE, CB), F32),
                        pltpu.VMEM((2, N_FFT, CB), F32)],
        compiler_params=_cparams(("parallel", "parallel")),
        name="kf",
    )(h, h, bias, f1, f2)


TIME_ROWS = 256
GROUPS = TIME_ROWS // N2


def _group_rows(j, i):
    return pl.ds((j * GROUPS + i) * Z_STRIDE, N2)


def _hyena_kernel(a_ref, g_ref, kr_ref, ki_ref, f1_ref, f2_ref, i1_ref, i2_ref, o_ref,
                  z_ref, s1_ref, s2_ref):
    def fill(j, carry):
        rows = pl.ds(pl.multiple_of(j * TIME_ROWS, TIME_ROWS), TIME_ROWS)
        for r in range(2):
            val = a_ref[r, rows, :].astype(F32)
            for i in range(GROUPS):
                z_ref[r, _group_rows(j, i), :] = val[i * N2:(i + 1) * N2]
        return carry
    lax.fori_loop(0, SEQ // TIME_ROWS, fill, 0, unroll=2)

    _stage_f1(z_ref, s1_ref, f1_ref, False)

    def store_blocks(k1b, vr, vi):
        for i in range(KB):
            lanes = slice(i * CB, (i + 1) * CB)
            s2_ref[0, _k1_rows(k1b + i), :] = vr[:, lanes]
            s2_ref[1, _k1_rows(k1b + i), :] = vi[:, lanes]

    def mid_fwd(it, carry):
        k1b = it * KB
        xr, xi = _load_spectrum_blocks(s1_ref, f2_ref, k1b)
        kr = jnp.concatenate([kr_ref[_spectrum_rows(k1b, i), :] for i in range(KB)], axis=1).astype(F32)
        ki = jnp.concatenate([ki_ref[_spectrum_rows(k1b, i), :] for i in range(KB)], axis=1).astype(F32)
        store_blocks(k1b, xr * kr - xi * ki, xr * ki + xi * kr)
        return carry
    lax.fori_loop(0, N1 // KB, mid_fwd, 0, unroll=MID_UNROLL)

    def mid_inv(it, carry):
        k1b = it * KB
        yr = jnp.concatenate([s2_ref[0, _k1_rows(k1b + i), :] for i in range(KB)], axis=1)
        yi = jnp.concatenate([s2_ref[1, _k1_rows(k1b + i), :] for i in range(KB)], axis=1)
        br, bi = _dft_blocks(i1_ref, yr, yi, False)
        store_blocks(k1b, br, bi)
        return carry
    lax.fori_loop(0, N1 // KB, mid_inv, 0, unroll=MID_UNROLL)

    y_ref = s1_ref

    def last(n2, carry):
        st = jnp.concatenate([s2_ref[0, pl.ds(n2, N1, stride=K_STRIDE), :],
                              s2_ref[1, pl.ds(n2, N1, stride=K_STRIDE), :]], axis=1).astype(BF16)
        p = jnp.dot(i2_ref[n2], st, preferred_element_type=F32)
        y_ref[0, pl.ds(n2, H1, stride=Z_STRIDE), :] = p[:H1, :CB] + p[H1:, CB:]
        y_ref[1, pl.ds(n2, H1, stride=Z_STRIDE), :] = p[:H1, CB:] - p[H1:, :CB]
        return carry
    lax.fori_loop(0, N2, last, 0, unroll=STAGE_UNROLL)

    def finish(j, carry):
        rows = pl.ds(pl.multiple_of(j * TIME_ROWS, TIME_ROWS), TIME_ROWS)
        for r in range(2):
            y = jnp.concatenate([y_ref[r, _group_rows(j, i), :] for i in range(GROUPS)], axis=0)
            o_ref[r, rows, :] = (g_ref[r, rows, :].astype(F32) * y).astype(BF16)
        return carry
    lax.fori_loop(0, SEQ // TIME_ROWS, finish, 0, unroll=2)


def _hyena_call(a, a_slot, g, g_slot, kr, ki, order, tables):
    b = a.shape[0]
    f1, f2, i1, i2 = tables
    io = lambda slot: pl.BlockSpec((2, None, SEQ, CB), lambda c, p: (p, slot + c, 0, 0))
    kspec = pl.BlockSpec((None, N_FFT, CB), lambda c, p: (order, 0, c))
    assert N2 * A_STRIDE >= H1 * Z_STRIDE
    return pl.pallas_call(
        _hyena_kernel,
        grid=(N_CB, b // 2),
        in_specs=[io(a_slot), io(g_slot), kspec, kspec,
                  _const_spec((N2, 2 * N1, 2 * H1)), _const_spec((2 * N2, 2 * N2)),
                  _const_spec((2 * N2, 2 * N2)), _const_spec((N2, 2 * H1, N1))],
        out_specs=io(0),
        out_shape=jax.ShapeDtypeStruct((b, N_CB, SEQ, CB), BF16),
        scratch_shapes=[pltpu.VMEM((2, H1 * Z_STRIDE, CB), F32), pltpu.VMEM((2, N2 * A_STRIDE, CB), F32),
                        pltpu.VMEM((2, N1 * K_STRIDE, CB), F32)],
        compiler_params=_cparams(("parallel", "parallel")),
        name=f"hyena{order}",
    )(a, g, kr, ki, f1, f2, i1, i2)


ATT_TQ = 128
ATT_KW = ATT_TQ + 2 * WINDOW


ATT_QB = 16
ATT_RIDER_CHUNKS = 16


def _attn_kernel(sink_ref, q_ref, k_ref, v_ref, *refs):
    n_riders = (len(refs) - 1) // 2
    o_ref = refs[n_riders]
    _cast_rider(pl.program_id(0) * pl.num_programs(1) + pl.program_id(1), ATT_RIDER_CHUNKS,
                refs[:n_riders], refs[n_riders + 1:])
    lo_q = lax.broadcasted_iota(jnp.int32, (ATT_TQ, LANES), 1) < HEAD_DIM
    for qb in range(ATT_QB):
        i = pl.program_id(1) * ATT_QB + qb
        start = pl.multiple_of(jnp.clip(i * ATT_TQ - WINDOW, 0, SEQ - ATT_KW), LANES)
        win = pl.ds(start, ATT_KW)
        qrows = slice(qb * ATT_TQ, (qb + 1) * ATT_TQ)
        qpos = i * ATT_TQ + lax.broadcasted_iota(jnp.int32, (ATT_TQ, ATT_KW), 0)
        kpos = start + lax.broadcasted_iota(jnp.int32, (ATT_TQ, ATT_KW), 1)
        bias = jnp.where(jnp.abs(qpos - kpos) <= WINDOW, 0.0, NEG_INF).astype(BF16)
        pairs = [(h, j) for h in range(N_KV_HEADS) for j in range(2)]
        scores = []
        for h, j in pairs:
            kbd = jnp.concatenate([k_ref[0, 2 * h, win, :], k_ref[0, 2 * h + 1, win, :]], axis=0)
            q2 = q_ref[0, qrows, LANES * (2 * h + j):LANES * (2 * h + j + 1)]
            scores.append(lax.dot_general(q2, kbd, (((1,), (1,)), ((), ())), preferred_element_type=F32))
        probs, scales = [], []
        for (h, j), s in zip(pairs, scores):
            ps, inv = [], []
            for e in range(2):
                sk = sink_ref[4 * h + 2 * j + e] * LOG2E
                se = s[:, e * ATT_KW:(e + 1) * ATT_KW].astype(BF16) + bias
                m = jnp.maximum(jnp.max(se, axis=-1, keepdims=True).astype(F32), sk)
                p = jnp.exp2(se - m.astype(BF16))
                part = functools.reduce(lambda a, b: a + b,
                                        [p[:, c * LANES:(c + 1) * LANES] for c in range(ATT_KW // LANES)])
                inv.append(1.0 / (jnp.sum(part.astype(F32), axis=-1, keepdims=True) + jnp.exp2(sk - m)))
                ps.append(p)
            probs.append(jnp.concatenate(ps, axis=1))
            scales.append(jnp.where(lo_q, inv[0], inv[1]))
        for (h, j), p2, sc in zip(pairs, probs, scales):
            vbd = jnp.concatenate([v_ref[0, 2 * h, win, :], v_ref[0, 2 * h + 1, win, :]], axis=0)
            o2 = jnp.dot(p2, vbd, preferred_element_type=F32)
            c0 = LANES * (2 * h + j)
            o_ref[0, qrows, c0:c0 + LANES] = (o2 * sc).astype(BF16)


def _attn_call(sink, q, k, v, riders):
    b = q.shape[0]
    nt = SEQ // (ATT_TQ * ATT_QB)
    assert not riders or b * nt >= ATT_RIDER_CHUNKS
    full = pl.BlockSpec((1, 2 * N_KV_HEADS, SEQ, LANES), lambda bi, i: (bi, 0, 0, 0))
    tile = pl.BlockSpec((1, ATT_TQ * ATT_QB, D_ATTN), lambda bi, i: (bi, i, 0))
    flat = lambda bi, i: bi * nt + i
    out = pl.pallas_call(
        _attn_kernel,
        grid=(b, nt),
        in_specs=[pl.BlockSpec(memory_space=pltpu.SMEM), tile, full, full,
                  *_cast_rider_specs(riders, ATT_RIDER_CHUNKS, flat)],
        out_specs=[tile, *_cast_rider_specs(riders, ATT_RIDER_CHUNKS, flat)],
        out_shape=[jax.ShapeDtypeStruct((b, SEQ, D_ATTN), BF16),
                   *[jax.ShapeDtypeStruct(w.shape, BF16) for w in riders]],
        compiler_params=_cparams(("arbitrary", "arbitrary")),
        name="attn",
    )(sink, q, k, v, *riders)
    return out[0], out[1:]


MIX_TM = 512
MIX_SPLIT = 2
MXU_COLS = 256
FF_SPLITS = (0, 6 * MXU_COLS, D_FF)


def _rms(x, g):
    return x * lax.rsqrt(jnp.mean(x * x, axis=-1, keepdims=True) + EPS) * g


def _mix_kernel(x_ref, yh_ref, ya_ref, gate_ref, wuh_ref, wua_ref, wo_ref, g2_ref, wg_ref, wu_ref, wd_ref,
                gf_ref, o_ref):
    groups = [slice(r * (MIX_TM // MIX_SPLIT), (r + 1) * (MIX_TM // MIX_SPLIT)) for r in range(MIX_SPLIT)]
    ups = []
    for rows in groups:
        yh = jnp.concatenate([yh_ref[0, c, rows, :] for c in range(N_CB)], axis=1)
        ups.append((jnp.dot(yh, wuh_ref[...], preferred_element_type=F32),
                    jnp.dot(ya_ref[0, rows, :], wua_ref[...], preferred_element_type=F32)))
    merged = [(gate_ref[0, rows, :D_MODEL].astype(F32) * up_h
               + gate_ref[0, rows, D_MODEL:].astype(F32) * up_a).astype(BF16)
              for rows, (up_h, up_a) in zip(groups, ups)]
    x1s = [x_ref[0, rows, :] + jnp.dot(m, wo_ref[...], preferred_element_type=F32)
           for rows, m in zip(groups, merged)]
    hns = [_rms(x1, g2_ref[...]).astype(BF16) for x1 in x1s]
    accs = list(x1s)
    for c in range(len(FF_SPLITS) - 1):
        cols = slice(FF_SPLITS[c], FF_SPLITS[c + 1])
        gates = [jnp.dot(hn, wg_ref[:, cols], preferred_element_type=F32) for hn in hns]
        ups = [jnp.dot(hn, wu_ref[:, cols], preferred_element_type=F32) for hn in hns]
        acts = [(jax.nn.silu(g) * u).astype(BF16) for g, u in zip(gates, ups)]
        accs = [acc + jnp.dot(act, wd_ref[cols, :], preferred_element_type=F32) for acc, act in zip(accs, acts)]
    for rows, acc in zip(groups, accs):
        o_ref[0, rows, :] = _rms(acc, gf_ref[...])


def _mix_call(x, batch0, yh, ya, gates, wuh, wua, wo, g2, wg, wu, wd, gf):
    b = x.shape[0]
    tile = lambda w: pl.BlockSpec((1, MIX_TM, w), lambda bi, i: (bi, i, 0))
    stacked_tile = lambda w: pl.BlockSpec((1, MIX_TM, w), lambda bi, i: (batch0 + bi, i, 0))
    return pl.pallas_call(
        _mix_kernel,
        grid=(b, SEQ // MIX_TM),
        in_specs=[tile(D_MODEL), pl.BlockSpec((1, N_CB, MIX_TM, CB), lambda bi, i: (batch0 + bi, 0, i, 0)),
                  stacked_tile(D_ATTN), stacked_tile(2 * D_MODEL),
                  _const_spec((D_HYENA, D_MODEL)), _const_spec((D_ATTN, D_MODEL)),
                  _const_spec((D_MODEL, D_MODEL)), _const_spec((1, D_MODEL)),
                  _const_spec((D_MODEL, D_FF)), _const_spec((D_MODEL, D_FF)), _const_spec((D_FF, D_MODEL)),
                  _const_spec((1, D_MODEL))],
        out_specs=tile(D_MODEL),
        out_shape=jax.ShapeDtypeStruct((b, SEQ, D_MODEL), F32),
        compiler_params=_cparams(("parallel", "parallel")),
        name="mix",
    )(x, yh, ya, gates, wuh, wua, wo, g2, wg, wu, wd, gf)


def _layers(xs, tables, rope, kf, norm1_g, w_in, short_w, short_b, sink_logit, w_up_hyena,
            w_up_attn, w_out, norm2_g, w_ff_gate, w_ff_up, w_ff_down, final_g):
    kr, ki = kf
    total = sum(x.shape[0] for x in xs)
    conv_w = jnp.concatenate([short_w, short_b[None, :], jnp.zeros((4, D_UH), F32)], axis=0)
    stacked, offsets, off = None, [], 0
    for x in xs:
        stacked = _proj_call(x, off, total, stacked, norm1_g[None, :], w_in, conv_w, *rope)
        offsets.append(off)
        off += x.shape[0]
    uh, q, k, v, gates = stacked
    blocks = D_HYENA // LANES
    z1 = _hyena_call(uh, 0, uh, blocks, kr, ki, 0, tables)
    yh = _hyena_call(z1, 0, uh, 2 * blocks, kr, ki, 1, tables)
    ya, (wuh, wua, wo, wg, wu, wd) = _attn_call(sink_logit, q, k, v,
                                                (w_up_hyena, w_up_attn, w_out, w_ff_gate, w_ff_up, w_ff_down))
    weights = (wuh, wua, wo, norm2_g[None, :], wg, wu, wd, final_g[None, :])
    return tuple(_mix_call(x, o, yh, ya, gates, *weights) for x, o in zip(xs, offsets))


def kernel(x_prompt, x_sample, norm1_g, w_in, short_w, short_b, filt_w0, filt_b0, filt_w_inner, filt_b_inner,
           filt_w_out, filt_freq, hyena_bias, sink_logit, w_up_hyena, w_up_attn, w_out, norm2_g, w_ff_gate,
           w_ff_up, w_ff_down, final_g):
    tables = _dft_tables()
    rope = _rope_tables()
    z_emb, deltas = _filter_embedding()
    w0 = jnp.pad(filt_w0[0], ((0, LANES - FILTER_EMB), (0, 0)))
    h, w_in_bf16 = _filter_call(z_emb, w0, filt_b0[0][None, :], filt_w_inner[0], filt_b_inner[0][:, None, :],
                                filt_w_out[0], filt_freq[0][None, :], deltas, w_in[0])
    kf = _kf_call(h, hyena_bias[0][:, None, :], tables[0], tables[1])
    args = (norm1_g[0], w_in_bf16, short_w[0], short_b[0], sink_logit[0], w_up_hyena[0],
            w_up_attn[0], w_out[0], norm2_g[0], w_ff_gate[0], w_ff_up[0], w_ff_down[0], final_g)
    y_prompt, y_sample = _layers((x_prompt, x_sample), tables, rope, kf, *args)
    return (y_prompt, y_sample)
```

```python
import functools
import math

import jax
import jax.numpy as jnp
from jax import lax
from jax.experimental import pallas as pl
from jax.experimental.pallas import tpu as pltpu

F32 = jnp.float32
BF16 = jnp.bfloat16

D_MODEL = 1024
SEQ = 4096
D_HYENA = 512
HYENA_ORDER = 2
FILTER_EMB = 33
FILTER_ORDER = 64
N_FILT_CH = 2 * HYENA_ORDER * D_HYENA
FAST_DECAY_PCT = 0.3
SLOW_DECAY_PCT = 1.5
DECAY_TARGET = 1e-2
N_Q_HEADS = 8
N_KV_HEADS = 2
HEAD_DIM = 64
D_ATTN = N_Q_HEADS * HEAD_DIM
D_KV = N_KV_HEADS * HEAD_DIM
WINDOW = 128
ROPE_THETA = 500000.0
ROT_DIM = HEAD_DIM // 4
D_UH = (HYENA_ORDER + 1) * D_HYENA
D_IN = D_UH + D_ATTN + 2 * D_KV + 2 * D_MODEL
D_FF = 2816
EPS = 1e-6
NEG_INF = -1e30

LANES = 128
SUBLANES = 8
BF16_ROWS = 2 * SUBLANES
N_FFT = 2 * SEQ
N1 = 128
N2 = N_FFT // N1
H1 = N1 // 2
Z_STRIDE = N2 + 1
A_STRIDE = N1 + 1
K_STRIDE = N2 + 1
CB = LANES
N_CB = D_HYENA // CB
VMEM_LIMIT = 56 * 1024 * 1024


def _cparams(sem):
    return pltpu.CompilerParams(dimension_semantics=sem, vmem_limit_bytes=VMEM_LIMIT)


def _cast_rider_specs(weights, n_chunks, flat_step):
    specs = []
    for w in weights:
        rows = w.shape[0] // n_chunks
        assert rows * n_chunks == w.shape[0] and rows % BF16_ROWS == 0
        specs.append(pl.BlockSpec((rows, w.shape[1]),
                                  lambda *g: (jnp.minimum(flat_step(*g), n_chunks - 1), 0)))
    return specs


def _cast_rider(step, n_chunks, src_refs, dst_refs):
    @pl.when(step < n_chunks)
    def _():
        for src, dst in zip(src_refs, dst_refs):
            dst[...] = src[...].astype(BF16)


def _const_spec(shape):
    nd = len(shape)
    return pl.BlockSpec(shape, lambda *_: (0,) * nd, pipeline_mode=pl.Buffered(1))


def _dft_tables():
    two_pi = 2.0 * math.pi
    n1 = jnp.arange(H1, dtype=jnp.int32)
    n2 = jnp.arange(N2, dtype=jnp.int32)
    k1 = jnp.arange(N1, dtype=jnp.int32)
    ang_a = ((n1[None, :] * k1[:, None]) % N1).astype(F32) * (two_pi / N1)
    ang_t = (n2[:, None] * k1[None, :]).astype(F32) * (two_pi / N_FFT)
    ar, ai = jnp.cos(ang_a)[None], -jnp.sin(ang_a)[None]
    tr, ti = jnp.cos(ang_t)[:, :, None], -jnp.sin(ang_t)[:, :, None]
    er, ei = ar * tr - ai * ti, ar * ti + ai * tr
    f1 = jnp.concatenate([jnp.concatenate([er, -ei], axis=2),
                          jnp.concatenate([ei, er], axis=2)], axis=1)
    ert, eit = jnp.swapaxes(er, 1, 2), jnp.swapaxes(ei, 1, 2)
    i2 = jnp.concatenate([ert, eit], axis=1)
    mg = (n2[:, None] * n2[None, :]) % N2
    angg = mg.astype(F32) * (two_pi / N2)
    gr, gi = jnp.cos(angg), -jnp.sin(angg)
    f2 = jnp.concatenate([jnp.concatenate([gr, -gi], axis=1),
                          jnp.concatenate([gi, gr], axis=1)], axis=0)
    f2 = f2.reshape(2, N2 // SUBLANES, SUBLANES, 2 * N2).transpose(1, 0, 2, 3).reshape(2 * N2, 2 * N2)
    i1 = jnp.concatenate([jnp.concatenate([gr, gi], axis=1),
                          jnp.concatenate([-gi, gr], axis=1)], axis=0)
    return f1.astype(BF16), f2.astype(BF16), i1.astype(BF16), i2.astype(BF16)


def _rope_tables():
    half = ROT_DIM // 2
    pos = jnp.arange(SEQ, dtype=F32)
    inv = 1.0 / (ROPE_THETA ** (jnp.arange(0, ROT_DIM, 2, dtype=F32) / ROT_DIM))
    ang = pos[:, None] * inv[None, :]
    cos, sin = jnp.cos(ang), jnp.sin(ang)
    ones = jnp.ones((SEQ, HEAD_DIM - ROT_DIM), F32)
    zeros = jnp.zeros((SEQ, HEAD_DIM - ROT_DIM), F32)
    zh = jnp.zeros((SEQ, half), F32)
    c = jnp.concatenate([cos, cos, ones], axis=1)
    s_up = jnp.concatenate([-sin, zh, zeros], axis=1)
    s_dn = jnp.concatenate([zh, sin, zeros], axis=1)
    rep = LANES // HEAD_DIM
    return jnp.tile(c, (1, rep)), jnp.tile(s_up, (1, rep)), jnp.tile(s_dn, (1, rep))


def _filter_embedding():
    bands = (FILTER_EMB - 1) // 2
    t = jnp.linspace(0.0, 1.0, SEQ, dtype=F32)[:, None]
    w = 2.0 * math.pi * jnp.arange(SEQ, dtype=F32)[:, None] / SEQ
    f = jnp.linspace(1e-4, bands - 1, bands, dtype=F32)[None, :]
    z = jnp.concatenate([t, jnp.cos(f * w), jnp.sin(f * w)], axis=-1)
    z = jnp.pad(z, ((0, 0), (0, LANES - FILTER_EMB)))
    min_decay = math.log(DECAY_TARGET) / SLOW_DECAY_PCT
    max_decay = math.log(DECAY_TARGET) / FAST_DECAY_PCT
    deltas = jnp.tile(jnp.linspace(min_decay, max_decay, D_HYENA, dtype=F32), 2 * HYENA_ORDER)
    return z, jnp.abs(deltas)[None, :]


PROJ_TM = 1024
PROJ_COLS = 512
PROJ_HALO = BF16_ROWS
LOG2E = math.log2(math.e)
Q_SCALE = HEAD_DIM ** -0.5 * LOG2E


def _proj_kernel(x_ref, xp_ref, xn_ref, g_ref, w_ref, cw_ref, c_ref, su_ref, sd_ref, *refs):
    uh_ref, q_ref, k_ref, v_ref, gate_ref = refs[-5:]

    def norm(xf):
        return xf * lax.rsqrt(jnp.mean(xf * xf, axis=-1, keepdims=True) + EPS) * g_ref[...]

    i = pl.program_id(1)
    hn = norm(x_ref[0]).astype(BF16)
    h_prev = (norm(xp_ref[0]) * (i > 0).astype(F32)).astype(BF16)
    h_next = (norm(xn_ref[0]) * (i < pl.num_programs(1) - 1).astype(F32)).astype(BF16)
    hn_ext = jnp.concatenate([h_prev, hn, h_next], axis=0)

    def proj(c0, width):
        return jnp.dot(hn, w_ref[:, c0:c0 + width], preferred_element_type=F32)

    rows = PROJ_TM + 2 * PROJ_HALO
    for j in range(D_UH // PROJ_COLS):
        cols = slice(j * PROJ_COLS, (j + 1) * PROJ_COLS)
        u = jnp.dot(hn_ext, w_ref[:, cols], preferred_element_type=F32)
        prev = pltpu.roll(u, 1, axis=0)[PROJ_HALO:PROJ_HALO + PROJ_TM]
        nxt = pltpu.roll(u, rows - 1, axis=0)[PROJ_HALO:PROJ_HALO + PROJ_TM]
        cur = u[PROJ_HALO:PROJ_HALO + PROJ_TM]
        uc = (prev * cw_ref[0:1, cols] + cur * cw_ref[1:2, cols] + nxt * cw_ref[2:3, cols] + cw_ref[3:4, cols])
        for c in range(PROJ_COLS // LANES):
            uh_ref[0, j * (PROJ_COLS // LANES) + c] = uc[:, c * LANES:(c + 1) * LANES].astype(BF16)

    def rope(xc):
        return (xc * c_ref[...] + pltpu.roll(xc, LANES - ROT_DIM // 2, axis=1) * su_ref[...]
                + pltpu.roll(xc, ROT_DIM // 2, axis=1) * sd_ref[...])

    qkv = proj(D_UH, D_ATTN + 2 * D_KV)
    for j in range(D_ATTN // LANES):
        qc = qkv[:, j * LANES:(j + 1) * LANES]
        q_ref[0, :, j * LANES:(j + 1) * LANES] = (rope(qc) * Q_SCALE).astype(BF16)
    lo = lax.broadcasted_iota(jnp.int32, (PROJ_TM, LANES), 1) < HEAD_DIM
    for val, ref in ((rope(qkv[:, D_ATTN:D_ATTN + D_KV]), k_ref), (qkv[:, D_ATTN + D_KV:], v_ref)):
        rolled = pltpu.roll(val, HEAD_DIM, axis=1)
        ref[0, 0] = jnp.where(lo, val, 0.0).astype(BF16)
        ref[0, 1] = jnp.where(lo, 0.0, rolled).astype(BF16)
        ref[0, 2] = jnp.where(lo, rolled, 0.0).astype(BF16)
        ref[0, 3] = jnp.where(lo, 0.0, val).astype(BF16)
    g0 = D_UH + D_ATTN + 2 * D_KV
    for j in range(2 * D_MODEL // PROJ_COLS):
        cols = slice(j * PROJ_COLS, (j + 1) * PROJ_COLS)
        gate_ref[0, :, cols] = jax.nn.sigmoid(proj(g0 + j * PROJ_COLS, PROJ_COLS)).astype(BF16)


def _proj_call(x, batch0, total, stacked, g1, w_in, conv_w, rope_c, rope_su, rope_sd):
    b = x.shape[0]
    nt = SEQ // PROJ_TM
    per_tile = PROJ_TM // PROJ_HALO
    tile = lambda w: pl.BlockSpec((1, PROJ_TM, w), lambda bi, i: (bi, i, 0))
    halo_prev = pl.BlockSpec((1, PROJ_HALO, D_MODEL), lambda bi, i: (bi, jnp.maximum(i * per_tile - 1, 0), 0))
    halo_next = pl.BlockSpec((1, PROJ_HALO, D_MODEL),
                             lambda bi, i: (bi, jnp.minimum((i + 1) * per_tile, SEQ // PROJ_HALO - 1), 0))
    rope_spec = pl.BlockSpec((PROJ_TM, LANES), lambda bi, i: (i, 0))
    out_tile = lambda w: pl.BlockSpec((1, PROJ_TM, w), lambda bi, i: (batch0 + bi, i, 0))
    out_slabs = lambda n: pl.BlockSpec((1, n, PROJ_TM, LANES), lambda bi, i: (batch0 + bi, 0, i, 0))
    in_specs = [tile(D_MODEL), halo_prev, halo_next, _const_spec((1, D_MODEL)), _const_spec((D_MODEL, D_IN)),
                _const_spec((8, D_UH)), rope_spec, rope_spec, rope_spec]
    operands = [x, x, x, g1, w_in, conv_w, rope_c, rope_su, rope_sd]
    aliases = {}
    if stacked is not None:
        aliases = {len(operands) + n: n for n in range(len(stacked))}
        in_specs += [pl.BlockSpec(memory_space=pl.ANY)] * len(stacked)
        operands += list(stacked)
    return pl.pallas_call(
        _proj_kernel,
        grid=(b, nt),
        in_specs=in_specs,
        out_specs=[out_slabs(D_UH // LANES), out_tile(D_ATTN), out_slabs(2 * N_KV_HEADS),
                   out_slabs(2 * N_KV_HEADS), out_tile(2 * D_MODEL)],
        out_shape=[jax.ShapeDtypeStruct((total, D_UH // LANES, SEQ, LANES), BF16),
                   jax.ShapeDtypeStruct((total, SEQ, D_ATTN), BF16),
                   jax.ShapeDtypeStruct((total, 2 * N_KV_HEADS, SEQ, LANES), BF16),
                   jax.ShapeDtypeStruct((total, 2 * N_KV_HEADS, SEQ, LANES), BF16),
                   jax.ShapeDtypeStruct((total, SEQ, 2 * D_MODEL), BF16)],
        input_output_aliases=aliases,
        compiler_params=_cparams(("parallel", "parallel")),
        name="proj",
    )(*operands)


FILT_TL = 512


def _filter_kernel(z_ref, w0_ref, b0_ref, wi_ref, bi_ref, wo_ref, fr_ref, dl_ref, ride_src, h_ref, ride_dst):
    _cast_rider(pl.program_id(0), pl.num_programs(0), [ride_src], [ride_dst])
    hi = lax.Precision.HIGHEST
    half = FILT_TL // 2
    z = z_ref[...]
    z2 = jnp.concatenate([z[:half], z[half:]], axis=1)
    fr = fr_ref[...]
    h = jnp.sin(fr * (jnp.dot(z2, w0_ref[...], precision=hi, preferred_element_type=F32) + b0_ref[...]))
    for i in range(wi_ref.shape[0]):
        h = jnp.sin(fr * (jnp.dot(h, wi_ref[i], precision=hi, preferred_element_type=F32) + bi_ref[i]))
    def split(v):
        top = v.astype(BF16)
        return top, (v - top.astype(F32)).astype(BF16)
    h_hi, h_lo = split(h)
    w_hi, w_lo = split(wo_ref[...])
    out = (jnp.dot(h_hi, w_hi, preferred_element_type=F32) + jnp.dot(h_hi, w_lo, preferred_element_type=F32)
           + jnp.dot(h_lo, w_hi, preferred_element_type=F32))
    for r in range(2):
        rows = slice(r * half, (r + 1) * half)
        t = z[rows, 0:1]
        h_ref[rows, :] = out[:, r * N_FILT_CH:(r + 1) * N_FILT_CH] * jnp.exp(-t * dl_ref[...])


def _block_diag2(w):
    zeros = jnp.zeros_like(w)
    return jnp.concatenate([jnp.concatenate([w, zeros], axis=-1), jnp.concatenate([zeros, w], axis=-1)], axis=-2)


def _filter_call(z, w0, b0, wi, bi, wo, fr, deltas, rider):
    n_inner = wi.shape[0]
    twice = lambda v: jnp.concatenate([v, v], axis=-1)
    w0, wi, wo = _block_diag2(w0), _block_diag2(wi), _block_diag2(wo)
    b0, bi, fr = twice(b0), twice(bi), twice(fr)
    wide = 2 * FILTER_ORDER
    steps = SEQ // FILT_TL
    ride_in, = _cast_rider_specs([rider], steps, lambda i: i)
    ride_out, = _cast_rider_specs([rider], steps, lambda i: i)
    return pl.pallas_call(
        _filter_kernel,
        grid=(steps,),
        in_specs=[pl.BlockSpec((FILT_TL, LANES), lambda i: (i, 0)),
                  _const_spec((2 * LANES, wide)), _const_spec((1, wide)),
                  _const_spec((n_inner, wide, wide)), _const_spec((n_inner, 1, wide)),
                  _const_spec((wide, 2 * N_FILT_CH)), _const_spec((1, wide)),
                  _const_spec((1, N_FILT_CH)), ride_in],
        out_specs=[pl.BlockSpec((FILT_TL, N_FILT_CH), lambda i: (i, 0)), ride_out],
        out_shape=[jax.ShapeDtypeStruct((SEQ, N_FILT_CH), F32), jax.ShapeDtypeStruct(rider.shape, BF16)],
        compiler_params=_cparams(("arbitrary",)),
        name="filt",
    )(z, w0, b0, wi, bi, wo, fr, deltas, rider)


STAGE_UNROLL = 64
KB = 2
MID_UNROLL = 64


def _stage_f1(z_ref, a_ref, f1_ref, real_only):
    def body(n2, carry):
        zr = z_ref[0, pl.ds(n2, H1, stride=Z_STRIDE), :]
        if real_only:
            res = jnp.dot(f1_ref[n2][:, :H1], zr.astype(BF16), preferred_element_type=F32)
        else:
            zi = z_ref[1, pl.ds(n2, H1, stride=Z_STRIDE), :]
            st = jnp.concatenate([zr, zi], axis=0).astype(BF16)
            res = jnp.dot(f1_ref[n2], st, preferred_element_type=F32)
        base = n2 * A_STRIDE
        a_ref[0, pl.ds(base, N1), :] = res[:N1]
        a_ref[1, pl.ds(base, N1), :] = res[N1:]
        return carry
    lax.fori_loop(0, N2, body, 0, unroll=STAGE_UNROLL)


def _k1_rows(k1):
    return pl.ds(k1 * K_STRIDE, N2)


def _dft_blocks(mat_ref, vr, vi, interleaved):
    st = jnp.concatenate([vr, vi], axis=0).astype(BF16)
    x = jnp.dot(mat_ref[...], st, preferred_element_type=F32)
    if not interleaved:
        return x[:N2], x[N2:]
    x4 = x.reshape(N2 // SUBLANES, 2, SUBLANES, x.shape[-1])
    return x4[:, 0].reshape(N2, x.shape[-1]), x4[:, 1].reshape(N2, x.shape[-1])


def _load_spectrum_blocks(a_ref, f2_ref, k1b):
    ar = jnp.concatenate([a_ref[0, pl.ds(k1b + i, N2, stride=A_STRIDE), :] for i in range(KB)], axis=1)
    ai = jnp.concatenate([a_ref[1, pl.ds(k1b + i, N2, stride=A_STRIDE), :] for i in range(KB)], axis=1)
    return _dft_blocks(f2_ref, ar, ai, True)


def _spectrum_rows(k1b, i):
    return pl.ds(pl.multiple_of((k1b + i) * N2, N2), N2)


def _kf_kernel(hf_ref, hb_ref, b_ref, f1_ref, f2_ref, kr_ref, ki_ref, z_ref, ab_ref, acc_ref):
    inv_n = 1.0 / N_FFT

    def load_time(src_ref, backward):
        def body(j, carry):
            rows = pl.multiple_of(j * N2, N2)
            v = src_ref[pl.ds(rows, N2), :]
            ridx = lax.broadcasted_iota(jnp.int32, (N2, CB), 0) + rows
            v = jnp.where(ridx == 0, 0.0 if backward else v + b_ref[...], v)
            z_ref[0, pl.ds(j * Z_STRIDE, N2), :] = v
            return carry
        lax.fori_loop(0, H1, body, 0, unroll=4)

    def spectrum(accumulate):
        def body(it, carry):
            k1b = it * KB
            xr, xi = _load_spectrum_blocks(ab_ref, f2_ref, k1b)
            for i in range(KB):
                rows = _spectrum_rows(k1b, i)
                lanes = slice(i * CB, (i + 1) * CB)
                if accumulate:
                    kr_ref[rows, :] = (acc_ref[0, rows, :] + xr[:, lanes] * inv_n).astype(BF16)
                    ki_ref[rows, :] = (acc_ref[1, rows, :] - xi[:, lanes] * inv_n).astype(BF16)
                else:
                    acc_ref[0, rows, :] = xr[:, lanes] * inv_n
                    acc_ref[1, rows, :] = xi[:, lanes] * inv_n
            return carry
        lax.fori_loop(0, N1 // KB, body, 0, unroll=MID_UNROLL)

    load_time(hf_ref, False)
    _stage_f1(z_ref, ab_ref, f1_ref, True)
    spectrum(False)
    load_time(hb_ref, True)
    _stage_f1(z_ref, ab_ref, f1_ref, True)
    spectrum(True)


def _kf_call(h, bias, f1, f2):
    blocks_per_order = N_CB
    bias_spec = pl.BlockSpec((None, 1, CB), lambda o, c: (o, 0, c))
    fwd = pl.BlockSpec((SEQ, CB), lambda o, c: (0, o * blocks_per_order + c))
    bwd = pl.BlockSpec((SEQ, CB), lambda o, c: (0, (HYENA_ORDER + o) * blocks_per_order + c))
    out = pl.BlockSpec((None, N_FFT, CB), lambda o, c: (o, 0, c))
    return pl.pallas_call(
        _kf_kernel,
        grid=(HYENA_ORDER, N_CB),
        in_specs=[fwd, bwd, bias_spec, _const_spec((N2, 2 * N1, 2 * H1)), _const_spec((2 * N2, 2 * N2))],
        out_specs=[out, out],
        out_shape=[jax.ShapeDtypeStruct((HYENA_ORDER, N_FFT, D_HYENA), BF16)] * 2,
        scratch_shapes=[pltpu.VMEM((2, H1 * Z_STRIDE, CB), F32), pltpu.VMEM((2, N2 * A_STRIDE, CB), F32),
                        pltpu.VMEM((2, N_FFT, CB), F32)],
        compiler_params=_cparams(("parallel", "parallel")),
        name="kf",
    )(h, h, bias, f1, f2)


TIME_ROWS = 256
GROUPS = TIME_ROWS // N2


def _group_rows(j, i):
    return pl.ds((j * GROUPS + i) * Z_STRIDE, N2)


def _hyena_kernel(a_ref, g_ref, kr_ref, ki_ref, f1_ref, f2_ref, i1_ref, i2_ref, o_ref,
                  z_ref, s1_ref, s2_ref):
    def fill(j, carry):
        rows = pl.ds(pl.multiple_of(j * TIME_ROWS, TIME_ROWS), TIME_ROWS)
        for r in range(2):
            val = a_ref[r, rows, :].astype(F32)
            for i in range(GROUPS):
                z_ref[r, _group_rows(j, i), :] = val[i * N2:(i + 1) * N2]
        return carry
    lax.fori_loop(0, SEQ // TIME_ROWS, fill, 0, unroll=2)

    _stage_f1(z_ref, s1_ref, f1_ref, False)

    def store_blocks(k1b, vr, vi):
        for i in range(KB):
            lanes = slice(i * CB, (i + 1) * CB)
            s2_ref[0, _k1_rows(k1b + i), :] = vr[:, lanes]
            s2_ref[1, _k1_rows(k1b + i), :] = vi[:, lanes]

    def mid(it, carry):
        k1b = it * KB
        xr, xi = _load_spectrum_blocks(s1_ref, f2_ref, k1b)
        kr = jnp.concatenate([kr_ref[_spectrum_rows(k1b, i), :] for i in range(KB)], axis=1).astype(F32)
        ki = jnp.concatenate([ki_ref[_spectrum_rows(k1b, i), :] for i in range(KB)], axis=1).astype(F32)
        br, bi = _dft_blocks(i1_ref, xr * kr - xi * ki, xr * ki + xi * kr, False)
        store_blocks(k1b, br, bi)
        return carry
    lax.fori_loop(0, N1 // KB, mid, 0, unroll=MID_UNROLL)

    y_ref = s1_ref

    def last(n2, carry):
        st = jnp.concatenate([s2_ref[0, pl.ds(n2, N1, stride=K_STRIDE), :],
                              s2_ref[1, pl.ds(n2, N1, stride=K_STRIDE), :]], axis=1).astype(BF16)
        p = jnp.dot(i2_ref[n2], st, preferred_element_type=F32)
        y_ref[0, pl.ds(n2, H1, stride=Z_STRIDE), :] = p[:H1, :CB] + p[H1:, CB:]
        y_ref[1, pl.ds(n2, H1, stride=Z_STRIDE), :] = p[:H1, CB:] - p[H1:, :CB]
        return carry
    lax.fori_loop(0, N2, last, 0, unroll=STAGE_UNROLL)

    def finish(j, carry):
        rows = pl.ds(pl.multiple_of(j * TIME_ROWS, TIME_ROWS), TIME_ROWS)
        for r in range(2):
            y = jnp.concatenate([y_ref[r, _group_rows(j, i), :] for i in range(GROUPS)], axis=0)
            o_ref[r, rows, :] = (g_ref[r, rows, :].astype(F32) * y).astype(BF16)
        return carry
    lax.fori_loop(0, SEQ // TIME_ROWS, finish, 0, unroll=2)


def _hyena_call(a, a_slot, g, g_slot, kr, ki, order, tables):
    b = a.shape[0]
    f1, f2, i1, i2 = tables
    io = lambda slot: pl.BlockSpec((2, None, SEQ, CB), lambda c, p: (p, slot + c, 0, 0))
    kspec = pl.BlockSpec((None, N_FFT, CB), lambda c, p: (order, 0, c))
    assert N2 * A_STRIDE >= H1 * Z_STRIDE
    return pl.pallas_call(
        _hyena_kernel,
        grid=(N_CB, b // 2),
        in_specs=[io(a_slot), io(g_slot), kspec, kspec,
                  _const_spec((N2, 2 * N1, 2 * H1)), _const_spec((2 * N2, 2 * N2)),
                  _const_spec((2 * N2, 2 * N2)), _const_spec((N2, 2 * H1, N1))],
        out_specs=io(0),
        out_shape=jax.ShapeDtypeStruct((b, N_CB, SEQ, CB), BF16),
        scratch_shapes=[pltpu.VMEM((2, H1 * Z_STRIDE, CB), F32), pltpu.VMEM((2, N2 * A_STRIDE, CB), F32),
                        pltpu.VMEM((2, N1 * K_STRIDE, CB), F32)],
        compiler_params=_cparams(("parallel", "parallel")),
        name=f"hyena{order}",
    )(a, g, kr, ki, f1, f2, i1, i2)


ATT_TQ = 128
ATT_KW = ATT_TQ + 2 * WINDOW


ATT_QB = 16
ATT_RIDER_CHUNKS = 16


def _attn_kernel(sink_ref, q_ref, k_ref, v_ref, *refs):
    n_riders = (len(refs) - 1) // 2
    o_ref = refs[n_riders]
    _cast_rider(pl.program_id(0) * pl.num_programs(1) + pl.program_id(1), ATT_RIDER_CHUNKS,
                refs[:n_riders], refs[n_riders + 1:])
    lo_q = lax.broadcasted_iota(jnp.int32, (ATT_TQ, LANES), 1) < HEAD_DIM
    for qb in range(ATT_QB):
        i = pl.program_id(1) * ATT_QB + qb
        start = pl.multiple_of(jnp.clip(i * ATT_TQ - WINDOW, 0, SEQ - ATT_KW), LANES)
        win = pl.ds(start, ATT_KW)
        qrows = slice(qb * ATT_TQ, (qb + 1) * ATT_TQ)
        qpos = i * ATT_TQ + lax.broadcasted_iota(jnp.int32, (ATT_TQ, ATT_KW), 0)
        kpos = start + lax.broadcasted_iota(jnp.int32, (ATT_TQ, ATT_KW), 1)
        bias = jnp.where(jnp.abs(qpos - kpos) <= WINDOW, 0.0, NEG_INF).astype(BF16)
        pairs = [(h, j) for h in range(N_KV_HEADS) for j in range(2)]
        scores = []
        for h, j in pairs:
            kbd = jnp.concatenate([k_ref[0, 2 * h, win, :], k_ref[0, 2 * h + 1, win, :]], axis=0)
            q2 = q_ref[0, qrows, LANES * (2 * h + j):LANES * (2 * h + j + 1)]
            scores.append(lax.dot_general(q2, kbd, (((1,), (1,)), ((), ())), preferred_element_type=F32))
        probs, scales = [], []
        for (h, j), s in zip(pairs, scores):
            ps, inv = [], []
            for e in range(2):
                sk = sink_ref[4 * h + 2 * j + e] * LOG2E
                se = s[:, e * ATT_KW:(e + 1) * ATT_KW].astype(BF16) + bias
                m = jnp.maximum(jnp.max(se, axis=-1, keepdims=True).astype(F32), sk)
                p = jnp.exp2(se - m.astype(BF16))
                part = functools.reduce(lambda a, b: a + b,
                                        [p[:, c * LANES:(c + 1) * LANES] for c in range(ATT_KW // LANES)])
                inv.append(1.0 / (jnp.sum(part.astype(F32), axis=-1, keepdims=True) + jnp.exp2(sk - m)))
                ps.append(p)
            probs.append(jnp.concatenate(ps, axis=1))
            scales.append(jnp.where(lo_q, inv[0], inv[1]))
        for (h, j), p2, sc in zip(pairs, probs, scales):
            vbd = jnp.concatenate([v_ref[0, 2 * h, win, :], v_ref[0, 2 * h + 1, win, :]], axis=0)
            o2 = jnp.dot(p2, vbd, preferred_element_type=F32)
            c0 = LANES * (2 * h + j)
            o_ref[0, qrows, c0:c0 + LANES] = (o2 * sc).astype(BF16)


def _attn_call(sink, q, k, v, riders):
    b = q.shape[0]
    nt = SEQ // (ATT_TQ * ATT_QB)
    assert not riders or b * nt >= ATT_RIDER_CHUNKS
    full = pl.BlockSpec((1, 2 * N_KV_HEADS, SEQ, LANES), lambda bi, i: (bi, 0, 0, 0))
    tile = pl.BlockSpec((1, ATT_TQ * ATT_QB, D_ATTN), lambda bi, i: (bi, i, 0))
    flat = lambda bi, i: bi * nt + i
    out = pl.pallas_call(
        _attn_kernel,
        grid=(b, nt),
        in_specs=[pl.BlockSpec(memory_space=pltpu.SMEM), tile, full, full,
                  *_cast_rider_specs(riders, ATT_RIDER_CHUNKS, flat)],
        out_specs=[tile, *_cast_rider_specs(riders, ATT_RIDER_CHUNKS, flat)],
        out_shape=[jax.ShapeDtypeStruct((b, SEQ, D_ATTN), BF16),
                   *[jax.ShapeDtypeStruct(w.shape, BF16) for w in riders]],
        compiler_params=_cparams(("arbitrary", "arbitrary")),
        name="attn",
    )(sink, q, k, v, *riders)
    return out[0], out[1:]


MIX_TM = 512
MIX_SPLIT = 2
MXU_COLS = 256
FF_SPLITS = (0, 6 * MXU_COLS, D_FF)


def _rms(x, g):
    return x * lax.rsqrt(jnp.mean(x * x, axis=-1, keepdims=True) + EPS) * g


def _mix_kernel(x_ref, yh_ref, ya_ref, gate_ref, wuh_ref, wua_ref, wo_ref, g2_ref, wg_ref, wu_ref, wd_ref,
                gf_ref, o_ref):
    groups = [slice(r * (MIX_TM // MIX_SPLIT), (r + 1) * (MIX_TM // MIX_SPLIT)) for r in range(MIX_SPLIT)]
    ups = []
    for rows in groups:
        yh = jnp.concatenate([yh_ref[0, c, rows, :] for c in range(N_CB)], axis=1)
        ups.append((jnp.dot(yh, wuh_ref[...], preferred_element_type=F32),
                    jnp.dot(ya_ref[0, rows, :], wua_ref[...], preferred_element_type=F32)))
    merged = [(gate_ref[0, rows, :D_MODEL].astype(F32) * up_h
               + gate_ref[0, rows, D_MODEL:].astype(F32) * up_a).astype(BF16)
              for rows, (up_h, up_a) in zip(groups, ups)]
    x1s = [x_ref[0, rows, :] + jnp.dot(m, wo_ref[...], preferred_element_type=F32)
           for rows, m in zip(groups, merged)]
    hns = [_rms(x1, g2_ref[...]).astype(BF16) for x1 in x1s]
    accs = list(x1s)
    for c in range(len(FF_SPLITS) - 1):
        cols = slice(FF_SPLITS[c], FF_SPLITS[c + 1])
        gates = [jnp.dot(hn, wg_ref[:, cols], preferred_element_type=F32) for hn in hns]
        ups = [jnp.dot(hn, wu_ref[:, cols], preferred_element_type=F32) for hn in hns]
        acts = [(jax.nn.silu(g) * u).astype(BF16) for g, u in zip(gates, ups)]
        accs = [acc + jnp.dot(act, wd_ref[cols, :], preferred_element_type=F32) for acc, act in zip(accs, acts)]
    for rows, acc in zip(groups, accs):
        o_ref[0, rows, :] = _rms(acc, gf_ref[...])


def _mix_call(x, batch0, yh, ya, gates, wuh, wua, wo, g2, wg, wu, wd, gf):
    b = x.shape[0]
    tile = lambda w: pl.BlockSpec((1, MIX_TM, w), lambda bi, i: (bi, i, 0))
    stacked_tile = lambda w: pl.BlockSpec((1, MIX_TM, w), lambda bi, i: (batch0 + bi, i, 0))
    return pl.pallas_call(
        _mix_kernel,
        grid=(b, SEQ // MIX_TM),
        in_specs=[tile(D_MODEL), pl.BlockSpec((1, N_CB, MIX_TM, CB), lambda bi, i: (batch0 + bi, 0, i, 0)),
                  stacked_tile(D_ATTN), stacked_tile(2 * D_MODEL),
                  _const_spec((D_HYENA, D_MODEL)), _const_spec((D_ATTN, D_MODEL)),
                  _const_spec((D_MODEL, D_MODEL)), _const_spec((1, D_MODEL)),
                  _const_spec((D_MODEL, D_FF)), _const_spec((D_MODEL, D_FF)), _const_spec((D_FF, D_MODEL)),
                  _const_spec((1, D_MODEL))],
        out_specs=tile(D_MODEL),
        out_shape=jax.ShapeDtypeStruct((b, SEQ, D_MODEL), F32),
        compiler_params=_cparams(("parallel", "parallel")),
        name="mix",
    )(x, yh, ya, gates, wuh, wua, wo, g2, wg, wu, wd, gf)


def _layers(xs, tables, rope, kf, norm1_g, w_in, short_w, short_b, sink_logit, w_up_hyena,
            w_up_attn, w_out, norm2_g, w_ff_gate, w_ff_up, w_ff_down, final_g):
    kr, ki = kf
    total = sum(x.shape[0] for x in xs)
    conv_w = jnp.concatenate([short_w, short_b[None, :], jnp.zeros((4, D_UH), F32)], axis=0)
    stacked, offsets, off = None, [], 0
    for x in xs:
        stacked = _proj_call(x, off, total, stacked, norm1_g[None, :], w_in, conv_w, *rope)
        offsets.append(off)
        off += x.shape[0]
    uh, q, k, v, gates = stacked
    blocks = D_HYENA // LANES
    z1 = _hyena_call(uh, 0, uh, blocks, kr, ki, 0, tables)
    yh = _hyena_call(z1, 0, uh, 2 * blocks, kr, ki, 1, tables)
    ya, (wuh, wua, wo, wg, wu, wd) = _attn_call(sink_logit, q, k, v,
                                                (w_up_hyena, w_up_attn, w_out, w_ff_gate, w_ff_up, w_ff_down))
    weights = (wuh, wua, wo, norm2_g[None, :], wg, wu, wd, final_g[None, :])
    return tuple(_mix_call(x, o, yh, ya, gates, *weights) for x, o in zip(xs, offsets))


def kernel(x_prompt, x_sample, norm1_g, w_in, short_w, short_b, filt_w0, filt_b0, filt_w_inner, filt_b_inner,
           filt_w_out, filt_freq, hyena_bias, sink_logit, w_up_hyena, w_up_attn, w_out, norm2_g, w_ff_gate,
           w_ff_up, w_ff_down, final_g):
    tables = _dft_tables()
    rope = _rope_tables()
    z_emb, deltas = _filter_embedding()
    w0 = jnp.pad(filt_w0[0], ((0, LANES - FILTER_EMB), (0, 0)))
    h, w_in_bf16 = _filter_call(z_emb, w0, filt_b0[0][None, :], filt_w_inner[0], filt_b_inner[0][:, None, :],
                                filt_w_out[0], filt_freq[0][None, :], deltas, w_in[0])
    kf = _kf_call(h, hyena_bias[0][:, None, :], tables[0], tables[1])
    args = (norm1_g[0], w_in_bf16, short_w[0], short_b[0], sink_logit[0], w_up_hyena[0],
            w_up_attn[0], w_out[0], norm2_g[0], w_ff_gate[0], w_ff_up[0], w_ff_down[0], final_g)
    y_prompt, y_sample = _layers((x_prompt, x_sample), tables, rope, kf, *args)
    return (y_prompt, y_sample)
```

```python
import functools
import math

import jax
import jax.numpy as jnp
from jax import lax
from jax.experimental import pallas as pl
from jax.experimental.pallas import tpu as pltpu

F32 = jnp.float32
BF16 = jnp.bfloat16

D_MODEL = 1024
SEQ = 4096
D_HYENA = 512
HYENA_ORDER = 2
FILTER_EMB = 33
FILTER_ORDER = 64
N_FILT_CH = 2 * HYENA_ORDER * D_HYENA
FAST_DECAY_PCT = 0.3
SLOW_DECAY_PCT = 1.5
DECAY_TARGET = 1e-2
N_Q_HEADS = 8
N_KV_HEADS = 2
HEAD_DIM = 64
D_ATTN = N_Q_HEADS * HEAD_DIM
D_KV = N_KV_HEADS * HEAD_DIM
WINDOW = 128
ROPE_THETA = 500000.0
ROT_DIM = HEAD_DIM // 4
D_UH = (HYENA_ORDER + 1) * D_HYENA
D_IN = D_UH + D_ATTN + 2 * D_KV + 2 * D_MODEL
D_FF = 2816
EPS = 1e-6
NEG_INF = -1e30

LANES = 128
SUBLANES = 8
BF16_ROWS = 2 * SUBLANES
N_FFT = 2 * SEQ
N1 = 128
N2 = N_FFT // N1
H1 = N1 // 2
Z_STRIDE = N2 + 1
A_STRIDE = N1 + 1
CB = LANES
N_CB = D_HYENA // CB
VMEM_LIMIT = 56 * 1024 * 1024


def _cparams(sem):
    return pltpu.CompilerParams(dimension_semantics=sem, vmem_limit_bytes=VMEM_LIMIT)


def _cast_rider_specs(weights, n_chunks, flat_step):
    specs = []
    for w in weights:
        rows = w.shape[0] // n_chunks
        assert rows * n_chunks == w.shape[0] and rows % BF16_ROWS == 0
        specs.append(pl.BlockSpec((rows, w.shape[1]),
                                  lambda *g: (jnp.minimum(flat_step(*g), n_chunks - 1), 0)))
    return specs


def _cast_rider(step, n_chunks, src_refs, dst_refs):
    @pl.when(step < n_chunks)
    def _():
        for src, dst in zip(src_refs, dst_refs):
            dst[...] = src[...].astype(BF16)


def _const_spec(shape):
    nd = len(shape)
    return pl.BlockSpec(shape, lambda *_: (0,) * nd, pipeline_mode=pl.Buffered(1))


def _dft_tables():
    two_pi = 2.0 * math.pi
    n1 = jnp.arange(H1, dtype=jnp.int32)
    n2 = jnp.arange(N2, dtype=jnp.int32)
    k1 = jnp.arange(N1, dtype=jnp.int32)
    ang_a = ((n1[None, :] * k1[:, None]) % N1).astype(F32) * (two_pi / N1)
    ang_t = (n2[:, None] * k1[None, :]).astype(F32) * (two_pi / N_FFT)
    ar, ai = jnp.cos(ang_a)[None], -jnp.sin(ang_a)[None]
    tr, ti = jnp.cos(ang_t)[:, :, None], -jnp.sin(ang_t)[:, :, None]
    er, ei = ar * tr - ai * ti, ar * ti + ai * tr
    f1 = jnp.concatenate([jnp.concatenate([er, -ei], axis=2),
                          jnp.concatenate([ei, er], axis=2)], axis=1)
    ert, eit = jnp.swapaxes(er, 1, 2), jnp.swapaxes(ei, 1, 2)
    i2 = jnp.concatenate([ert, eit], axis=1)
    mg = (n2[:, None] * n2[None, :]) % N2
    angg = mg.astype(F32) * (two_pi / N2)
    gr, gi = jnp.cos(angg), -jnp.sin(angg)
    f2 = jnp.concatenate([jnp.concatenate([gr, -gi], axis=1),
                          jnp.concatenate([gi, gr], axis=1)], axis=0)
    f2 = f2.reshape(2, N2 // SUBLANES, SUBLANES, 2 * N2).transpose(1, 0, 2, 3).reshape(2 * N2, 2 * N2)
    i1 = jnp.concatenate([jnp.concatenate([gr, gi], axis=1),
                          jnp.concatenate([-gi, gr], axis=1)], axis=0)
    return f1.astype(BF16), f2.astype(BF16), i1.astype(BF16), i2.astype(BF16)


def _rope_tables():
    half = ROT_DIM // 2
    pos = jnp.arange(SEQ, dtype=F32)
    inv = 1.0 / (ROPE_THETA ** (jnp.arange(0, ROT_DIM, 2, dtype=F32) / ROT_DIM))
    ang = pos[:, None] * inv[None, :]
    cos, sin = jnp.cos(ang), jnp.sin(ang)
    ones = jnp.ones((SEQ, HEAD_DIM - ROT_DIM), F32)
    zeros = jnp.zeros((SEQ, HEAD_DIM - ROT_DIM), F32)
    zh = jnp.zeros((SEQ, half), F32)
    c = jnp.concatenate([cos, cos, ones], axis=1)
    s_up = jnp.concatenate([-sin, zh, zeros], axis=1)
    s_dn = jnp.concatenate([zh, sin, zeros], axis=1)
    rep = LANES // HEAD_DIM
    return jnp.tile(c, (1, rep)), jnp.tile(s_up, (1, rep)), jnp.tile(s_dn, (1, rep))


def _filter_embedding():
    bands = (FILTER_EMB - 1) // 2
    t = jnp.linspace(0.0, 1.0, SEQ, dtype=F32)[:, None]
    w = 2.0 * math.pi * jnp.arange(SEQ, dtype=F32)[:, None] / SEQ
    f = jnp.linspace(1e-4, bands - 1, bands, dtype=F32)[None, :]
    z = jnp.concatenate([t, jnp.cos(f * w), jnp.sin(f * w)], axis=-1)
    z = jnp.pad(z, ((0, 0), (0, LANES - FILTER_EMB)))
    min_decay = math.log(DECAY_TARGET) / SLOW_DECAY_PCT
    max_decay = math.log(DECAY_TARGET) / FAST_DECAY_PCT
    deltas = jnp.tile(jnp.linspace(min_decay, max_decay, D_HYENA, dtype=F32), 2 * HYENA_ORDER)
    return z, jnp.abs(deltas)[None, :]


PROJ_TM = 1024
PROJ_COLS = 512
PROJ_HALO = BF16_ROWS
LOG2E = math.log2(math.e)
Q_SCALE = HEAD_DIM ** -0.5 * LOG2E


def _proj_kernel(x_ref, xp_ref, xn_ref, g_ref, w_ref, cw_ref, c_ref, su_ref, sd_ref, *refs):
    uh_ref, q_ref, k_ref, v_ref, gate_ref = refs[-5:]

    def norm(xf):
        return xf * lax.rsqrt(jnp.mean(xf * xf, axis=-1, keepdims=True) + EPS) * g_ref[...]

    i = pl.program_id(1)
    hn = norm(x_ref[0]).astype(BF16)
    h_prev = (norm(xp_ref[0]) * (i > 0).astype(F32)).astype(BF16)
    h_next = (norm(xn_ref[0]) * (i < pl.num_programs(1) - 1).astype(F32)).astype(BF16)
    hn_ext = jnp.concatenate([h_prev, hn, h_next], axis=0)

    def proj(c0, width):
        return jnp.dot(hn, w_ref[:, c0:c0 + width], preferred_element_type=F32)

    rows = PROJ_TM + 2 * PROJ_HALO
    for j in range(D_UH // PROJ_COLS):
        cols = slice(j * PROJ_COLS, (j + 1) * PROJ_COLS)
        u = jnp.dot(hn_ext, w_ref[:, cols], preferred_element_type=F32)
        prev = pltpu.roll(u, 1, axis=0)[PROJ_HALO:PROJ_HALO + PROJ_TM]
        nxt = pltpu.roll(u, rows - 1, axis=0)[PROJ_HALO:PROJ_HALO + PROJ_TM]
        cur = u[PROJ_HALO:PROJ_HALO + PROJ_TM]
        uc = (prev * cw_ref[0:1, cols] + cur * cw_ref[1:2, cols] + nxt * cw_ref[2:3, cols] + cw_ref[3:4, cols])
        for c in range(PROJ_COLS // LANES):
            uh_ref[0, j * (PROJ_COLS // LANES) + c] = uc[:, c * LANES:(c + 1) * LANES].astype(BF16)

    def rope(xc):
        return (xc * c_ref[...] + pltpu.roll(xc, LANES - ROT_DIM // 2, axis=1) * su_ref[...]
                + pltpu.roll(xc, ROT_DIM // 2, axis=1) * sd_ref[...])

    qkv = proj(D_UH, D_ATTN + 2 * D_KV)
    for j in range(D_ATTN // LANES):
        qc = qkv[:, j * LANES:(j + 1) * LANES]
        q_ref[0, :, j * LANES:(j + 1) * LANES] = (rope(qc) * Q_SCALE).astype(BF16)
    lo = lax.broadcasted_iota(jnp.int32, (PROJ_TM, LANES), 1) < HEAD_DIM
    for val, ref in ((rope(qkv[:, D_ATTN:D_ATTN + D_KV]), k_ref), (qkv[:, D_ATTN + D_KV:], v_ref)):
        rolled = pltpu.roll(val, HEAD_DIM, axis=1)
        ref[0, 0] = jnp.where(lo, val, 0.0).astype(BF16)
        ref[0, 1] = jnp.where(lo, 0.0, rolled).astype(BF16)
        ref[0, 2] = jnp.where(lo, rolled, 0.0).astype(BF16)
        ref[0, 3] = jnp.where(lo, 0.0, val).astype(BF16)
    g0 = D_UH + D_ATTN + 2 * D_KV
    for j in range(2 * D_MODEL // PROJ_COLS):
        cols = slice(j * PROJ_COLS, (j + 1) * PROJ_COLS)
        gate_ref[0, :, cols] = jax.nn.sigmoid(proj(g0 + j * PROJ_COLS, PROJ_COLS)).astype(BF16)


def _proj_call(x, batch0, total, stacked, g1, w_in, conv_w, rope_c, rope_su, rope_sd):
    b = x.shape[0]
    nt = SEQ // PROJ_TM
    per_tile = PROJ_TM // PROJ_HALO
    tile = lambda w: pl.BlockSpec((1, PROJ_TM, w), lambda bi, i: (bi, i, 0))
    halo_prev = pl.BlockSpec((1, PROJ_HALO, D_MODEL), lambda bi, i: (bi, jnp.maximum(i * per_tile - 1, 0), 0))
    halo_next = pl.BlockSpec((1, PROJ_HALO, D_MODEL),
                             lambda bi, i: (bi, jnp.minimum((i + 1) * per_tile, SEQ // PROJ_HALO - 1), 0))
    rope_spec = pl.BlockSpec((PROJ_TM, LANES), lambda bi, i: (i, 0))
    out_tile = lambda w: pl.BlockSpec((1, PROJ_TM, w), lambda bi, i: (batch0 + bi, i, 0))
    out_slabs = lambda n: pl.BlockSpec((1, n, PROJ_TM, LANES), lambda bi, i: (batch0 + bi, 0, i, 0))
    in_specs = [tile(D_MODEL), halo_prev, halo_next, _const_spec((1, D_MODEL)), _const_spec((D_MODEL, D_IN)),
                _const_spec((8, D_UH)), rope_spec, rope_spec, rope_spec]
    operands = [x, x, x, g1, w_in, conv_w, rope_c, rope_su, rope_sd]
    aliases = {}
    if stacked is not None:
        aliases = {len(operands) + n: n for n in range(len(stacked))}
        in_specs += [pl.BlockSpec(memory_space=pl.ANY)] * len(stacked)
        operands += list(stacked)
    return pl.pallas_call(
        _proj_kernel,
        grid=(b, nt),
        in_specs=in_specs,
        out_specs=[out_slabs(D_UH // LANES), out_tile(D_ATTN), out_slabs(2 * N_KV_HEADS),
                   out_slabs(2 * N_KV_HEADS), out_tile(2 * D_MODEL)],
        out_shape=[jax.ShapeDtypeStruct((total, D_UH // LANES, SEQ, LANES), BF16),
                   jax.ShapeDtypeStruct((total, SEQ, D_ATTN), BF16),
                   jax.ShapeDtypeStruct((total, 2 * N_KV_HEADS, SEQ, LANES), BF16),
                   jax.ShapeDtypeStruct((total, 2 * N_KV_HEADS, SEQ, LANES), BF16),
                   jax.ShapeDtypeStruct((total, SEQ, 2 * D_MODEL), BF16)],
        input_output_aliases=aliases,
        compiler_params=_cparams(("parallel", "parallel")),
        name="proj",
    )(*operands)


FILT_TL = 512


def _filter_kernel(z_ref, w0_ref, b0_ref, wi_ref, bi_ref, wo_ref, fr_ref, dl_ref, ride_src, h_ref, ride_dst):
    _cast_rider(pl.program_id(0), pl.num_programs(0), [ride_src], [ride_dst])
    hi = lax.Precision.HIGHEST
    half = FILT_TL // 2
    z = z_ref[...]
    z2 = jnp.concatenate([z[:half], z[half:]], axis=1)
    fr = fr_ref[...]
    h = jnp.sin(fr * (jnp.dot(z2, w0_ref[...], precision=hi, preferred_element_type=F32) + b0_ref[...]))
    for i in range(wi_ref.shape[0]):
        h = jnp.sin(fr * (jnp.dot(h, wi_ref[i], precision=hi, preferred_element_type=F32) + bi_ref[i]))
    def split(v):
        top = v.astype(BF16)
        return top, (v - top.astype(F32)).astype(BF16)
    h_hi, h_lo = split(h)
    w_hi, w_lo = split(wo_ref[...])
    out = (jnp.dot(h_hi, w_hi, preferred_element_type=F32) + jnp.dot(h_hi, w_lo, preferred_element_type=F32)
           + jnp.dot(h_lo, w_hi, preferred_element_type=F32))
    for r in range(2):
        rows = slice(r * half, (r + 1) * half)
        t = z[rows, 0:1]
        h_ref[rows, :] = out[:, r * N_FILT_CH:(r + 1) * N_FILT_CH] * jnp.exp(-t * dl_ref[...])


def _block_diag2(w):
    zeros = jnp.zeros_like(w)
    return jnp.concatenate([jnp.concatenate([w, zeros], axis=-1), jnp.concatenate([zeros, w], axis=-1)], axis=-2)


def _filter_call(z, w0, b0, wi, bi, wo, fr, deltas, rider):
    n_inner = wi.shape[0]
    twice = lambda v: jnp.concatenate([v, v], axis=-1)
    w0, wi, wo = _block_diag2(w0), _block_diag2(wi), _block_diag2(wo)
    b0, bi, fr = twice(b0), twice(bi), twice(fr)
    wide = 2 * FILTER_ORDER
    steps = SEQ // FILT_TL
    ride_in, = _cast_rider_specs([rider], steps, lambda i: i)
    ride_out, = _cast_rider_specs([rider], steps, lambda i: i)
    return pl.pallas_call(
        _filter_kernel,
        grid=(steps,),
        in_specs=[pl.BlockSpec((FILT_TL, LANES), lambda i: (i, 0)),
                  _const_spec((2 * LANES, wide)), _const_spec((1, wide)),
                  _const_spec((n_inner, wide, wide)), _const_spec((n_inner, 1, wide)),
                  _const_spec((wide, 2 * N_FILT_CH)), _const_spec((1, wide)),
                  _const_spec((1, N_FILT_CH)), ride_in],
        out_specs=[pl.BlockSpec((FILT_TL, N_FILT_CH), lambda i: (i, 0)), ride_out],
        out_shape=[jax.ShapeDtypeStruct((SEQ, N_FILT_CH), F32), jax.ShapeDtypeStruct(rider.shape, BF16)],
        compiler_params=_cparams(("arbitrary",)),
        name="filt",
    )(z, w0, b0, wi, bi, wo, fr, deltas, rider)


STAGE_UNROLL = 64
KB = 2
MID_UNROLL = 64


def _stage_f1(z_ref, a_ref, f1_ref, real_only):
    def body(n2, carry):
        zr = z_ref[0, pl.ds(n2, H1, stride=Z_STRIDE), :]
        if real_only:
            res = jnp.dot(f1_ref[n2][:, :H1], zr.astype(BF16), preferred_element_type=F32)
        else:
            zi = z_ref[1, pl.ds(n2, H1, stride=Z_STRIDE), :]
            st = jnp.concatenate([zr, zi], axis=0).astype(BF16)
            res = jnp.dot(f1_ref[n2], st, preferred_element_type=F32)
        base = n2 * A_STRIDE
        a_ref[0, pl.ds(base, N1), :] = res[:N1]
        a_ref[1, pl.ds(base, N1), :] = res[N1:]
        return carry
    lax.fori_loop(0, N2, body, 0, unroll=STAGE_UNROLL)


def _dft_blocks(mat_ref, vr, vi, interleaved):
    st = jnp.concatenate([vr, vi], axis=0).astype(BF16)
    x = jnp.dot(mat_ref[...], st, preferred_element_type=F32)
    if not interleaved:
        return x[:N2], x[N2:]
    x4 = x.reshape(N2 // SUBLANES, 2, SUBLANES, x.shape[-1])
    return x4[:, 0].reshape(N2, x.shape[-1]), x4[:, 1].reshape(N2, x.shape[-1])


def _load_spectrum_blocks(a_ref, f2_ref, k1b):
    ar = jnp.concatenate([a_ref[0, pl.ds(k1b + i, N2, stride=A_STRIDE), :] for i in range(KB)], axis=1)
    ai = jnp.concatenate([a_ref[1, pl.ds(k1b + i, N2, stride=A_STRIDE), :] for i in range(KB)], axis=1)
    return _dft_blocks(f2_ref, ar, ai, True)


def _spectrum_rows(k1b, i):
    return pl.ds(pl.multiple_of((k1b + i) * N2, N2), N2)


def _kf_kernel(hf_ref, hb_ref, b_ref, f1_ref, f2_ref, kr_ref, ki_ref, z_ref, ab_ref, acc_ref):
    inv_n = 1.0 / N_FFT

    def load_time(src_ref, backward):
        def body(j, carry):
            rows = pl.multiple_of(j * N2, N2)
            v = src_ref[pl.ds(rows, N2), :]
            ridx = lax.broadcasted_iota(jnp.int32, (N2, CB), 0) + rows
            v = jnp.where(ridx == 0, 0.0 if backward else v + b_ref[...], v)
            z_ref[0, pl.ds(j * Z_STRIDE, N2), :] = v
            return carry
        lax.fori_loop(0, H1, body, 0, unroll=4)

    def spectrum(accumulate):
        def body(it, carry):
            k1b = it * KB
            xr, xi = _load_spectrum_blocks(ab_ref, f2_ref, k1b)
            for i in range(KB):
                rows = _spectrum_rows(k1b, i)
                lanes = slice(i * CB, (i + 1) * CB)
                if accumulate:
                    kr_ref[rows, :] = (acc_ref[0, rows, :] + xr[:, lanes] * inv_n).astype(BF16)
                    ki_ref[rows, :] = (acc_ref[1, rows, :] - xi[:, lanes] * inv_n).astype(BF16)
                else:
                    acc_ref[0, rows, :] = xr[:, lanes] * inv_n
                    acc_ref[1, rows, :] = xi[:, lanes] * inv_n
            return carry
        lax.fori_loop(0, N1 // KB, body, 0, unroll=MID_UNROLL)

    load_time(hf_ref, False)
    _stage_f1(z_ref, ab_ref, f1_ref, True)
    spectrum(False)
    load_time(hb_ref, True)
    _stage_f1(z_ref, ab_ref, f1_ref, True)
    spectrum(True)


def _kf_call(h, bias, f1, f2):
    blocks_per_order = N_CB
    bias_spec = pl.BlockSpec((None, 1, CB), lambda o, c: (o, 0, c))
    fwd = pl.BlockSpec((SEQ, CB), lambda o, c: (0, o * blocks_per_order + c))
    bwd = pl.BlockSpec((SEQ, CB), lambda o, c: (0, (HYENA_ORDER + o) * blocks_per_order + c))
    out = pl.BlockSpec((None, N_FFT, CB), lambda o, c: (o, 0, c))
    return pl.pallas_call(
        _kf_kernel,
        grid=(HYENA_ORDER, N_CB),
        in_specs=[fwd, bwd, bias_spec, _const_spec((N2, 2 * N1, 2 * H1)), _const_spec((2 * N2, 2 * N2))],
        out_specs=[out, out],
        out_shape=[jax.ShapeDtypeStruct((HYENA_ORDER, N_FFT, D_HYENA), BF16)] * 2,
        scratch_shapes=[pltpu.VMEM((2, H1 * Z_STRIDE, CB), F32), pltpu.VMEM((2, N2 * A_STRIDE, CB), F32),
                        pltpu.VMEM((2, N_FFT, CB), F32)],
        compiler_params=_cparams(("parallel", "parallel")),
        name="kf",
    )(h, h, bias, f1, f2)


TIME_ROWS = 256
GROUPS = TIME_ROWS // N2


def _group_rows(j, i):
    return pl.ds((j * GROUPS + i) * Z_STRIDE, N2)


def _hyena_kernel(v_ref, x1_ref, x2_ref, kr_ref, ki_ref, f1_ref, f2_ref, i1_ref, i2_ref, o_ref, z_ref, s_ref):
    def fill(j, carry):
        rows = pl.ds(pl.multiple_of(j * TIME_ROWS, TIME_ROWS), TIME_ROWS)
        for r in range(2):
            val = v_ref[r, rows, :].astype(F32)
            for i in range(GROUPS):
                z_ref[r, _group_rows(j, i), :] = val[i * N2:(i + 1) * N2]
        return carry
    lax.fori_loop(0, SEQ // TIME_ROWS, fill, 0, unroll=2)

    for order in range(HYENA_ORDER):
        _stage_f1(z_ref, s_ref, f1_ref, False)

        def mid(it, carry, order=order):
            k1b = it * KB
            xr, xi = _load_spectrum_blocks(s_ref, f2_ref, k1b)
            kr = jnp.concatenate([kr_ref[order, _spectrum_rows(k1b, i), :] for i in range(KB)], axis=1).astype(F32)
            ki = jnp.concatenate([ki_ref[order, _spectrum_rows(k1b, i), :] for i in range(KB)], axis=1).astype(F32)
            br, bi = _dft_blocks(i1_ref, xr * kr - xi * ki, xr * ki + xi * kr, False)
            for i in range(KB):
                lanes = slice(i * CB, (i + 1) * CB)
                s_ref[0, pl.ds(k1b + i, N2, stride=A_STRIDE), :] = br[:, lanes]
                s_ref[1, pl.ds(k1b + i, N2, stride=A_STRIDE), :] = bi[:, lanes]
            return carry
        lax.fori_loop(0, N1 // KB, mid, 0, unroll=MID_UNROLL)

        def last(n2, carry):
            base = n2 * A_STRIDE
            st = jnp.concatenate([s_ref[0, pl.ds(base, N1), :], s_ref[1, pl.ds(base, N1), :]], axis=1).astype(BF16)
            p = jnp.dot(i2_ref[n2], st, preferred_element_type=F32)
            z_ref[0, pl.ds(n2, H1, stride=Z_STRIDE), :] = p[:H1, :CB] + p[H1:, CB:]
            z_ref[1, pl.ds(n2, H1, stride=Z_STRIDE), :] = p[:H1, CB:] - p[H1:, :CB]
            return carry
        lax.fori_loop(0, N2, last, 0, unroll=STAGE_UNROLL)

        def gate(j, carry, order=order):
            rows = pl.ds(pl.multiple_of(j * TIME_ROWS, TIME_ROWS), TIME_ROWS)
            for r in range(2):
                if order + 1 < HYENA_ORDER:
                    g = x1_ref[r, rows, :].astype(F32)
                    for i in range(GROUPS):
                        z_ref[r, _group_rows(j, i), :] = g[i * N2:(i + 1) * N2] * z_ref[r, _group_rows(j, i), :]
                else:
                    y = jnp.concatenate([z_ref[r, _group_rows(j, i), :] for i in range(GROUPS)], axis=0)
                    o_ref[r, rows, :] = (x2_ref[r, rows, :].astype(F32) * y).astype(BF16)
            return carry
        lax.fori_loop(0, SEQ // TIME_ROWS, gate, 0, unroll=2)


def _hyena_call(uh, kr, ki, tables):
    assert HYENA_ORDER == 2
    b = uh.shape[0]
    f1, f2, i1, i2 = tables
    io = lambda slot: pl.BlockSpec((2, None, SEQ, CB), lambda c, p: (p, slot + c, 0, 0))
    kspec = pl.BlockSpec((HYENA_ORDER, N_FFT, CB), lambda c, p: (0, 0, c))
    return pl.pallas_call(
        _hyena_kernel,
        grid=(N_CB, b // 2),
        in_specs=[io(0), io(N_CB), io(2 * N_CB), kspec, kspec,
                  _const_spec((N2, 2 * N1, 2 * H1)), _const_spec((2 * N2, 2 * N2)),
                  _const_spec((2 * N2, 2 * N2)), _const_spec((N2, 2 * H1, N1))],
        out_specs=io(0),
        out_shape=jax.ShapeDtypeStruct((b, N_CB, SEQ, CB), BF16),
        scratch_shapes=[pltpu.VMEM((2, H1 * Z_STRIDE, CB), F32), pltpu.VMEM((2, N2 * A_STRIDE, CB), F32)],
        compiler_params=_cparams(("parallel", "parallel")),
        name="hyena",
    )(uh, uh, uh, kr, ki, f1, f2, i1, i2)


ATT_TQ = 128
ATT_KW = ATT_TQ + 2 * WINDOW


ATT_QB = 16
ATT_RIDER_CHUNKS = 16


def _attn_kernel(sink_ref, q_ref, k_ref, v_ref, *refs):
    n_riders = (len(refs) - 1) // 2
    o_ref = refs[n_riders]
    _cast_rider(pl.program_id(0) * pl.num_programs(1) + pl.program_id(1), ATT_RIDER_CHUNKS,
                refs[:n_riders], refs[n_riders + 1:])
    lo_q = lax.broadcasted_iota(jnp.int32, (ATT_TQ, LANES), 1) < HEAD_DIM
    for qb in range(ATT_QB):
        i = pl.program_id(1) * ATT_QB + qb
        start = pl.multiple_of(jnp.clip(i * ATT_TQ - WINDOW, 0, SEQ - ATT_KW), LANES)
        win = pl.ds(start, ATT_KW)
        qrows = slice(qb * ATT_TQ, (qb + 1) * ATT_TQ)
        qpos = i * ATT_TQ + lax.broadcasted_iota(jnp.int32, (ATT_TQ, ATT_KW), 0)
        kpos = start + lax.broadcasted_iota(jnp.int32, (ATT_TQ, ATT_KW), 1)
        bias = jnp.where(jnp.abs(qpos - kpos) <= WINDOW, 0.0, NEG_INF).astype(BF16)
        pairs = [(h, j) for h in range(N_KV_HEADS) for j in range(2)]
        scores = []
        for h, j in pairs:
            kbd = jnp.concatenate([k_ref[0, 2 * h, win, :], k_ref[0, 2 * h + 1, win, :]], axis=0)
            q2 = q_ref[0, qrows, LANES * (2 * h + j):LANES * (2 * h + j + 1)]
            scores.append(lax.dot_general(q2, kbd, (((1,), (1,)), ((), ())), preferred_element_type=F32))
        probs, scales = [], []
        for (h, j), s in zip(pairs, scores):
            ps, inv = [], []
            for e in range(2):
                sk = sink_ref[4 * h + 2 * j + e] * LOG2E
                se = s[:, e * ATT_KW:(e + 1) * ATT_KW].astype(BF16) + bias
                m = jnp.maximum(jnp.max(se, axis=-1, keepdims=True).astype(F32), sk)
                p = jnp.exp2(se - m.astype(BF16))
                part = functools.reduce(lambda a, b: a + b,
                                        [p[:, c * LANES:(c + 1) * LANES] for c in range(ATT_KW // LANES)])
                inv.append(1.0 / (jnp.sum(part.astype(F32), axis=-1, keepdims=True) + jnp.exp2(sk - m)))
                ps.append(p)
            probs.append(jnp.concatenate(ps, axis=1))
            scales.append(jnp.where(lo_q, inv[0], inv[1]))
        for (h, j), p2, sc in zip(pairs, probs, scales):
            vbd = jnp.concatenate([v_ref[0, 2 * h, win, :], v_ref[0, 2 * h + 1, win, :]], axis=0)
            o2 = jnp.dot(p2, vbd, preferred_element_type=F32)
            c0 = LANES * (2 * h + j)
            o_ref[0, qrows, c0:c0 + LANES] = (o2 * sc).astype(BF16)


def _attn_call(sink, q, k, v, riders):
    b = q.shape[0]
    nt = SEQ // (ATT_TQ * ATT_QB)
    assert not riders or b * nt >= ATT_RIDER_CHUNKS
    full = pl.BlockSpec((1, 2 * N_KV_HEADS, SEQ, LANES), lambda bi, i: (bi, 0, 0, 0))
    tile = pl.BlockSpec((1, ATT_TQ * ATT_QB, D_ATTN), lambda bi, i: (bi, i, 0))
    flat = lambda bi, i: bi * nt + i
    out = pl.pallas_call(
        _attn_kernel,
        grid=(b, nt),
        in_specs=[pl.BlockSpec(memory_space=pltpu.SMEM), tile, full, full,
                  *_cast_rider_specs(riders, ATT_RIDER_CHUNKS, flat)],
        out_specs=[tile, *_cast_rider_specs(riders, ATT_RIDER_CHUNKS, flat)],
        out_shape=[jax.ShapeDtypeStruct((b, SEQ, D_ATTN), BF16),
                   *[jax.ShapeDtypeStruct(w.shape, BF16) for w in riders]],
        compiler_params=_cparams(("arbitrary", "arbitrary")),
        name="attn",
    )(sink, q, k, v, *riders)
    return out[0], out[1:]


MIX_TM = 512
MIX_SPLIT = 2
MXU_COLS = 256
FF_SPLITS = (0, 6 * MXU_COLS, D_FF)


def _rms(x, g):
    return x * lax.rsqrt(jnp.mean(x * x, axis=-1, keepdims=True) + EPS) * g


def _mix_kernel(x_ref, yh_ref, ya_ref, gate_ref, wuh_ref, wua_ref, wo_ref, g2_ref, wg_ref, wu_ref, wd_ref,
                gf_ref, o_ref):
    groups = [slice(r * (MIX_TM // MIX_SPLIT), (r + 1) * (MIX_TM // MIX_SPLIT)) for r in range(MIX_SPLIT)]
    ups = []
    for rows in groups:
        yh = jnp.concatenate([yh_ref[0, c, rows, :] for c in range(N_CB)], axis=1)
        ups.append((jnp.dot(yh, wuh_ref[...], preferred_element_type=F32),
                    jnp.dot(ya_ref[0, rows, :], wua_ref[...], preferred_element_type=F32)))
    merged = [(gate_ref[0, rows, :D_MODEL].astype(F32) * up_h
               + gate_ref[0, rows, D_MODEL:].astype(F32) * up_a).astype(BF16)
              for rows, (up_h, up_a) in zip(groups, ups)]
    x1s = [x_ref[0, rows, :] + jnp.dot(m, wo_ref[...], preferred_element_type=F32)
           for rows, m in zip(groups, merged)]
    hns = [_rms(x1, g2_ref[...]).astype(BF16) for x1 in x1s]
    accs = list(x1s)
    for c in range(len(FF_SPLITS) - 1):
        cols = slice(FF_SPLITS[c], FF_SPLITS[c + 1])
        gates = [jnp.dot(hn, wg_ref[:, cols], preferred_element_type=F32) for hn in hns]
        ups = [jnp.dot(hn, wu_ref[:, cols], preferred_element_type=F32) for hn in hns]
        acts = [(jax.nn.silu(g) * u).astype(BF16) for g, u in zip(gates, ups)]
        accs = [acc + jnp.dot(act, wd_ref[cols, :], preferred_element_type=F32) for acc, act in zip(accs, acts)]
    for rows, acc in zip(groups, accs):
        o_ref[0, rows, :] = _rms(acc, gf_ref[...])


def _mix_call(x, batch0, yh, ya, gates, wuh, wua, wo, g2, wg, wu, wd, gf):
    b = x.shape[0]
    tile = lambda w: pl.BlockSpec((1, MIX_TM, w), lambda bi, i: (bi, i, 0))
    stacked_tile = lambda w: pl.BlockSpec((1, MIX_TM, w), lambda bi, i: (batch0 + bi, i, 0))
    return pl.pallas_call(
        _mix_kernel,
        grid=(b, SEQ // MIX_TM),
        in_specs=[tile(D_MODEL), pl.BlockSpec((1, N_CB, MIX_TM, CB), lambda bi, i: (batch0 + bi, 0, i, 0)),
                  stacked_tile(D_ATTN), stacked_tile(2 * D_MODEL),
                  _const_spec((D_HYENA, D_MODEL)), _const_spec((D_ATTN, D_MODEL)),
                  _const_spec((D_MODEL, D_MODEL)), _const_spec((1, D_MODEL)),
                  _const_spec((D_MODEL, D_FF)), _const_spec((D_MODEL, D_FF)), _const_spec((D_FF, D_MODEL)),
                  _const_spec((1, D_MODEL))],
        out_specs=tile(D_MODEL),
        out_shape=jax.ShapeDtypeStruct((b, SEQ, D_MODEL), F32),
        compiler_params=_cparams(("parallel", "parallel")),
        name="mix",
    )(x, yh, ya, gates, wuh, wua, wo, g2, wg, wu, wd, gf)


def _layers(xs, tables, rope, kf, norm1_g, w_in, short_w, short_b, sink_logit, w_up_hyena,
            w_up_attn, w_out, norm2_g, w_ff_gate, w_ff_up, w_ff_down, final_g):
    kr, ki = kf
    total = sum(x.shape[0] for x in xs)
    conv_w = jnp.concatenate([short_w, short_b[None, :], jnp.zeros((4, D_UH), F32)], axis=0)
    stacked, offsets, off = None, [], 0
    for x in xs:
        stacked = _proj_call(x, off, total, stacked, norm1_g[None, :], w_in, conv_w, *rope)
        offsets.append(off)
        off += x.shape[0]
    uh, q, k, v, gates = stacked
    yh = _hyena_call(uh, kr, ki, tables)
    ya, (wuh, wua, wo, wg, wu, wd) = _attn_call(sink_logit, q, k, v,
                                                (w_up_hyena, w_up_attn, w_out, w_ff_gate, w_ff_up, w_ff_down))
    weights = (wuh, wua, wo, norm2_g[None, :], wg, wu, wd, final_g[None, :])
    return tuple(_mix_call(x, o, yh, ya, gates, *weights) for x, o in zip(xs, offsets))


def kernel(x_prompt, x_sample, norm1_g, w_in, short_w, short_b, filt_w0, filt_b0, filt_w_inner, filt_b_inner,
           filt_w_out, filt_freq, hyena_bias, sink_logit, w_up_hyena, w_up_attn, w_out, norm2_g, w_ff_gate,
           w_ff_up, w_ff_down, final_g):
    tables = _dft_tables()
    rope = _rope_tables()
    z_emb, deltas = _filter_embedding()
    w0 = jnp.pad(filt_w0[0], ((0, LANES - FILTER_EMB), (0, 0)))
    h, w_in_bf16 = _filter_call(z_emb, w0, filt_b0[0][None, :], filt_w_inner[0], filt_b_inner[0][:, None, :],
                                filt_w_out[0], filt_freq[0][None, :], deltas, w_in[0])
    kf = _kf_call(h, hyena_bias[0][:, None, :], tables[0], tables[1])
    args = (norm1_g[0], w_in_bf16, short_w[0], short_b[0], sink_logit[0], w_up_hyena[0],
            w_up_attn[0], w_out[0], norm2_g[0], w_ff_gate[0], w_ff_up[0], w_ff_down[0], final_g)
    y_prompt, y_sample = _layers((x_prompt, x_sample), tables, rope, kf, *args)
    return (y_prompt, y_sample)
```

```python
import functools
import math

import jax
import jax.numpy as jnp
from jax import lax
from jax.experimental import pallas as pl
from jax.experimental.pallas import tpu as pltpu

F32 = jnp.float32
BF16 = jnp.bfloat16

D_MODEL = 1024
SEQ = 4096
D_HYENA = 512
HYENA_ORDER = 2
FILTER_EMB = 33
FILTER_ORDER = 64
N_FILT_CH = 2 * HYENA_ORDER * D_HYENA
FAST_DECAY_PCT = 0.3
SLOW_DECAY_PCT = 1.5
DECAY_TARGET = 1e-2
N_Q_HEADS = 8
N_KV_HEADS = 2
HEAD_DIM = 64
D_ATTN = N_Q_HEADS * HEAD_DIM
D_KV = N_KV_HEADS * HEAD_DIM
WINDOW = 128
ROPE_THETA = 500000.0
ROT_DIM = HEAD_DIM // 4
D_UH = (HYENA_ORDER + 1) * D_HYENA
D_IN = D_UH + D_ATTN + 2 * D_KV + 2 * D_MODEL
D_FF = 2816
EPS = 1e-6
NEG_INF = -1e30

LANES = 128
SUBLANES = 8
BF16_ROWS = 2 * SUBLANES
N_FFT = 2 * SEQ
N1 = 128
N2 = N_FFT // N1
H1 = N1 // 2
Z_STRIDE = N2 + 1
A_STRIDE = N1 + 1
CB = LANES
N_CB = D_HYENA // CB
VMEM_LIMIT = 56 * 1024 * 1024


def _cparams(sem):
    return pltpu.CompilerParams(dimension_semantics=sem, vmem_limit_bytes=VMEM_LIMIT)


def _cast_rider_specs(weights, n_chunks, flat_step):
    specs = []
    for w in weights:
        rows = w.shape[0] // n_chunks
        assert rows * n_chunks == w.shape[0] and rows % BF16_ROWS == 0
        specs.append(pl.BlockSpec((rows, w.shape[1]),
                                  lambda *g: (jnp.minimum(flat_step(*g), n_chunks - 1), 0)))
    return specs


def _cast_rider(step, n_chunks, src_refs, dst_refs):
    @pl.when(step < n_chunks)
    def _():
        for src, dst in zip(src_refs, dst_refs):
            dst[...] = src[...].astype(BF16)


def _const_spec(shape):
    nd = len(shape)
    return pl.BlockSpec(shape, lambda *_: (0,) * nd, pipeline_mode=pl.Buffered(1))


def _dft_tables():
    two_pi = 2.0 * math.pi
    n1 = jnp.arange(H1, dtype=jnp.int32)
    n2 = jnp.arange(N2, dtype=jnp.int32)
    k1 = jnp.arange(N1, dtype=jnp.int32)
    ang_a = ((n1[None, :] * k1[:, None]) % N1).astype(F32) * (two_pi / N1)
    ang_t = (n2[:, None] * k1[None, :]).astype(F32) * (two_pi / N_FFT)
    ar, ai = jnp.cos(ang_a)[None], -jnp.sin(ang_a)[None]
    tr, ti = jnp.cos(ang_t)[:, :, None], -jnp.sin(ang_t)[:, :, None]
    er, ei = ar * tr - ai * ti, ar * ti + ai * tr
    f1 = jnp.concatenate([jnp.concatenate([er, -ei], axis=2),
                          jnp.concatenate([ei, er], axis=2)], axis=1)
    ert, eit = jnp.swapaxes(er, 1, 2), jnp.swapaxes(ei, 1, 2)
    i2 = jnp.concatenate([ert, eit], axis=1)
    mg = (n2[:, None] * n2[None, :]) % N2
    angg = mg.astype(F32) * (two_pi / N2)
    gr, gi = jnp.cos(angg), -jnp.sin(angg)
    f2 = jnp.concatenate([jnp.concatenate([gr, -gi], axis=1),
                          jnp.concatenate([gi, gr], axis=1)], axis=0)
    f2 = f2.reshape(2, N2 // SUBLANES, SUBLANES, 2 * N2).transpose(1, 0, 2, 3).reshape(2 * N2, 2 * N2)
    i1 = jnp.concatenate([jnp.concatenate([gr, gi], axis=1),
                          jnp.concatenate([-gi, gr], axis=1)], axis=0)
    return f1.astype(BF16), f2.astype(BF16), i1.astype(BF16), i2.astype(BF16)


def _rope_tables():
    half = ROT_DIM // 2
    pos = jnp.arange(SEQ, dtype=F32)
    inv = 1.0 / (ROPE_THETA ** (jnp.arange(0, ROT_DIM, 2, dtype=F32) / ROT_DIM))
    ang = pos[:, None] * inv[None, :]
    cos, sin = jnp.cos(ang), jnp.sin(ang)
    ones = jnp.ones((SEQ, HEAD_DIM - ROT_DIM), F32)
    zeros = jnp.zeros((SEQ, HEAD_DIM - ROT_DIM), F32)
    zh = jnp.zeros((SEQ, half), F32)
    c = jnp.concatenate([cos, cos, ones], axis=1)
    s_up = jnp.concatenate([-sin, zh, zeros], axis=1)
    s_dn = jnp.concatenate([zh, sin, zeros], axis=1)
    rep = LANES // HEAD_DIM
    return jnp.tile(c, (1, rep)), jnp.tile(s_up, (1, rep)), jnp.tile(s_dn, (1, rep))


def _filter_embedding():
    bands = (FILTER_EMB - 1) // 2
    t = jnp.linspace(0.0, 1.0, SEQ, dtype=F32)[:, None]
    w = 2.0 * math.pi * jnp.arange(SEQ, dtype=F32)[:, None] / SEQ
    f = jnp.linspace(1e-4, bands - 1, bands, dtype=F32)[None, :]
    z = jnp.concatenate([t, jnp.cos(f * w), jnp.sin(f * w)], axis=-1)
    z = jnp.pad(z, ((0, 0), (0, LANES - FILTER_EMB)))
    min_decay = math.log(DECAY_TARGET) / SLOW_DECAY_PCT
    max_decay = math.log(DECAY_TARGET) / FAST_DECAY_PCT
    deltas = jnp.tile(jnp.linspace(min_decay, max_decay, D_HYENA, dtype=F32), 2 * HYENA_ORDER)
    return z, jnp.abs(deltas)[None, :]


PROJ_TM = 1024
PROJ_COLS = 512
PROJ_HALO = BF16_ROWS
LOG2E = math.log2(math.e)
Q_SCALE = HEAD_DIM ** -0.5 * LOG2E


def _proj_kernel(x_ref, xp_ref, xn_ref, g_ref, w_ref, cw_ref, c_ref, su_ref, sd_ref, *refs):
    uh_ref, q_ref, k_ref, v_ref, gate_ref = refs[-5:]

    def norm(xf):
        return xf * lax.rsqrt(jnp.mean(xf * xf, axis=-1, keepdims=True) + EPS) * g_ref[...]

    i = pl.program_id(1)
    hn = norm(x_ref[0]).astype(BF16)
    h_prev = (norm(xp_ref[0]) * (i > 0).astype(F32)).astype(BF16)
    h_next = (norm(xn_ref[0]) * (i < pl.num_programs(1) - 1).astype(F32)).astype(BF16)
    hn_ext = jnp.concatenate([h_prev, hn, h_next], axis=0)

    def proj(c0, width):
        return jnp.dot(hn, w_ref[:, c0:c0 + width], preferred_element_type=F32)

    rows = PROJ_TM + 2 * PROJ_HALO
    for j in range(D_UH // PROJ_COLS):
        cols = slice(j * PROJ_COLS, (j + 1) * PROJ_COLS)
        u = jnp.dot(hn_ext, w_ref[:, cols], preferred_element_type=F32)
        prev = pltpu.roll(u, 1, axis=0)[PROJ_HALO:PROJ_HALO + PROJ_TM]
        nxt = pltpu.roll(u, rows - 1, axis=0)[PROJ_HALO:PROJ_HALO + PROJ_TM]
        cur = u[PROJ_HALO:PROJ_HALO + PROJ_TM]
        uc = (prev * cw_ref[0:1, cols] + cur * cw_ref[1:2, cols] + nxt * cw_ref[2:3, cols] + cw_ref[3:4, cols])
        for c in range(PROJ_COLS // LANES):
            uh_ref[0, j * (PROJ_COLS // LANES) + c] = uc[:, c * LANES:(c + 1) * LANES].astype(BF16)

    def rope(xc):
        return (xc * c_ref[...] + pltpu.roll(xc, LANES - ROT_DIM // 2, axis=1) * su_ref[...]
                + pltpu.roll(xc, ROT_DIM // 2, axis=1) * sd_ref[...])

    qkv = proj(D_UH, D_ATTN + 2 * D_KV)
    for j in range(D_ATTN // LANES):
        qc = qkv[:, j * LANES:(j + 1) * LANES]
        q_ref[0, :, j * LANES:(j + 1) * LANES] = (rope(qc) * Q_SCALE).astype(BF16)
    lo = lax.broadcasted_iota(jnp.int32, (PROJ_TM, LANES), 1) < HEAD_DIM
    for val, ref in ((rope(qkv[:, D_ATTN:D_ATTN + D_KV]), k_ref), (qkv[:, D_ATTN + D_KV:], v_ref)):
        rolled = pltpu.roll(val, HEAD_DIM, axis=1)
        ref[0, 0] = jnp.where(lo, val, 0.0).astype(BF16)
        ref[0, 1] = jnp.where(lo, 0.0, rolled).astype(BF16)
        ref[0, 2] = jnp.where(lo, rolled, 0.0).astype(BF16)
        ref[0, 3] = jnp.where(lo, 0.0, val).astype(BF16)
    g0 = D_UH + D_ATTN + 2 * D_KV
    for j in range(2 * D_MODEL // PROJ_COLS):
        cols = slice(j * PROJ_COLS, (j + 1) * PROJ_COLS)
        gate_ref[0, :, cols] = jax.nn.sigmoid(proj(g0 + j * PROJ_COLS, PROJ_COLS)).astype(BF16)


def _proj_call(x, batch0, total, stacked, g1, w_in, conv_w, rope_c, rope_su, rope_sd):
    b = x.shape[0]
    nt = SEQ // PROJ_TM
    per_tile = PROJ_TM // PROJ_HALO
    tile = lambda w: pl.BlockSpec((1, PROJ_TM, w), lambda bi, i: (bi, i, 0))
    halo_prev = pl.BlockSpec((1, PROJ_HALO, D_MODEL), lambda bi, i: (bi, jnp.maximum(i * per_tile - 1, 0), 0))
    halo_next = pl.BlockSpec((1, PROJ_HALO, D_MODEL),
                             lambda bi, i: (bi, jnp.minimum((i + 1) * per_tile, SEQ // PROJ_HALO - 1), 0))
    rope_spec = pl.BlockSpec((PROJ_TM, LANES), lambda bi, i: (i, 0))
    out_tile = lambda w: pl.BlockSpec((1, PROJ_TM, w), lambda bi, i: (batch0 + bi, i, 0))
    out_slabs = lambda n: pl.BlockSpec((1, n, PROJ_TM, LANES), lambda bi, i: (batch0 + bi, 0, i, 0))
    in_specs = [tile(D_MODEL), halo_prev, halo_next, _const_spec((1, D_MODEL)), _const_spec((D_MODEL, D_IN)),
                _const_spec((8, D_UH)), rope_spec, rope_spec, rope_spec]
    operands = [x, x, x, g1, w_in, conv_w, rope_c, rope_su, rope_sd]
    aliases = {}
    if stacked is not None:
        aliases = {len(operands) + n: n for n in range(len(stacked))}
        in_specs += [pl.BlockSpec(memory_space=pl.ANY)] * len(stacked)
        operands += list(stacked)
    return pl.pallas_call(
        _proj_kernel,
        grid=(b, nt),
        in_specs=in_specs,
        out_specs=[out_slabs(D_UH // LANES), out_tile(D_ATTN), out_slabs(2 * N_KV_HEADS),
                   out_slabs(2 * N_KV_HEADS), out_tile(2 * D_MODEL)],
        out_shape=[jax.ShapeDtypeStruct((total, D_UH // LANES, SEQ, LANES), BF16),
                   jax.ShapeDtypeStruct((total, SEQ, D_ATTN), BF16),
                   jax.ShapeDtypeStruct((total, 2 * N_KV_HEADS, SEQ, LANES), BF16),
                   jax.ShapeDtypeStruct((total, 2 * N_KV_HEADS, SEQ, LANES), BF16),
                   jax.ShapeDtypeStruct((total, SEQ, 2 * D_MODEL), BF16)],
        input_output_aliases=aliases,
        compiler_params=_cparams(("parallel", "parallel")),
        name="proj",
    )(*operands)


FILT_TL = 512


def _filter_kernel(z_ref, w0_ref, b0_ref, wi_ref, bi_ref, wo_ref, fr_ref, dl_ref, ride_src, h_ref, ride_dst):
    _cast_rider(pl.program_id(0), pl.num_programs(0), [ride_src], [ride_dst])
    hi = lax.Precision.HIGHEST
    half = FILT_TL // 2
    z = z_ref[...]
    z2 = jnp.concatenate([z[:half], z[half:]], axis=1)
    fr = fr_ref[...]
    h = jnp.sin(fr * (jnp.dot(z2, w0_ref[...], precision=hi, preferred_element_type=F32) + b0_ref[...]))
    for i in range(wi_ref.shape[0]):
        h = jnp.sin(fr * (jnp.dot(h, wi_ref[i], precision=hi, preferred_element_type=F32) + bi_ref[i]))
    def split(v):
        top = v.astype(BF16)
        return top, (v - top.astype(F32)).astype(BF16)
    h_hi, h_lo = split(h)
    w_hi, w_lo = split(wo_ref[...])
    out = (jnp.dot(h_hi, w_hi, preferred_element_type=F32) + jnp.dot(h_hi, w_lo, preferred_element_type=F32)
           + jnp.dot(h_lo, w_hi, preferred_element_type=F32))
    for r in range(2):
        rows = slice(r * half, (r + 1) * half)
        t = z[rows, 0:1]
        h_ref[rows, :] = out[:, r * N_FILT_CH:(r + 1) * N_FILT_CH] * jnp.exp(-t * dl_ref[...])


def _block_diag2(w):
    zeros = jnp.zeros_like(w)
    return jnp.concatenate([jnp.concatenate([w, zeros], axis=-1), jnp.concatenate([zeros, w], axis=-1)], axis=-2)


def _filter_call(z, w0, b0, wi, bi, wo, fr, deltas, rider):
    n_inner = wi.shape[0]
    twice = lambda v: jnp.concatenate([v, v], axis=-1)
    w0, wi, wo = _block_diag2(w0), _block_diag2(wi), _block_diag2(wo)
    b0, bi, fr = twice(b0), twice(bi), twice(fr)
    wide = 2 * FILTER_ORDER
    steps = SEQ // FILT_TL
    ride_in, = _cast_rider_specs([rider], steps, lambda i: i)
    ride_out, = _cast_rider_specs([rider], steps, lambda i: i)
    return pl.pallas_call(
        _filter_kernel,
        grid=(steps,),
        in_specs=[pl.BlockSpec((FILT_TL, LANES), lambda i: (i, 0)),
                  _const_spec((2 * LANES, wide)), _const_spec((1, wide)),
                  _const_spec((n_inner, wide, wide)), _const_spec((n_inner, 1, wide)),
                  _const_spec((wide, 2 * N_FILT_CH)), _const_spec((1, wide)),
                  _const_spec((1, N_FILT_CH)), ride_in],
        out_specs=[pl.BlockSpec((FILT_TL, N_FILT_CH), lambda i: (i, 0)), ride_out],
        out_shape=[jax.ShapeDtypeStruct((SEQ, N_FILT_CH), F32), jax.ShapeDtypeStruct(rider.shape, BF16)],
        compiler_params=_cparams(("arbitrary",)),
        name="filt",
    )(z, w0, b0, wi, bi, wo, fr, deltas, rider)


STAGE_UNROLL = 64
KB = 2
MID_UNROLL = 64


def _stage_f1(z_ref, a_ref, f1_ref, real_only):
    def body(n2, carry):
        zr = z_ref[0, pl.ds(n2, H1, stride=Z_STRIDE), :]
        if real_only:
            res = jnp.dot(f1_ref[n2][:, :H1], zr.astype(BF16), preferred_element_type=F32)
        else:
            zi = z_ref[1, pl.ds(n2, H1, stride=Z_STRIDE), :]
            st = jnp.concatenate([zr, zi], axis=0).astype(BF16)
            res = jnp.dot(f1_ref[n2], st, preferred_element_type=F32)
        base = n2 * A_STRIDE
        a_ref[0, pl.ds(base, N1), :] = res[:N1]
        a_ref[1, pl.ds(base, N1), :] = res[N1:]
        return carry
    lax.fori_loop(0, N2, body, 0, unroll=STAGE_UNROLL)


def _dft_blocks(mat_ref, vr, vi, interleaved):
    st = jnp.concatenate([vr, vi], axis=0).astype(BF16)
    x = jnp.dot(mat_ref[...], st, preferred_element_type=F32)
    if not interleaved:
        return x[:N2], x[N2:]
    x4 = x.reshape(N2 // SUBLANES, 2, SUBLANES, x.shape[-1])
    return x4[:, 0].reshape(N2, x.shape[-1]), x4[:, 1].reshape(N2, x.shape[-1])


def _load_spectrum_blocks(a_ref, f2_ref, k1b):
    ar = jnp.concatenate([a_ref[0, pl.ds(k1b + i, N2, stride=A_STRIDE), :] for i in range(KB)], axis=1)
    ai = jnp.concatenate([a_ref[1, pl.ds(k1b + i, N2, stride=A_STRIDE), :] for i in range(KB)], axis=1)
    return _dft_blocks(f2_ref, ar, ai, True)


def _spectrum_rows(k1b, i):
    return pl.ds(pl.multiple_of((k1b + i) * N2, N2), N2)


def _kf_kernel(hf_ref, hb_ref, b_ref, f1_ref, f2_ref, kr_ref, ki_ref, z_ref, ab_ref, acc_ref):
    inv_n = 1.0 / N_FFT

    def load_time(src_ref, backward):
        def body(j, carry):
            rows = pl.multiple_of(j * N2, N2)
            v = src_ref[pl.ds(rows, N2), :]
            ridx = lax.broadcasted_iota(jnp.int32, (N2, CB), 0) + rows
            v = jnp.where(ridx == 0, 0.0 if backward else v + b_ref[...], v)
            z_ref[0, pl.ds(j * Z_STRIDE, N2), :] = v
            return carry
        lax.fori_loop(0, H1, body, 0, unroll=4)

    def spectrum(accumulate):
        def body(it, carry):
            k1b = it * KB
            xr, xi = _load_spectrum_blocks(ab_ref, f2_ref, k1b)
            for i in range(KB):
                rows = _spectrum_rows(k1b, i)
                lanes = slice(i * CB, (i + 1) * CB)
                if accumulate:
                    kr_ref[rows, :] = (acc_ref[0, rows, :] + xr[:, lanes] * inv_n).astype(BF16)
                    ki_ref[rows, :] = (acc_ref[1, rows, :] - xi[:, lanes] * inv_n).astype(BF16)
                else:
                    acc_ref[0, rows, :] = xr[:, lanes] * inv_n
                    acc_ref[1, rows, :] = xi[:, lanes] * inv_n
            return carry
        lax.fori_loop(0, N1 // KB, body, 0, unroll=MID_UNROLL)

    load_time(hf_ref, False)
    _stage_f1(z_ref, ab_ref, f1_ref, True)
    spectrum(False)
    load_time(hb_ref, True)
    _stage_f1(z_ref, ab_ref, f1_ref, True)
    spectrum(True)


def _kf_call(h, bias, f1, f2):
    blocks_per_order = N_CB
    bias_spec = pl.BlockSpec((None, 1, CB), lambda o, c: (o, 0, c))
    fwd = pl.BlockSpec((SEQ, CB), lambda o, c: (0, o * blocks_per_order + c))
    bwd = pl.BlockSpec((SEQ, CB), lambda o, c: (0, (HYENA_ORDER + o) * blocks_per_order + c))
    out = pl.BlockSpec((None, N_FFT, CB), lambda o, c: (o, 0, c))
    return pl.pallas_call(
        _kf_kernel,
        grid=(HYENA_ORDER, N_CB),
        in_specs=[fwd, bwd, bias_spec, _const_spec((N2, 2 * N1, 2 * H1)), _const_spec((2 * N2, 2 * N2))],
        out_specs=[out, out],
        out_shape=[jax.ShapeDtypeStruct((HYENA_ORDER, N_FFT, D_HYENA), BF16)] * 2,
        scratch_shapes=[pltpu.VMEM((2, H1 * Z_STRIDE, CB), F32), pltpu.VMEM((2, N2 * A_STRIDE, CB), F32),
                        pltpu.VMEM((2, N_FFT, CB), F32)],
        compiler_params=_cparams(("parallel", "parallel")),
        name="kf",
    )(h, h, bias, f1, f2)


TIME_ROWS = 256
GROUPS = TIME_ROWS // N2


def _group_rows(j, i):
    return pl.ds((j * GROUPS + i) * Z_STRIDE, N2)


def _hyena_kernel(v_ref, x1_ref, x2_ref, kr_ref, ki_ref, f1_ref, f2_ref, i1_ref, i2_ref, o_ref, z_ref, s_ref):
    def fill(j, carry):
        rows = pl.ds(pl.multiple_of(j * TIME_ROWS, TIME_ROWS), TIME_ROWS)
        for r in range(2):
            val = v_ref[r, rows, :].astype(F32)
            for i in range(GROUPS):
                z_ref[r, _group_rows(j, i), :] = val[i * N2:(i + 1) * N2]
        return carry
    lax.fori_loop(0, SEQ // TIME_ROWS, fill, 0, unroll=2)

    for order in range(HYENA_ORDER):
        _stage_f1(z_ref, s_ref, f1_ref, False)

        def mid(it, carry, order=order):
            k1b = it * KB
            xr, xi = _load_spectrum_blocks(s_ref, f2_ref, k1b)
            kr = jnp.concatenate([kr_ref[order, _spectrum_rows(k1b, i), :] for i in range(KB)], axis=1).astype(F32)
            ki = jnp.concatenate([ki_ref[order, _spectrum_rows(k1b, i), :] for i in range(KB)], axis=1).astype(F32)
            br, bi = _dft_blocks(i1_ref, xr * kr - xi * ki, xr * ki + xi * kr, False)
            for i in range(KB):
                lanes = slice(i * CB, (i + 1) * CB)
                s_ref[0, pl.ds(k1b + i, N2, stride=A_STRIDE), :] = br[:, lanes]
                s_ref[1, pl.ds(k1b + i, N2, stride=A_STRIDE), :] = bi[:, lanes]
            return carry
        lax.fori_loop(0, N1 // KB, mid, 0, unroll=MID_UNROLL)

        def last(n2, carry):
            base = n2 * A_STRIDE
            st = jnp.concatenate([s_ref[0, pl.ds(base, N1), :], s_ref[1, pl.ds(base, N1), :]], axis=1).astype(BF16)
            p = jnp.dot(i2_ref[n2], st, preferred_element_type=F32)
            z_ref[0, pl.ds(n2, H1, stride=Z_STRIDE), :] = p[:H1, :CB] + p[H1:, CB:]
            z_ref[1, pl.ds(n2, H1, stride=Z_STRIDE), :] = p[:H1, CB:] - p[H1:, :CB]
            return carry
        lax.fori_loop(0, N2, last, 0, unroll=STAGE_UNROLL)

        def gate(j, carry, order=order):
            rows = pl.ds(pl.multiple_of(j * TIME_ROWS, TIME_ROWS), TIME_ROWS)
            for r in range(2):
                if order + 1 < HYENA_ORDER:
                    g = x1_ref[r, rows, :].astype(F32)
                    for i in range(GROUPS):
                        z_ref[r, _group_rows(j, i), :] = g[i * N2:(i + 1) * N2] * z_ref[r, _group_rows(j, i), :]
                else:
                    y = jnp.concatenate([z_ref[r, _group_rows(j, i), :] for i in range(GROUPS)], axis=0)
                    o_ref[r, rows, :] = (x2_ref[r, rows, :].astype(F32) * y).astype(BF16)
            return carry
        lax.fori_loop(0, SEQ // TIME_ROWS, gate, 0, unroll=2)


def _hyena_call(uh, kr, ki, tables):
    assert HYENA_ORDER == 2
    b = uh.shape[0]
    f1, f2, i1, i2 = tables
    io = lambda slot: pl.BlockSpec((2, None, SEQ, CB), lambda c, p: (p, slot + c, 0, 0))
    kspec = pl.BlockSpec((HYENA_ORDER, N_FFT, CB), lambda c, p: (0, 0, c))
    return pl.pallas_call(
        _hyena_kernel,
        grid=(N_CB, b // 2),
        in_specs=[io(0), io(N_CB), io(2 * N_CB), kspec, kspec,
                  _const_spec((N2, 2 * N1, 2 * H1)), _const_spec((2 * N2, 2 * N2)),
                  _const_spec((2 * N2, 2 * N2)), _const_spec((N2, 2 * H1, N1))],
        out_specs=io(0),
        out_shape=jax.ShapeDtypeStruct((b, N_CB, SEQ, CB), BF16),
        scratch_shapes=[pltpu.VMEM((2, H1 * Z_STRIDE, CB), F32), pltpu.VMEM((2, N2 * A_STRIDE, CB), F32)],
        compiler_params=_cparams(("parallel", "parallel")),
        name="hyena",
    )(uh, uh, uh, kr, ki, f1, f2, i1, i2)


ATT_TQ = 128
ATT_KW = ATT_TQ + 2 * WINDOW


ATT_QB = 16
ATT_RIDER_CHUNKS = 16


def _attn_kernel(sink_ref, q_ref, k_ref, v_ref, *refs):
    n_riders = (len(refs) - 1) // 2
    o_ref = refs[n_riders]
    _cast_rider(pl.program_id(0) * pl.num_programs(1) + pl.program_id(1), ATT_RIDER_CHUNKS,
                refs[:n_riders], refs[n_riders + 1:])
    lo_q = lax.broadcasted_iota(jnp.int32, (ATT_TQ, LANES), 1) < HEAD_DIM
    pairs = [(h, j) for h in range(N_KV_HEADS) for j in range(2)]

    def block_setup(qb):
        i = pl.program_id(1) * ATT_QB + qb
        start = pl.multiple_of(jnp.clip(i * ATT_TQ - WINDOW, 0, SEQ - ATT_KW), LANES)
        qpos = i * ATT_TQ + lax.broadcasted_iota(jnp.int32, (ATT_TQ, ATT_KW), 0)
        kpos = start + lax.broadcasted_iota(jnp.int32, (ATT_TQ, ATT_KW), 1)
        bias = jnp.where(jnp.abs(qpos - kpos) <= WINDOW, 0.0, NEG_INF).astype(BF16)
        return pl.ds(start, ATT_KW), slice(qb * ATT_TQ, (qb + 1) * ATT_TQ), bias

    def score_stage(win, qrows):
        out = []
        for h, j in pairs:
            kbd = jnp.concatenate([k_ref[0, 2 * h, win, :], k_ref[0, 2 * h + 1, win, :]], axis=0)
            q2 = q_ref[0, qrows, LANES * (2 * h + j):LANES * (2 * h + j + 1)]
            out.append(lax.dot_general(q2, kbd, (((1,), (1,)), ((), ())), preferred_element_type=F32))
        return out

    setups = [block_setup(qb) for qb in range(ATT_QB)]
    scores = score_stage(setups[0][0], setups[0][1])
    for qb in range(ATT_QB):
        win, qrows, bias = setups[qb]
        nxt = score_stage(setups[qb + 1][0], setups[qb + 1][1]) if qb + 1 < ATT_QB else None
        probs, scales = [], []
        for (h, j), s in zip(pairs, scores):
            ps, inv = [], []
            for e in range(2):
                sk = sink_ref[4 * h + 2 * j + e] * LOG2E
                se = s[:, e * ATT_KW:(e + 1) * ATT_KW].astype(BF16) + bias
                m = jnp.maximum(jnp.max(se, axis=-1, keepdims=True).astype(F32), sk)
                p = jnp.exp2(se - m.astype(BF16))
                part = functools.reduce(lambda a, b: a + b,
                                        [p[:, c * LANES:(c + 1) * LANES] for c in range(ATT_KW // LANES)])
                inv.append(1.0 / (jnp.sum(part.astype(F32), axis=-1, keepdims=True) + jnp.exp2(sk - m)))
                ps.append(p)
            probs.append(jnp.concatenate(ps, axis=1))
            scales.append(jnp.where(lo_q, inv[0], inv[1]))
        for (h, j), p2, sc in zip(pairs, probs, scales):
            vbd = jnp.concatenate([v_ref[0, 2 * h, win, :], v_ref[0, 2 * h + 1, win, :]], axis=0)
            o2 = jnp.dot(p2, vbd, preferred_element_type=F32)
            c0 = LANES * (2 * h + j)
            o_ref[0, qrows, c0:c0 + LANES] = (o2 * sc).astype(BF16)
        scores = nxt


def _attn_call(sink, q, k, v, riders):
    b = q.shape[0]
    nt = SEQ // (ATT_TQ * ATT_QB)
    assert not riders or b * nt >= ATT_RIDER_CHUNKS
    full = pl.BlockSpec((1, 2 * N_KV_HEADS, SEQ, LANES), lambda bi, i: (bi, 0, 0, 0))
    tile = pl.BlockSpec((1, ATT_TQ * ATT_QB, D_ATTN), lambda bi, i: (bi, i, 0))
    flat = lambda bi, i: bi * nt + i
    out = pl.pallas_call(
        _attn_kernel,
        grid=(b, nt),
        in_specs=[pl.BlockSpec(memory_space=pltpu.SMEM), tile, full, full,
                  *_cast_rider_specs(riders, ATT_RIDER_CHUNKS, flat)],
        out_specs=[tile, *_cast_rider_specs(riders, ATT_RIDER_CHUNKS, flat)],
        out_shape=[jax.ShapeDtypeStruct((b, SEQ, D_ATTN), BF16),
                   *[jax.ShapeDtypeStruct(w.shape, BF16) for w in riders]],
        compiler_params=_cparams(("arbitrary", "arbitrary")),
        name="attn",
    )(sink, q, k, v, *riders)
    return out[0], out[1:]


MIX_TM = 512
MIX_SPLIT = 2
MXU_COLS = 256
FF_SPLITS = (0, 6 * MXU_COLS, D_FF)


def _rms(x, g):
    return x * lax.rsqrt(jnp.mean(x * x, axis=-1, keepdims=True) + EPS) * g


def _mix_kernel(x_ref, yh_ref, ya_ref, gate_ref, wuh_ref, wua_ref, wo_ref, g2_ref, wg_ref, wu_ref, wd_ref,
                gf_ref, o_ref):
    groups = [slice(r * (MIX_TM // MIX_SPLIT), (r + 1) * (MIX_TM // MIX_SPLIT)) for r in range(MIX_SPLIT)]
    ups = []
    for rows in groups:
        yh = jnp.concatenate([yh_ref[0, c, rows, :] for c in range(N_CB)], axis=1)
        ups.append((jnp.dot(yh, wuh_ref[...], preferred_element_type=F32),
                    jnp.dot(ya_ref[0, rows, :], wua_ref[...], preferred_element_type=F32)))
    merged = [(gate_ref[0, rows, :D_MODEL].astype(F32) * up_h
               + gate_ref[0, rows, D_MODEL:].astype(F32) * up_a).astype(BF16)
              for rows, (up_h, up_a) in zip(groups, ups)]
    x1s = [x_ref[0, rows, :] + jnp.dot(m, wo_ref[...], preferred_element_type=F32)
           for rows, m in zip(groups, merged)]
    hns = [_rms(x1, g2_ref[...]).astype(BF16) for x1 in x1s]
    accs = list(x1s)
    for c in range(len(FF_SPLITS) - 1):
        cols = slice(FF_SPLITS[c], FF_SPLITS[c + 1])
        gates = [jnp.dot(hn, wg_ref[:, cols], preferred_element_type=F32) for hn in hns]
        ups = [jnp.dot(hn, wu_ref[:, cols], preferred_element_type=F32) for hn in hns]
        acts = [(jax.nn.silu(g) * u).astype(BF16) for g, u in zip(gates, ups)]
        accs = [acc + jnp.dot(act, wd_ref[cols, :], preferred_element_type=F32) for acc, act in zip(accs, acts)]
    for rows, acc in zip(groups, accs):
        o_ref[0, rows, :] = _rms(acc, gf_ref[...])


def _mix_call(x, batch0, yh, ya, gates, wuh, wua, wo, g2, wg, wu, wd, gf):
    b = x.shape[0]
    tile = lambda w: pl.BlockSpec((1, MIX_TM, w), lambda bi, i: (bi, i, 0))
    stacked_tile = lambda w: pl.BlockSpec((1, MIX_TM, w), lambda bi, i: (batch0 + bi, i, 0))
    return pl.pallas_call(
        _mix_kernel,
        grid=(b, SEQ // MIX_TM),
        in_specs=[tile(D_MODEL), pl.BlockSpec((1, N_CB, MIX_TM, CB), lambda bi, i: (batch0 + bi, 0, i, 0)),
                  stacked_tile(D_ATTN), stacked_tile(2 * D_MODEL),
                  _const_spec((D_HYENA, D_MODEL)), _const_spec((D_ATTN, D_MODEL)),
                  _const_spec((D_MODEL, D_MODEL)), _const_spec((1, D_MODEL)),
                  _const_spec((D_MODEL, D_FF)), _const_spec((D_MODEL, D_FF)), _const_spec((D_FF, D_MODEL)),
                  _const_spec((1, D_MODEL))],
        out_specs=tile(D_MODEL),
        out_shape=jax.ShapeDtypeStruct((b, SEQ, D_MODEL), F32),
        compiler_params=_cparams(("parallel", "parallel")),
        name="mix",
    )(x, yh, ya, gates, wuh, wua, wo, g2, wg, wu, wd, gf)


def _layers(xs, tables, rope, kf, norm1_g, w_in, short_w, short_b, sink_logit, w_up_hyena,
            w_up_attn, w_out, norm2_g, w_ff_gate, w_ff_up, w_ff_down, final_g):
    kr, ki = kf
    total = sum(x.shape[0] for x in xs)
    conv_w = jnp.concatenate([short_w, short_b[None, :], jnp.zeros((4, D_UH), F32)], axis=0)
    stacked, offsets, off = None, [], 0
    for x in xs:
        stacked = _proj_call(x, off, total, stacked, norm1_g[None, :], w_in, conv_w, *rope)
        offsets.append(off)
        off += x.shape[0]
    uh, q, k, v, gates = stacked
    yh = _hyena_call(uh, kr, ki, tables)
    ya, (wuh, wua, wo, wg, wu, wd) = _attn_call(sink_logit, q, k, v,
                                                (w_up_hyena, w_up_attn, w_out, w_ff_gate, w_ff_up, w_ff_down))
    weights = (wuh, wua, wo, norm2_g[None, :], wg, wu, wd, final_g[None, :])
    return tuple(_mix_call(x, o, yh, ya, gates, *weights) for x, o in zip(xs, offsets))


def kernel(x_prompt, x_sample, norm1_g, w_in, short_w, short_b, filt_w0, filt_b0, filt_w_inner, filt_b_inner,
           filt_w_out, filt_freq, hyena_bias, sink_logit, w_up_hyena, w_up_attn, w_out, norm2_g, w_ff_gate,
           w_ff_up, w_ff_down, final_g):
    tables = _dft_tables()
    rope = _rope_tables()
    z_emb, deltas = _filter_embedding()
    w0 = jnp.pad(filt_w0[0], ((0, LANES - FILTER_EMB), (0, 0)))
    h, w_in_bf16 = _filter_call(z_emb, w0, filt_b0[0][None, :], filt_w_inner[0], filt_b_inner[0][:, None, :],
                                filt_w_out[0], filt_freq[0][None, :], deltas, w_in[0])
    kf = _kf_call(h, hyena_bias[0][:, None, :], tables[0], tables[1])
    args = (norm1_g[0], w_in_bf16, short_w[0], short_b[0], sink_logit[0], w_up_hyena[0],
            w_up_attn[0], w_out[0], norm2_g[0], w_ff_gate[0], w_ff_up[0], w_ff_down[0], final_g)
    y_prompt, y_sample = _layers((x_prompt, x_sample), tables, rope, kf, *args)
    return (y_prompt, y_sample)
```
